```python
import math
import jax
import jax.numpy as jnp
from jax import lax
import numpy as np

D_MODEL = 2048
BATCH = 8
SEQ = 2048
DEPTH = 2
DEC_BATCH = 128
DEC_SEQ = 8
PAST_LEN = 2048
PAGE_SIZE = 128

N_HEADS = 8
HEAD_DIM = 64
K_DIM = 2 * HEAD_DIM
V_DIM = 2 * HEAD_DIM
QK_WIDTH = N_HEADS * K_DIM
ATTN_WIDTH = N_HEADS * V_DIM
SSM_WIDTH = D_MODEL // 2
GROUP_CH = 16
N_GROUPS = SSM_WIDTH // GROUP_CH
STATE_DIM = 64
DT_MIN = 1e-3
DT_MAX = 1e-1
IN_COLS = 2 * QK_WIDTH + ATTN_WIDTH + SSM_WIDTH + 2 * D_MODEL
D_FF = 5632
N_EXPERTS = 8
TOP_K = 2
D_FF_EXPERT = 2816
Q_BLOCK = 128
EPS = 1e-6
NEG_INF = -1e30
POOL_NUM = 5
POOL_DEN = 4

kernel_name = 'hybrid_diffattn_s5_moe_step'


def rmsnorm(x, g):
    xf = x.astype(jnp.float32)
    y = xf * lax.rsqrt(jnp.mean(xf * xf, axis=-1, keepdims=True) + EPS)
    return (y * g.astype(jnp.float32)).astype(x.dtype)


def lambda_init(layer):
    return 0.8 - 0.6 * math.exp(-0.3 * layer)


def swiglu(x, w_gate, w_up, w_down):
    return (jax.nn.silu(x @ w_gate) * (x @ w_up)) @ w_down


def diff_weights(s, lam):
    p = jax.nn.softmax(s, axis=-1)
    return p[:, :, 0] - lam * p[:, :, 1]


def attn_prompt(q, k, v, lam):
    b, l = q.shape[:2]
    n_blocks = l // Q_BLOCK
    scale = HEAD_DIM ** -0.5
    q_blocks = q.reshape(b, n_blocks, Q_BLOCK, N_HEADS, 2, HEAD_DIM).transpose(1, 0, 2, 3, 4, 5)
    k_pos = jnp.arange(l)

    def one_block(args):
        q_blk, i = args
        s = jnp.einsum('bqhmd,bkhmd->bhmqk', q_blk, k).astype(jnp.float32) * scale
        q_pos = i * Q_BLOCK + jnp.arange(Q_BLOCK)
        s = jnp.where(k_pos[None, :] <= q_pos[:, None], s, NEG_INF)
        w = diff_weights(s, lam).astype(v.dtype)
        return jnp.einsum('bhqk,bkhe->bqhe', w, v)

    out = lax.map(one_block, (q_blocks, jnp.arange(n_blocks)))
    return out.transpose(1, 0, 2, 3, 4).reshape(b, l, N_HEADS, V_DIM)


def attn_sample(q, k_new, v_new, k_past, v_past, lam):
    t = q.shape[1]
    past = k_past.shape[1]
    scale = HEAD_DIM ** -0.5
    s_past = jnp.einsum('bqhmd,bkhmd->bhmqk', q, k_past).astype(jnp.float32) * scale
    s_new = jnp.einsum('bqhmd,bkhmd->bhmqk', q, k_new).astype(jnp.float32) * scale
    causal = jnp.arange(t)[None, :] <= jnp.arange(t)[:, None]
    s_new = jnp.where(causal, s_new, NEG_INF)
    s = jnp.concatenate([s_past, s_new], axis=-1)
    w = diff_weights(s, lam).astype(v_new.dtype)
    return (jnp.einsum('bhqk,bkhe->bqhe', w[..., :past], v_past)
            + jnp.einsum('bhqk,bkhe->bqhe', w[..., past:], v_new))


def ssm_discretize(a_re, a_im, b_re, b_im, log_dt):
    dt = jnp.exp(log_dt)[:, None]
    mag = jnp.exp(dt * a_re)
    ab_re = mag * jnp.cos(dt * a_im)
    ab_im = mag * jnp.sin(dt * a_im)
    den = a_re * a_re + a_im * a_im
    nr = ab_re - 1.0
    coef_re = (nr * a_re + ab_im * a_im) / den
    coef_im = (ab_im * a_re - nr * a_im) / den
    bb_re = coef_re[..., None] * b_re - coef_im[..., None] * b_im
    bb_im = coef_re[..., None] * b_im + coef_im[..., None] * b_re
    return ab_re, ab_im, bb_re, bb_im


def ssm_combine(e1, e2):
    a1r, a1i, b1r, b1i = e1
    a2r, a2i, b2r, b2i = e2
    return (a2r * a1r - a2i * a1i,
            a2r * a1i + a2i * a1r,
            a2r * b1r - a2i * b1i + b2r,
            a2r * b1i + a2i * b1r + b2i)


def ssm_branch(u, h0_re, h0_im, a_re, a_im, b_re, b_im, c_re, c_im, d, log_dt, w_glu):
    b, l = u.shape[:2]
    ab_re, ab_im, bb_re, bb_im = ssm_discretize(a_re, a_im, b_re, b_im, log_dt)
    bu_re = jnp.einsum('blgc,gpc->blgp', u, bb_re)
    bu_im = jnp.einsum('blgc,gpc->blgp', u, bb_im)
    bu_re = bu_re.at[:, 0].add(ab_re * h0_re - ab_im * h0_im)
    bu_im = bu_im.at[:, 0].add(ab_re * h0_im + ab_im * h0_re)
    a_re_b = jnp.broadcast_to(ab_re, bu_re.shape)
    a_im_b = jnp.broadcast_to(ab_im, bu_im.shape)
    _, _, h_re, h_im = lax.associative_scan(ssm_combine, (a_re_b, a_im_b, bu_re, bu_im), axis=1)
    y = (jnp.einsum('blgp,gcp->blgc', h_re, c_re) - jnp.einsum('blgp,gcp->blgc', h_im, c_im)
         + d.reshape(N_GROUPS, GROUP_CH) * u).reshape(b, l, SSM_WIDTH)
    y = jax.nn.gelu(y)
    y = y * jax.nn.sigmoid(y @ w_glu)
    return y, h_re[:, -1], h_im[:, -1]


def mixer_sublayer(x, p, layer, h0_re, h0_im, k_past=None, v_past=None):
    b, l = x.shape[:2]
    xn = rmsnorm(x, p['norm_mix'])
    proj = xn @ p['w_in']
    c1 = QK_WIDTH
    c2 = c1 + QK_WIDTH
    c3 = c2 + ATTN_WIDTH
    c4 = c3 + SSM_WIDTH
    c5 = c4 + D_MODEL
    q, k, v, u, g_a, g_b = jnp.split(proj, [c1, c2, c3, c4, c5], axis=-1)
    q = rmsnorm(q.reshape(b, l, N_HEADS, 2, HEAD_DIM), p['q_norm'])
    k = rmsnorm(k.reshape(b, l, N_HEADS, 2, HEAD_DIM), p['k_norm'])
    v = v.reshape(b, l, N_HEADS, V_DIM)
    lam_0 = lambda_init(layer)
    lam = (jnp.exp(jnp.sum(p['lambda_q1'] * p['lambda_k1']).astype(jnp.float32))
           - jnp.exp(jnp.sum(p['lambda_q2'] * p['lambda_k2']).astype(jnp.float32)) + lam_0)
    if k_past is None:
        o = attn_prompt(q, k, v, lam)
    else:
        o = attn_sample(q, k, v, k_past, v_past, lam)
    o = (rmsnorm(o, p['subln']) * (1.0 - lam_0)).reshape(b, l, ATTN_WIDTH)
    y_ssm, h_re, h_im = ssm_branch(u.reshape(b, l, N_GROUPS, GROUP_CH), h0_re, h0_im,
                                   p['ssm_a_re'], p['ssm_a_im'], p['ssm_b_re'], p['ssm_b_im'],
                                   p['ssm_c_re'], p['ssm_c_im'], p['ssm_d'], p['ssm_log_dt'], p['w_glu'])
    merged = (jax.nn.sigmoid(g_a) * (o @ p['w_proj_attn'])
              + jax.nn.sigmoid(g_b) * (y_ssm @ p['w_proj_ssm']))
    return x + merged @ p['w_out'], k.reshape(b, l, N_HEADS, K_DIM), v, h_re, h_im


def moe_swiglu(x, router_w, w_gate, w_up, w_down):
    probs = jax.nn.softmax((x @ router_w).astype(jnp.float32), axis=-1)
    top_p, top_i = lax.top_k(probs, TOP_K)
    top_p = top_p / jnp.sum(top_p, axis=-1, keepdims=True)
    gates = jnp.sum(jax.nn.one_hot(top_i, N_EXPERTS, dtype=jnp.float32) * top_p[..., None], axis=-2)
    y = jnp.zeros_like(x)
    for e in range(N_EXPERTS):
        y = y + gates[..., e:e + 1].astype(x.dtype) * swiglu(x, w_gate[e], w_up[e], w_down[e])
    return y


def ffn_sublayer(x, g, layer, ffn_w_gate, ffn_w_up, ffn_w_down, router, moe_w_gate, moe_w_up, moe_w_down):
    xn = rmsnorm(x, g)
    j = layer // 2
    if layer % 2 == 0:
        return x + swiglu(xn, ffn_w_gate[j], ffn_w_up[j], ffn_w_down[j])
    return x + moe_swiglu(xn, router[j], moe_w_gate[j], moe_w_up[j], moe_w_down[j])


def setup_inputs(seed: int = 0) -> dict:
    key = jax.random.key(seed)
    keys = iter(jax.random.split(key, 48))
    f32 = jnp.float32

    def nrm(shape, scale):
        return jax.random.normal(next(keys), shape, f32) * scale

    def gain(shape):
        return 1.0 + nrm(shape, 0.02)

    n_pages = PAST_LEN // PAGE_SIZE
    n_pool = (DEC_BATCH * n_pages * POOL_NUM) // POOL_DEN
    n_dense = (DEPTH + 1) // 2
    n_moe = DEPTH // 2
    x_prompt = nrm((BATCH, SEQ, D_MODEL), 1.0)
    x_sample = nrm((DEC_BATCH, DEC_SEQ, D_MODEL), 1.0)
    cache_k = nrm((DEPTH, n_pool, PAGE_SIZE, N_HEADS, K_DIM), 1.0)
    cache_v = nrm((DEPTH, n_pool, PAGE_SIZE, N_HEADS, V_DIM), 1.0)
    state_ssm_re = nrm((DEPTH, DEC_BATCH, N_GROUPS, STATE_DIM), 0.1)
    state_ssm_im = nrm((DEPTH, DEC_BATCH, N_GROUPS, STATE_DIM), 0.1)
    page_table = jax.random.permutation(next(keys), n_pool)[: DEC_BATCH * n_pages]
    page_table = page_table.reshape(DEC_BATCH, n_pages).astype(jnp.int32)
    a_im_init = math.pi * jnp.arange(STATE_DIM, dtype=f32)
    return {
        'x_prompt': x_prompt,
        'x_sample': x_sample,
        'cache_k': cache_k,
        'cache_v': cache_v,
        'state_ssm_re': state_ssm_re,
        'state_ssm_im': state_ssm_im,
        'page_table': page_table,
        'norm_mix': gain((DEPTH, D_MODEL)),
        'w_in': nrm((DEPTH, D_MODEL, IN_COLS), D_MODEL ** -0.5),
        'q_norm': gain((DEPTH, HEAD_DIM)),
        'k_norm': gain((DEPTH, HEAD_DIM)),
        'lambda_q1': nrm((DEPTH, HEAD_DIM), 0.1),
        'lambda_k1': nrm((DEPTH, HEAD_DIM), 0.1),
        'lambda_q2': nrm((DEPTH, HEAD_DIM), 0.1),
        'lambda_k2': nrm((DEPTH, HEAD_DIM), 0.1),
        'subln': gain((DEPTH, V_DIM)),
        'ssm_a_re': -0.5 + nrm((DEPTH, N_GROUPS, STATE_DIM), 0.01),
        'ssm_a_im': a_im_init + nrm((DEPTH, N_GROUPS, STATE_DIM), 0.01),
        'ssm_b_re': nrm((DEPTH, N_GROUPS, STATE_DIM, GROUP_CH), (2 * GROUP_CH) ** -0.5),
        'ssm_b_im': nrm((DEPTH, N_GROUPS, STATE_DIM, GROUP_CH), (2 * GROUP_CH) ** -0.5),
        'ssm_c_re': nrm((DEPTH, N_GROUPS, GROUP_CH, STATE_DIM), STATE_DIM ** -0.5),
        'ssm_c_im': nrm((DEPTH, N_GROUPS, GROUP_CH, STATE_DIM), STATE_DIM ** -0.5),
        'ssm_d': nrm((DEPTH, SSM_WIDTH), 1.0),
        'ssm_log_dt': jax.random.uniform(next(keys), (DEPTH, N_GROUPS), f32,
                                         math.log(DT_MIN), math.log(DT_MAX)),
        'w_glu': nrm((DEPTH, SSM_WIDTH, SSM_WIDTH), SSM_WIDTH ** -0.5),
        'w_proj_attn': nrm((DEPTH, ATTN_WIDTH, D_MODEL), ATTN_WIDTH ** -0.5),
        'w_proj_ssm': nrm((DEPTH, SSM_WIDTH, D_MODEL), SSM_WIDTH ** -0.5),
        'w_out': nrm((DEPTH, D_MODEL, D_MODEL), D_MODEL ** -0.5),
        'norm_ffn': gain((DEPTH, D_MODEL)),
        'ffn_w_gate': nrm((n_dense, D_MODEL, D_FF), D_MODEL ** -0.5),
        'ffn_w_up': nrm((n_dense, D_MODEL, D_FF), D_MODEL ** -0.5),
        'ffn_w_down': nrm((n_dense, D_FF, D_MODEL), D_FF ** -0.5),
        'router': nrm((n_moe, D_MODEL, N_EXPERTS), D_MODEL ** -0.5),
        'moe_w_gate': nrm((n_moe, N_EXPERTS, D_MODEL, D_FF_EXPERT), D_MODEL ** -0.5),
        'moe_w_up': nrm((n_moe, N_EXPERTS, D_MODEL, D_FF_EXPERT), D_MODEL ** -0.5),
        'moe_w_down': nrm((n_moe, N_EXPERTS, D_FF_EXPERT, D_MODEL), D_FF_EXPERT ** -0.5),
    }


def reference(x_prompt, x_sample, cache_k, cache_v, state_ssm_re, state_ssm_im, page_table,
              norm_mix, w_in, q_norm, k_norm, lambda_q1, lambda_k1, lambda_q2, lambda_k2, subln,
              ssm_a_re, ssm_a_im, ssm_b_re, ssm_b_im, ssm_c_re, ssm_c_im, ssm_d, ssm_log_dt, w_glu,
              w_proj_attn, w_proj_ssm, w_out, norm_ffn, ffn_w_gate, ffn_w_up, ffn_w_down,
              router, moe_w_gate, moe_w_up, moe_w_down):
    db, n_pages = page_table.shape
    past = n_pages * PAGE_SIZE
    xp = x_prompt
    xs = x_sample
    h0 = jnp.zeros((x_prompt.shape[0], N_GROUPS, STATE_DIM), x_prompt.dtype)
    kp_l, vp_l, hrp_l, hip_l = [], [], [], []
    ks_l, vs_l, hrs_l, his_l = [], [], [], []
    for layer in range(DEPTH):
        p = {
            'norm_mix': norm_mix[layer], 'w_in': w_in[layer],
            'q_norm': q_norm[layer], 'k_norm': k_norm[layer],
            'lambda_q1': lambda_q1[layer], 'lambda_k1': lambda_k1[layer],
            'lambda_q2': lambda_q2[layer], 'lambda_k2': lambda_k2[layer],
            'subln': subln[layer],
            'ssm_a_re': ssm_a_re[layer], 'ssm_a_im': ssm_a_im[layer],
            'ssm_b_re': ssm_b_re[layer], 'ssm_b_im': ssm_b_im[layer],
            'ssm_c_re': ssm_c_re[layer], 'ssm_c_im': ssm_c_im[layer],
            'ssm_d': ssm_d[layer], 'ssm_log_dt': ssm_log_dt[layer], 'w_glu': w_glu[layer],
            'w_proj_attn': w_proj_attn[layer], 'w_proj_ssm': w_proj_ssm[layer],
            'w_out': w_out[layer],
        }
        k_past = cache_k[layer, page_table].reshape(db, past, N_HEADS, 2, HEAD_DIM)
        v_past = cache_v[layer, page_table].reshape(db, past, N_HEADS, V_DIM)
        xp, kp, vp, hrp, hip = mixer_sublayer(xp, p, layer, h0, h0)
        xs, ks, vs, hrs, his = mixer_sublayer(xs, p, layer, state_ssm_re[layer], state_ssm_im[layer],
                                              k_past, v_past)
        xp = ffn_sublayer(xp, norm_ffn[layer], layer, ffn_w_gate, ffn_w_up, ffn_w_down,
                          router, moe_w_gate, moe_w_up, moe_w_down)
        xs = ffn_sublayer(xs, norm_ffn[layer], layer, ffn_w_gate, ffn_w_up, ffn_w_down,
                          router, moe_w_gate, moe_w_up, moe_w_down)
        kp_l.append(kp)
        vp_l.append(vp)
        hrp_l.append(hrp)
        hip_l.append(hip)
        ks_l.append(ks)
        vs_l.append(vs)
        hrs_l.append(hrs)
        his_l.append(his)
    k_prompt = jnp.stack(kp_l, axis=0)
    v_prompt = jnp.stack(vp_l, axis=0)
    ssm_re_prompt = jnp.stack(hrp_l, axis=0)
    ssm_im_prompt = jnp.stack(hip_l, axis=0)
    k_sample = jnp.stack(ks_l, axis=0)
    v_sample = jnp.stack(vs_l, axis=0)
    ssm_re_sample = jnp.stack(hrs_l, axis=0)
    ssm_im_sample = jnp.stack(his_l, axis=0)
    return (xp, xs, k_prompt, v_prompt, ssm_re_prompt, ssm_im_prompt,
            k_sample, v_sample, ssm_re_sample, ssm_im_sample)
```

```python
import functools
import math

import jax
import jax.numpy as jnp
from jax import lax
from jax.experimental import pallas as pl
from jax.experimental.pallas import tpu as pltpu

F32 = jnp.float32
BF16 = jnp.bfloat16

EPS = 1e-6
NEG_INF = -1e30
HEAD_DIM = 64
HEAD_LANES = 2 * HEAD_DIM
LANES = 128
SUBLANES = 8
GROUP_CH = 16
SSM_BLOCK_CH = 256
TOP_K = 2
VMEM_LIMIT_BYTES = 56 * 1024 * 1024


def _tile(dim, pref):
    t = min(pref, dim)
    while dim % t:
        t //= 2
    return t


def _log2(n):
    assert n & (n - 1) == 0
    return n.bit_length() - 1


def _params(n_axes):
    return pltpu.CompilerParams(dimension_semantics=("arbitrary",) * n_axes,
                                vmem_limit_bytes=VMEM_LIMIT_BYTES)


def _rmsnorm_kernel(x_ref, g_ref, o_ref):
    x = x_ref[...]
    ms = jnp.mean(x * x, axis=-1, keepdims=True)
    o_ref[...] = (x * lax.rsqrt(ms + EPS) * g_ref[...]).astype(o_ref.dtype)


def rmsnorm_cast(x, g, tm):
    t, d = x.shape
    return pl.pallas_call(
        _rmsnorm_kernel,
        grid=(t // tm,),
        in_specs=[pl.BlockSpec((tm, d), lambda i: (i, 0)),
                  pl.BlockSpec((1, d), lambda i: (0, 0))],
        out_specs=pl.BlockSpec((tm, d), lambda i: (i, 0)),
        out_shape=jax.ShapeDtypeStruct((t, d), BF16),
        compiler_params=_params(1),
        name="rmsnorm_cast",
    )(x, g.reshape(1, d))


def _split3(a):
    hi = a.astype(BF16)
    r1 = a - hi.astype(F32)
    mid = r1.astype(BF16)
    lo = (r1 - mid.astype(F32)).astype(BF16)
    return hi, mid, lo


def _rmsnorm_router_kernel(x_ref, g_ref, rw_ref, o_ref, gates_ref, *, n_experts):
    x = x_ref[...]
    ms = jnp.mean(x * x, axis=-1, keepdims=True)
    xn = x * lax.rsqrt(ms + EPS) * g_ref[...]
    o_ref[...] = xn.astype(o_ref.dtype)
    xh, xm, xl = _split3(xn)
    wh, wm, wl = _split3(rw_ref[...])
    dot = functools.partial(jnp.dot, preferred_element_type=F32)
    logits = (dot(xh, wh) + (dot(xh, wm) + dot(xm, wh))
              + (dot(xh, wl) + dot(xm, wm) + dot(xl, wh)))
    lane = lax.broadcasted_iota(jnp.int32, logits.shape, 1).astype(F32)
    logits = jnp.where(lane < n_experts, logits, -jnp.inf)
    m1 = jnp.max(logits, axis=-1, keepdims=True)
    i1 = jnp.min(jnp.where(logits == m1, lane, float(LANES)), axis=-1, keepdims=True)
    rest = jnp.where(lane == i1, -jnp.inf, logits)
    m2 = jnp.max(rest, axis=-1, keepdims=True)
    i2 = jnp.min(jnp.where(rest == m2, lane, float(LANES)), axis=-1, keepdims=True)
    e2 = jnp.exp(m2 - m1)
    g1 = 1.0 / (1.0 + e2)
    g2 = e2 / (1.0 + e2)
    gates_ref[...] = jnp.where(lane == i1, g1, 0.0) + jnp.where(lane == i2, g2, 0.0)


def rmsnorm_router(x, g, router_w, tm):
    t, d = x.shape
    n_experts = router_w.shape[1]
    rw = jnp.zeros((d, LANES), F32).at[:, :n_experts].set(router_w)
    return pl.pallas_call(
        functools.partial(_rmsnorm_router_kernel, n_experts=n_experts),
        grid=(t // tm,),
        in_specs=[pl.BlockSpec((tm, d), lambda i: (i, 0)),
                  pl.BlockSpec((1, d), lambda i: (0, 0)),
                  pl.BlockSpec((d, LANES), lambda i: (0, 0))],
        out_specs=[pl.BlockSpec((tm, d), lambda i: (i, 0)),
                   pl.BlockSpec((tm, LANES), lambda i: (i, 0))],
        out_shape=[jax.ShapeDtypeStruct((t, d), BF16),
                   jax.ShapeDtypeStruct((t, LANES), F32)],
        compiler_params=_params(1),
        name="rmsnorm_router",
    )(x, g.reshape(1, d), rw)


def fused_matmul(name, grid, lhs, terms, extras, outs, epilogue):
    n_lhs, n_terms, n_ex, n_out = len(lhs), len(terms), len(extras), len(outs)
    staged = [k for k, (a, _, _) in enumerate(lhs) if a.dtype != BF16]

    def kernel(*refs):
        lhs_refs = refs[:n_lhs]
        w_refs = refs[n_lhs:n_lhs + n_terms]
        ex_refs = refs[n_lhs + n_terms:n_lhs + n_terms + n_ex]
        out_refs = refs[n_lhs + n_terms + n_ex:n_lhs + n_terms + n_ex + n_out]
        scr_refs = refs[n_lhs + n_terms + n_ex + n_out:]
        pids = [pl.program_id(a) for a in range(len(grid))]
        if staged:
            first = pids[1] == 0
            for p in pids[2:]:
                first = jnp.logical_and(first, p == 0)

            @pl.when(first)
            def _():
                for s, k in enumerate(staged):
                    scr_refs[s][...] = lhs_refs[k][...].astype(BF16)

        accs = []
        for (li, _, _, _), w_ref in zip(terms, w_refs):
            a = scr_refs[staged.index(li)][...] if li in staged else lhs_refs[li][...]
            accs.append(jnp.dot(a, w_ref[...].astype(BF16), preferred_element_type=F32))
        epilogue(accs, ex_refs, out_refs, pids)

    in_specs = ([pl.BlockSpec(bs, im) for _, bs, im in lhs]
                + [pl.BlockSpec(bs, im) for _, _, bs, im in terms]
                + [pl.BlockSpec(bs, im) for _, bs, im in extras])
    args = [a for a, _, _ in lhs] + [w for _, w, _, _ in terms] + [a for a, _, _ in extras]
    scratch = [pltpu.VMEM(tuple(b for b in lhs[k][1] if b is not None), BF16) for k in staged]
    res = pl.pallas_call(
        kernel,
        grid=grid,
        in_specs=in_specs,
        out_specs=[pl.BlockSpec(bs, im) for _, bs, im in outs],
        out_shape=[sd for sd, _, _ in outs],
        scratch_shapes=scratch,
        compiler_params=_params(len(grid)),
        name=name,
    )(*args)
    return res


def _segment_mean64(sq):
    r = lax.broadcasted_iota(jnp.int32, (LANES, LANES), 0) >> _log2(HEAD_DIM)
    c = lax.broadcasted_iota(jnp.int32, (LANES, LANES), 1) >> _log2(HEAD_DIM)
    ones = (r == c).astype(BF16)
    hi, mid, lo = _split3(sq)
    dot = functools.partial(jnp.dot, preferred_element_type=F32)
    cols = []
    for j in range(sq.shape[1] // LANES):
        sl = slice(j * LANES, (j + 1) * LANES)
        cols.append(dot(hi[:, sl], ones) + dot(mid[:, sl], ones) + dot(lo[:, sl], ones))
    return jnp.concatenate(cols, axis=1) * (1.0 / HEAD_DIM)


def _attn_prompt_kernel(lam_ref, q_ref, k_ref, v_ref, g_ref, o_ref, *, tq, out_scale):
    qi = pl.program_id(2)
    lam = lam_ref[0]
    q = q_ref[...].astype(F32)
    lane = lax.broadcasted_iota(jnp.int32, q.shape, 1)
    qs = jnp.concatenate([jnp.where(lane < HEAD_DIM, q, 0.0),
                          jnp.where(lane >= HEAD_DIM, q, 0.0)], axis=0).astype(BF16)
    row = lax.broadcasted_iota(jnp.int32, (2 * tq, tq), 0)
    row = jnp.where(row >= tq, row - tq, row)
    col = lax.broadcasted_iota(jnp.int32, (2 * tq, tq), 1)
    delta = col - row

    def body(kb, carry):
        m, l, acc = carry
        start = pl.multiple_of(kb * tq, tq)
        k = k_ref[pl.ds(start, tq), :].astype(BF16)
        v = v_ref[pl.ds(start, tq), :].astype(BF16)
        s = lax.dot_general(qs, k, (((1,), (1,)), ((), ())), preferred_element_type=F32)
        s = jnp.where(delta <= (qi - kb) * tq, s, NEG_INF)
        m_new = jnp.maximum(m, jnp.max(s, axis=-1, keepdims=True))
        alpha = jnp.exp(m - m_new)
        p = jnp.exp(s - m_new)
        l = alpha * l + jnp.sum(p, axis=-1, keepdims=True)
        acc = alpha * acc + jnp.dot(p.astype(BF16), v, preferred_element_type=F32)
        return m_new, l, acc

    m0 = jnp.full((2 * tq, 1), NEG_INF, F32)
    l0 = jnp.zeros((2 * tq, 1), F32)
    acc0 = jnp.zeros((2 * tq, HEAD_LANES), F32)
    _, l, acc = lax.fori_loop(0, qi + 1, body, (m0, l0, acc0))
    o = acc / l
    o = o[:tq] - lam * o[tq:]
    o = o * lax.rsqrt(jnp.mean(o * o, axis=-1, keepdims=True) + EPS) * g_ref[...] * out_scale
    o_ref[...] = o.astype(o_ref.dtype)


def attn_prompt(lam, qn, kn, vu, subln, out_scale, n_batch, seq, n_heads):
    tq = _tile(seq, 256)
    nq = seq // tq
    return pl.pallas_call(
        functools.partial(_attn_prompt_kernel, tq=tq, out_scale=out_scale),
        grid=(n_batch, n_heads, nq),
        in_specs=[pl.BlockSpec(memory_space=pltpu.SMEM),
                  pl.BlockSpec((tq, HEAD_LANES), lambda b, h, i: (b * nq + i, h)),
                  pl.BlockSpec((seq, HEAD_LANES), lambda b, h, i: (b, h)),
                  pl.BlockSpec((seq, HEAD_LANES), lambda b, h, i: (b, h)),
                  pl.BlockSpec((1, HEAD_LANES), lambda b, h, i: (0, 0))],
        out_specs=pl.BlockSpec((tq, HEAD_LANES), lambda b, h, i: (b * nq + i, h)),
        out_shape=jax.ShapeDtypeStruct((n_batch * seq, n_heads * HEAD_LANES), BF16),
        compiler_params=_params(3),
        name="attn_prompt",
    )(lam.reshape(1), qn, kn, vu, subln.reshape(1, HEAD_LANES))


def _attn_sample_kernel(pt_ref, lam_ref, q_ref, kn_ref, vn_ref, kc_ref, vc_ref, g_ref, o_ref,
                        qrows_scr, m_scr, l_scr, acc_scr, *, n_heads, t_new, page, out_scale):
    del pt_ref
    p = pl.program_id(1)
    n_pages = pl.num_programs(1)
    width = n_heads * HEAD_LANES
    rows = 2 * n_heads * t_new
    nt = (((1,), (1,)), ((), ()))

    @pl.when(p == 0)
    def _():
        q = q_ref[0]
        qt = jnp.concatenate([q] * (2 * n_heads), axis=0)
        r = lax.broadcasted_iota(jnp.int32, (rows, width), 0)
        c = lax.broadcasted_iota(jnp.int32, (rows, width), 1)
        half = r >> _log2(n_heads * t_new)
        head = (r >> _log2(t_new)) & (n_heads - 1)
        keep = jnp.logical_and(c >> _log2(HEAD_LANES) == head, (c >> _log2(HEAD_DIM)) & 1 == half)
        qrows_scr[...] = jnp.where(keep, qt, 0.0).astype(BF16)
        m_scr[...] = jnp.full(m_scr.shape, NEG_INF, F32)
        l_scr[...] = jnp.zeros(l_scr.shape, F32)
        acc_scr[...] = jnp.zeros(acc_scr.shape, F32)

    def update(k, v, mask):
        s = lax.dot_general(qrows_scr[...], k, nt, preferred_element_type=F32)
        if mask is not None:
            s = jnp.where(mask, s, NEG_INF)
        m_old = m_scr[...]
        m_new = jnp.maximum(m_old, jnp.max(s, axis=-1, keepdims=True))
        alpha = jnp.exp(m_old - m_new)
        pe = jnp.exp(s - m_new)
        l_scr[...] = alpha * l_scr[...] + jnp.sum(pe, axis=-1, keepdims=True)
        acc_scr[...] = alpha * acc_scr[...] + jnp.dot(pe.astype(BF16), v, preferred_element_type=F32)
        m_scr[...] = m_new

    update(kc_ref[...].astype(BF16), vc_ref[...].astype(BF16), None)

    @pl.when(p == n_pages - 1)
    def _():
        pad = jnp.zeros((page - t_new, width), F32)
        k_new = jnp.concatenate([kn_ref[0], pad], axis=0).astype(BF16)
        v_new = jnp.concatenate([vn_ref[0], pad], axis=0).astype(BF16)
        r = lax.broadcasted_iota(jnp.int32, (rows, page), 0)
        c = lax.broadcasted_iota(jnp.int32, (rows, page), 1)
        update(k_new, v_new, c <= (r & (t_new - 1)))
        lam = lam_ref[0]
        inv_l = 1.0 / l_scr[...]
        outs = []
        for h in range(n_heads):
            r1 = slice(h * t_new, (h + 1) * t_new)
            r2 = slice((n_heads + h) * t_new, (n_heads + h + 1) * t_new)
            cs = slice(h * HEAD_LANES, (h + 1) * HEAD_LANES)
            o = acc_scr[r1, cs] * inv_l[r1] - lam * (acc_scr[r2, cs] * inv_l[r2])
            o = o * lax.rsqrt(jnp.mean(o * o, axis=-1, keepdims=True) + EPS) * g_ref[...] * out_scale
            outs.append(o)
        o_ref[0] = jnp.concatenate(outs, axis=1).astype(o_ref.dtype)


def attn_sample(lam, page_table, q_s, k_s, v_s, cache_k, cache_v, layer, subln, out_scale, n_heads):
    db, t_new, width = q_s.shape
    page = cache_k.shape[2]
    n_pages = page_table.shape[1]
    rows = 2 * n_heads * t_new
    grid_spec = pltpu.PrefetchScalarGridSpec(
        num_scalar_prefetch=1,
        grid=(db, n_pages),
        in_specs=[pl.BlockSpec(memory_space=pltpu.SMEM),
                  pl.BlockSpec((1, t_new, width), lambda b, p, pt: (b, 0, 0)),
                  pl.BlockSpec((1, t_new, width), lambda b, p, pt: (b, 0, 0)),
                  pl.BlockSpec((1, t_new, width), lambda b, p, pt: (b, 0, 0)),
                  pl.BlockSpec((None, None, page, width), lambda b, p, pt: (layer, pt[b, p], 0, 0)),
                  pl.BlockSpec((None, None, page, width), lambda b, p, pt: (layer, pt[b, p], 0, 0)),
                  pl.BlockSpec((1, HEAD_LANES), lambda b, p, pt: (0, 0))],
        out_specs=pl.BlockSpec((1, t_new, width), lambda b, p, pt: (b, 0, 0)),
        scratch_shapes=[pltpu.VMEM((rows, width), BF16),
                        pltpu.VMEM((rows, 1), F32),
                        pltpu.VMEM((rows, 1), F32),
                        pltpu.VMEM((rows, width), F32)],
    )
    return pl.pallas_call(
        functools.partial(_attn_sample_kernel, n_heads=n_heads, t_new=t_new, page=page,
                          out_scale=out_scale),
        grid_spec=grid_spec,
        out_shape=jax.ShapeDtypeStruct((db, t_new, width), F32),
        compiler_params=_params(2),
        name="attn_sample",
    )(page_table, lam.reshape(1), q_s, k_s, v_s, cache_k, cache_v, subln.reshape(1, HEAD_LANES))


def _ssm_kernel(u_ref, h0re_ref, h0im_ref, are_ref, aim_ref, b_ref, c_ref, d_ref,
                y_ref, hre_ref, him_ref, bu_scr, *, nb, tc):
    c_idx = pl.program_id(1)
    sw = are_ref.shape[1]

    @pl.when(c_idx == 0)
    def _():
        hre_ref[...] = h0re_ref[...]
        him_ref[...] = h0im_ref[...]

    u = u_ref[...]
    bu_scr[...] = jnp.dot(u.astype(BF16), b_ref[...], preferred_element_type=F32)
    a_re = are_ref[...]
    a_im = aim_ref[...]
    for r in range(nb // SUBLANES):
        rs = slice(r * SUBLANES, (r + 1) * SUBLANES)

        def body(t, carry, r=r):
            h_re, h_im = carry
            row = pl.multiple_of(t * nb + r * SUBLANES, SUBLANES)
            n_re = a_re * h_re - a_im * h_im + bu_scr[pl.ds(row, SUBLANES), 0:sw]
            n_im = a_re * h_im + a_im * h_re + bu_scr[pl.ds(row, SUBLANES), sw:2 * sw]
            bu_scr[pl.ds(row, SUBLANES), 0:sw] = n_re
            bu_scr[pl.ds(row, SUBLANES), sw:2 * sw] = n_im
            return n_re, n_im

        h_re, h_im = lax.fori_loop(0, tc, body, (hre_ref[rs, :], him_ref[rs, :]))
        hre_ref[rs, :] = h_re
        him_ref[rs, :] = h_im
    y = jnp.dot(bu_scr[...].astype(BF16), c_ref[...], preferred_element_type=F32) + d_ref[...] * u
    y_ref[...] = jax.nn.gelu(y)


def ssm_scan(u_tm, h0_re, h0_im, a_re, a_im, b_blk, c_blk, d, nb, tc):
    rows, ch = u_tm.shape
    n_gb = ch // SSM_BLOCK_CH
    sw = h0_re.shape[1] // n_gb
    n_chunks = rows // (tc * nb)
    blk_rows = tc * nb
    return pl.pallas_call(
        functools.partial(_ssm_kernel, nb=nb, tc=tc),
        grid=(n_gb, n_chunks),
        in_specs=[pl.BlockSpec((blk_rows, SSM_BLOCK_CH), lambda g, c: (c, g)),
                  pl.BlockSpec((nb, sw), lambda g, c: (0, g)),
                  pl.BlockSpec((nb, sw), lambda g, c: (0, g)),
                  pl.BlockSpec((SUBLANES, sw), lambda g, c: (0, g)),
                  pl.BlockSpec((SUBLANES, sw), lambda g, c: (0, g)),
                  pl.BlockSpec((None, SSM_BLOCK_CH, 2 * sw), lambda g, c: (g, 0, 0)),
                  pl.BlockSpec((None, 2 * sw, SSM_BLOCK_CH), lambda g, c: (g, 0, 0)),
                  pl.BlockSpec((1, SSM_BLOCK_CH), lambda g, c: (0, g))],
        out_specs=[pl.BlockSpec((blk_rows, SSM_BLOCK_CH), lambda g, c: (c, g)),
                   pl.BlockSpec((nb, sw), lambda g, c: (0, g)),
                   pl.BlockSpec((nb, sw), lambda g, c: (0, g))],
        out_shape=[jax.ShapeDtypeStruct((rows, ch), F32),
                   jax.ShapeDtypeStruct(h0_re.shape, F32),
                   jax.ShapeDtypeStruct(h0_im.shape, F32)],
        scratch_shapes=[pltpu.VMEM((blk_rows, 2 * sw), F32)],
        compiler_params=_params(2),
        name="ssm_scan",
    )(u_tm, h0_re, h0_im, a_re, a_im, b_blk, c_blk, d)


def _ssm_weights(a_re, a_im, b_re, b_im, c_re, c_im, log_dt):
    n_groups, state = a_re.shape
    gpb = SSM_BLOCK_CH // GROUP_CH
    n_gb = n_groups // gpb
    dt = jnp.exp(log_dt)[:, None]
    mag = jnp.exp(dt * a_re)
    ab_re = mag * jnp.cos(dt * a_im)
    ab_im = mag * jnp.sin(dt * a_im)
    den = a_re * a_re + a_im * a_im
    nr = ab_re - 1.0
    coef_re = (nr * a_re + ab_im * a_im) / den
    coef_im = (ab_im * a_re - nr * a_im) / den
    bb_re = coef_re[..., None] * b_re - coef_im[..., None] * b_im
    bb_im = coef_re[..., None] * b_im + coef_im[..., None] * b_re
    eye = jnp.eye(gpb, dtype=F32)

    def b_block(bb):
        bb = bb.reshape(n_gb, gpb, state, GROUP_CH)
        return jnp.einsum("bgpc,gh->bgchp", bb, eye).reshape(n_gb, gpb * GROUP_CH, gpb * state)

    def c_block(cc):
        cc = cc.reshape(n_gb, gpb, GROUP_CH, state)
        return jnp.einsum("bgcp,gh->bgphc", cc, eye).reshape(n_gb, gpb * state, gpb * GROUP_CH)

    b_blk = jnp.concatenate([b_block(bb_re), b_block(bb_im)], axis=2).astype(BF16)
    c_blk = jnp.concatenate([c_block(c_re), c_block(-c_im)], axis=1).astype(BF16)
    bc = lambda a: jnp.broadcast_to(a.reshape(1, n_groups * state), (SUBLANES, n_groups * state))
    return bc(ab_re), bc(ab_im), b_blk, c_blk


def _lambda_init(layer):
    return 0.8 - 0.6 * math.exp(-0.3 * layer)


def kernel(x_prompt, x_sample, cache_k, cache_v, state_ssm_re, state_ssm_im, page_table, norm_mix, w_in, q_norm, k_norm, lambda_q1, lambda_k1, lambda_q2, lambda_k2, subln, ssm_a_re, ssm_a_im, ssm_b_re, ssm_b_im, ssm_c_re, ssm_c_im, ssm_d, ssm_log_dt, w_glu, w_proj_attn, w_proj_ssm, w_out, norm_ffn, ffn_w_gate, ffn_w_up, ffn_w_down, router, moe_w_gate, moe_w_up, moe_w_down):
    n_b, seq, d = x_prompt.shape
    db, t_new, _ = x_sample.shape
    depth = w_in.shape[0]
    n_heads = cache_k.shape[3]
    qk_w = n_heads * HEAD_LANES
    ssm_w = w_glu.shape[1]
    n_groups, state = ssm_a_re.shape[1:]
    n_state = n_groups * state
    page = cache_k.shape[2]
    tp = n_b * seq
    ts = db * t_new
    t_all = tp + ts
    assert n_b == SUBLANES and db % SUBLANES == 0 and ssm_w % SSM_BLOCK_CH == 0
    assert w_in.shape[2] == 3 * qk_w + ssm_w + 2 * d and ssm_w == qk_w

    tm = _tile(math.gcd(seq, ts), 1024)
    n_i = t_all // tm
    cache_k = cache_k.reshape(cache_k.shape[0], cache_k.shape[1], page, qk_w)
    cache_v = cache_v.reshape(cache_v.shape[0], cache_v.shape[1], page, qk_w)
    x = jnp.concatenate([x_prompt.reshape(tp, d), x_sample.reshape(ts, d)], axis=0)
    zeros_h0 = jnp.zeros((n_b, n_state), F32)
    scale = HEAD_DIM ** -0.5

    def row_lhs(a):
        return (a, (tm, a.shape[1]), lambda i, j: (i, 0))

    def store_epilogue(fn=None):
        def ep(accs, ex, out, pids):
            out[0][...] = (accs[0] if fn is None else fn(accs[0])).astype(out[0].dtype)
        return ep

    k_out, v_out, hrp_out, hip_out, hrs_out, his_out = [], [], [], [], [], []
    for layer in range(depth):
        lam_0 = _lambda_init(layer)
        lam = (jnp.exp(jnp.sum(lambda_q1[layer] * lambda_k1[layer]))
               - jnp.exp(jnp.sum(lambda_q2[layer] * lambda_k2[layer])) + lam_0).astype(F32)
        xn = rmsnorm_cast(x, norm_mix[layer], _tile(t_all, 512))

        tn = _tile(qk_w, 512)
        w_spec = lambda off: ((None, d, tn), lambda i, j: (layer, 0, j + off))

        def head_norm(gain, mult):
            g_row = jnp.tile(gain, tn // HEAD_DIM).reshape(1, tn)

            def ep(accs, ex, out, pids):
                y = accs[0]
                y = y * lax.rsqrt(_segment_mean64(y * y) + EPS) * ex[0][...]
                out[0][...] = (y * mult).astype(out[0].dtype)
            return ep, (g_row, (1, tn), lambda i, j: (0, 0))

        ep_q, ex_q = head_norm(q_norm[layer], scale)
        qn, = fused_matmul(
            "proj_q", (n_i, qk_w // tn), [row_lhs(xn)], [(0, w_in) + w_spec(0)], [ex_q],
            [(jax.ShapeDtypeStruct((t_all, qk_w), BF16), (tm, tn), lambda i, j: (i, j))], ep_q)
        ep_k, ex_k = head_norm(k_norm[layer], 1.0)
        kn, = fused_matmul(
            "proj_k", (n_i, qk_w // tn), [row_lhs(xn)], [(0, w_in) + w_spec(qk_w // tn)], [ex_k],
            [(jax.ShapeDtypeStruct((t_all, qk_w), F32), (tm, tn), lambda i, j: (i, j))], ep_k)
        vu, = fused_matmul(
            "proj_vu", (n_i, (qk_w + ssm_w) // tn), [row_lhs(xn)],
            [(0, w_in) + w_spec(2 * qk_w // tn)], [],
            [(jax.ShapeDtypeStruct((t_all, qk_w + ssm_w), F32), (tm, tn), lambda i, j: (i, j))],
            store_epilogue())
        gates, = fused_matmul(
            "proj_gates", (n_i, 2 * d // tn), [row_lhs(xn)],
            [(0, w_in) + w_spec((3 * qk_w + ssm_w) // tn)], [],
            [(jax.ShapeDtypeStruct((t_all, 2 * d), F32), (tm, tn), lambda i, j: (i, j))],
            store_epilogue(jax.nn.sigmoid))

        out_scale = 1.0 - lam_0
        o_p = attn_prompt(lam, qn, kn, vu, subln[layer], out_scale, n_b, seq, n_heads)
        k_s = kn[tp:].reshape(db, t_new, qk_w)
        v_all = vu[:, :qk_w]
        v_s = v_all[tp:].reshape(db, t_new, qk_w)
        q_s = qn[tp:].astype(F32).reshape(db, t_new, qk_w)
        o_s = attn_sample(lam, page_table, q_s, k_s, v_s, cache_k, cache_v, layer, subln[layer],
                          out_scale, n_heads)
        o_all = jnp.concatenate([o_p, o_s.reshape(ts, qk_w).astype(BF16)], axis=0)

        a_re, a_im, b_blk, c_blk = _ssm_weights(
            ssm_a_re[layer], ssm_a_im[layer], ssm_b_re[layer], ssm_b_im[layer],
            ssm_c_re[layer], ssm_c_im[layer], ssm_log_dt[layer])
        d_row = ssm_d[layer].reshape(1, ssm_w)
        u_all = vu[:, qk_w:]
        u_p = u_all[:tp].reshape(n_b, seq, ssm_w).transpose(1, 0, 2).reshape(tp, ssm_w)
        u_s = u_all[tp:].reshape(db, t_new, ssm_w).transpose(1, 0, 2).reshape(ts, ssm_w)
        yg_p, hrp, hip = ssm_scan(u_p, zeros_h0, zeros_h0, a_re, a_im, b_blk, c_blk, d_row,
                                  n_b, _tile(seq, 256))
        yg_s, hrs, his = ssm_scan(u_s, state_ssm_re[layer].reshape(db, n_state),
                                  state_ssm_im[layer].reshape(db, n_state),
                                  a_re, a_im, b_blk, c_blk, d_row, db, t_new)
        yg = jnp.concatenate(
            [yg_p.reshape(seq, n_b, ssm_w).transpose(1, 0, 2).reshape(tp, ssm_w),
             yg_s.reshape(t_new, db, ssm_w).transpose(1, 0, 2).reshape(ts, ssm_w)], axis=0)

        def ep_glu(accs, ex, out, pids):
            out[0][...] = (ex[0][...] * jax.nn.sigmoid(accs[0])).astype(out[0].dtype)

        tn = _tile(ssm_w, 512)
        y_ssm, = fused_matmul(
            "ssm_glu", (n_i, ssm_w // tn), [row_lhs(yg)],
            [(0, w_glu, (None, ssm_w, tn), lambda i, j: (layer, 0, j))],
            [(yg, (tm, tn), lambda i, j: (i, j))],
            [(jax.ShapeDtypeStruct((t_all, ssm_w), BF16), (tm, tn), lambda i, j: (i, j))], ep_glu)

        tn = _tile(d, 512)

        def ep_merge(accs, ex, out, pids):
            out[0][...] = (ex[0][...] * accs[0] + ex[1][...] * accs[1]).astype(out[0].dtype)

        merged, = fused_matmul(
            "merge", (n_i, d // tn), [row_lhs(o_all), row_lhs(y_ssm)],
            [(0, w_proj_attn, (None, qk_w, tn), lambda i, j: (layer, 0, j)),
             (1, w_proj_ssm, (None, ssm_w, tn), lambda i, j: (layer, 0, j))],
            [(gates, (tm, tn), lambda i, j: (i, j)),
             (gates, (tm, tn), lambda i, j: (i, j + d // tn))],
            [(jax.ShapeDtypeStruct((t_all, d), BF16), (tm, tn), lambda i, j: (i, j))], ep_merge)

        def ep_residual(accs, ex, out, pids):
            out[0][...] = ex[0][...] + accs[0]

        x, = fused_matmul(
            "out_proj", (n_i, d // tn), [row_lhs(merged)],
            [(0, w_out, (None, d, tn), lambda i, j: (layer, 0, j))],
            [(x, (tm, tn), lambda i, j: (i, j))],
            [(jax.ShapeDtypeStruct((t_all, d), F32), (tm, tn), lambda i, j: (i, j))], ep_residual)

        def ep_swiglu(accs, ex, out, pids):
            out[0][...] = (jax.nn.silu(accs[0]) * accs[1]).astype(out[0].dtype)

        jl = layer // 2
        if layer % 2 == 0:
            xn2 = rmsnorm_cast(x, norm_ffn[layer], _tile(t_all, 512))
            d_ff = ffn_w_gate.shape[2]
            tn = _tile(d_ff, 512)
            h, = fused_matmul(
                "ffn_gate_up", (n_i, d_ff // tn), [row_lhs(xn2)],
                [(0, ffn_w_gate, (None, d, tn), lambda i, j: (jl, 0, j)),
                 (0, ffn_w_up, (None, d, tn), lambda i, j: (jl, 0, j))], [],
                [(jax.ShapeDtypeStruct((t_all, d_ff), BF16), (tm, tn), lambda i, j: (i, j))],
                ep_swiglu)
            tn = _tile(d, 256)
            x, = fused_matmul(
                "ffn_down", (n_i, d // tn), [row_lhs(h)],
                [(0, ffn_w_down, (None, d_ff, tn), lambda i, j: (jl, 0, j))],
                [(x, (tm, tn), lambda i, j: (i, j))],
                [(jax.ShapeDtypeStruct((t_all, d), F32), (tm, tn), lambda i, j: (i, j))],
                ep_residual)
        else:
            xn2, moe_gates = rmsnorm_router(x, norm_ffn[layer], router[jl], _tile(t_all, 512))
            n_e, _, d_fe = moe_w_gate.shape[1:]
            tn = _tile(d_fe, 256)
            h, = fused_matmul(
                "moe_gate_up", (n_i, n_e, d_fe // tn),
                [(xn2, (tm, d), lambda i, e, j: (i, 0))],
                [(0, moe_w_gate, (None, None, d, tn), lambda i, e, j: (jl, e, 0, j)),
                 (0, moe_w_up, (None, None, d, tn), lambda i, e, j: (jl, e, 0, j))], [],
                [(jax.ShapeDtypeStruct((n_e, t_all, d_fe), BF16), (None, tm, tn),
                  lambda i, e, j: (e, i, j))],
                ep_swiglu)

            def ep_moe_down(accs, ex, out, pids):
                e = pids[2]
                g = ex[1][...]
                lane = lax.broadcasted_iota(jnp.int32, g.shape, 1)
                ge = jnp.sum(jnp.where(lane == e, g, 0.0), axis=-1, keepdims=True)

                @pl.when(e == 0)
                def _():
                    out[0][...] = ex[0][...]

                out[0][...] += ge * accs[0]

            tn = _tile(d, 512)
            x, = fused_matmul(
                "moe_down", (n_i, d // tn, n_e),
                [(h, (None, tm, d_fe), lambda i, j, e: (e, i, 0))],
                [(0, moe_w_down, (None, None, d_fe, tn), lambda i, j, e: (jl, e, 0, j))],
                [(x, (tm, tn), lambda i, j, e: (i, j)),
                 (moe_gates, (tm, LANES), lambda i, j, e: (i, 0))],
                [(jax.ShapeDtypeStruct((t_all, d), F32), (tm, tn), lambda i, j, e: (i, j))],
                ep_moe_down)

        k_out.append(kn)
        v_out.append(v_all)
        hrp_out.append(hrp)
        hip_out.append(hip)
        hrs_out.append(hrs)
        his_out.append(his)

    def split(parts, lo, hi, shape):
        return jnp.stack([p[lo:hi].reshape(shape) for p in parts], axis=0)

    kv_p = (n_b, seq, n_heads, HEAD_LANES)
    kv_s = (db, t_new, n_heads, HEAD_LANES)
    st = lambda parts, n: jnp.stack([p.reshape(n, n_groups, state) for p in parts], axis=0)
    return (x[:tp].reshape(n_b, seq, d), x[tp:].reshape(db, t_new, d),
            split(k_out, 0, tp, kv_p), split(v_out, 0, tp, kv_p),
            st(hrp_out, n_b), st(hip_out, n_b),
            split(k_out, tp, t_all, kv_s), split(v_out, tp, t_all, kv_s),
            st(hrs_out, db), st(his_out, db))
```

```python
import functools
import math

import jax
import jax.numpy as jnp
from jax import lax
from jax.experimental import pallas as pl
from jax.experimental.pallas import tpu as pltpu

F32 = jnp.float32
BF16 = jnp.bfloat16

EPS = 1e-6
NEG_INF = -1e30
HEAD_DIM = 64
HEAD_LANES = 2 * HEAD_DIM
LANES = 128
SUBLANES = 8
GROUP_CH = 16
SSM_BLOCK_CH = 256
TOP_K = 2
VMEM_LIMIT_BYTES = 56 * 1024 * 1024


def _tile(dim, pref):
    t = min(pref, dim)
    while dim % t:
        t //= 2
    return t


def _log2(n):
    assert n & (n - 1) == 0
    return n.bit_length() - 1


def _params(n_axes):
    return pltpu.CompilerParams(dimension_semantics=("arbitrary",) * n_axes,
                                vmem_limit_bytes=VMEM_LIMIT_BYTES)


def _rmsnorm_kernel(x_ref, g_ref, o_ref):
    x = x_ref[...]
    ms = jnp.mean(x * x, axis=-1, keepdims=True)
    o_ref[...] = (x * lax.rsqrt(ms + EPS) * g_ref[...]).astype(o_ref.dtype)


def rmsnorm_cast(x, g, tm):
    t, d = x.shape
    return pl.pallas_call(
        _rmsnorm_kernel,
        grid=(t // tm,),
        in_specs=[pl.BlockSpec((tm, d), lambda i: (i, 0)),
                  pl.BlockSpec((1, d), lambda i: (0, 0))],
        out_specs=pl.BlockSpec((tm, d), lambda i: (i, 0)),
        out_shape=jax.ShapeDtypeStruct((t, d), BF16),
        compiler_params=_params(1),
        name="rmsnorm_cast",
    )(x, g.reshape(1, d))


def _split3(a):
    hi = a.astype(BF16)
    r1 = a - hi.astype(F32)
    mid = r1.astype(BF16)
    lo = (r1 - mid.astype(F32)).astype(BF16)
    return hi, mid, lo


def _rmsnorm_router_kernel(x_ref, g_ref, rw_ref, o_ref, gates_ref, *, n_experts):
    x = x_ref[...]
    ms = jnp.mean(x * x, axis=-1, keepdims=True)
    xn = x * lax.rsqrt(ms + EPS) * g_ref[...]
    o_ref[...] = xn.astype(o_ref.dtype)
    xh, xm, xl = _split3(xn)
    wh, wm, wl = _split3(rw_ref[...])
    dot = functools.partial(jnp.dot, preferred_element_type=F32)
    logits = (dot(xh, wh) + (dot(xh, wm) + dot(xm, wh))
              + (dot(xh, wl) + dot(xm, wm) + dot(xl, wh)))
    lane = lax.broadcasted_iota(jnp.int32, logits.shape, 1).astype(F32)
    logits = jnp.where(lane < n_experts, logits, -jnp.inf)
    m1 = jnp.max(logits, axis=-1, keepdims=True)
    i1 = jnp.min(jnp.where(logits == m1, lane, float(LANES)), axis=-1, keepdims=True)
    rest = jnp.where(lane == i1, -jnp.inf, logits)
    m2 = jnp.max(rest, axis=-1, keepdims=True)
    i2 = jnp.min(jnp.where(rest == m2, lane, float(LANES)), axis=-1, keepdims=True)
    e2 = jnp.exp(m2 - m1)
    g1 = 1.0 / (1.0 + e2)
    g2 = e2 / (1.0 + e2)
    gates_ref[...] = jnp.where(lane == 0.0, i1, jnp.where(lane == 1.0, i2, jnp.where(
        lane == 2.0, g1, jnp.where(lane == 3.0, g2, 0.0))))


def rmsnorm_router(x, g, router_w, tm):
    t, d = x.shape
    n_experts = router_w.shape[1]
    rw = jnp.zeros((d, LANES), F32).at[:, :n_experts].set(router_w)
    return pl.pallas_call(
        functools.partial(_rmsnorm_router_kernel, n_experts=n_experts),
        grid=(t // tm,),
        in_specs=[pl.BlockSpec((tm, d), lambda i: (i, 0)),
                  pl.BlockSpec((1, d), lambda i: (0, 0)),
                  pl.BlockSpec((d, LANES), lambda i: (0, 0))],
        out_specs=[pl.BlockSpec((tm, d), lambda i: (i, 0)),
                   pl.BlockSpec((tm, LANES), lambda i: (i, 0))],
        out_shape=[jax.ShapeDtypeStruct((t, d), F32),
                   jax.ShapeDtypeStruct((t, LANES), F32)],
        compiler_params=_params(1),
        name="rmsnorm_router",
    )(x, g.reshape(1, d), rw)


def _issue_row_gather(idx_ref, n_rows, src_hbm, dst_buf, slot, sem):
    def body(r, carry):
        row = idx_ref[0, r]
        pltpu.make_async_copy(src_hbm.at[pl.ds(row, 1)], dst_buf.at[slot, pl.ds(r, 1)],
                              sem.at[slot]).start()
        return carry
    lax.fori_loop(0, n_rows, body, 0, unroll=8)


def _wait_row_gather(n_rows, src_hbm, dst_buf, slot, sem):
    pltpu.make_async_copy(src_hbm.at[pl.ds(0, n_rows)], dst_buf.at[slot], sem.at[slot]).wait()


def _gathered_rows(step, n_steps, first, idx_ref, idx_next_ref, src_hbm, buf, sem, n_rows):
    slot = step % 2

    @pl.when(first)
    def _():
        @pl.when(step == 0)
        def _():
            _issue_row_gather(idx_ref, n_rows, src_hbm, buf, 0, sem)

        _wait_row_gather(n_rows, src_hbm, buf, slot, sem)

        @pl.when(step + 1 < n_steps)
        def _():
            _issue_row_gather(idx_next_ref, n_rows, src_hbm, buf, 1 - slot, sem)

    return slot


def _moe_gate_up_kernel(blk_e_ref, blk_rows_ref, idx_ref, idx_next_ref, xn_hbm, wg_ref, wu_ref,
                        h_ref, buf, lhs_scr, sem, *, tmb):
    del blk_e_ref
    bi, j = pl.program_id(0), pl.program_id(1)
    slot = _gathered_rows(bi, pl.num_programs(0), j == 0, idx_ref, idx_next_ref, xn_hbm, buf, sem, tmb)

    @pl.when(j == 0)
    def _():
        lhs_scr[...] = buf[slot].astype(BF16)

    n_valid = blk_rows_ref[bi]
    half = tmb // 2
    wg = wg_ref[...].astype(BF16)
    wu = wu_ref[...].astype(BF16)
    for k in range(2):
        rs = slice(k * half, (k + 1) * half)

        @pl.when(n_valid > k * half)
        def _(rs=rs):
            a = lhs_scr[rs, :]
            gate = jnp.dot(a, wg, preferred_element_type=F32)
            up = jnp.dot(a, wu, preferred_element_type=F32)
            h_ref[rs, :] = (jax.nn.silu(gate) * up).astype(h_ref.dtype)

        @pl.when(n_valid <= k * half)
        def _(rs=rs):
            h_ref[rs, :] = jnp.zeros((half, h_ref.shape[1]), h_ref.dtype)


def _moe_down_kernel(blk_e_ref, blk_rows_ref, h_ref, w_ref, gate_ref, y_ref, *, tmb):
    del blk_e_ref
    bi = pl.program_id(0)
    n_valid = blk_rows_ref[bi]
    half = tmb // 2
    w = w_ref[...].astype(BF16)
    for k in range(2):
        rs = slice(k * half, (k + 1) * half)

        @pl.when(n_valid > k * half)
        def _(rs=rs):
            y_ref[rs, :] = gate_ref[rs, :] * jnp.dot(h_ref[rs, :], w, preferred_element_type=F32)

        @pl.when(n_valid <= k * half)
        def _(rs=rs):
            y_ref[rs, :] = jnp.zeros((half, y_ref.shape[1]), F32)


def _moe_combine_kernel(idx_ref, idx_next_ref, x_ref, ys_hbm, o_ref, buf, sem, *, tc):
    i = pl.program_id(0)
    slot = _gathered_rows(i, pl.num_programs(0), i >= 0, idx_ref, idx_next_ref, ys_hbm, buf, sem, 2 * tc)
    o_ref[...] = x_ref[...] + (buf[slot, 0:tc, :] + buf[slot, tc:2 * tc, :])


def moe_sparse(x, xn, sel, w_gate, w_up, w_down, jl):
    t, d = x.shape
    n_e, _, d_fe = w_gate.shape[1:]
    n_assign = TOP_K * t
    tmb = 1024 if n_assign >= 8 * 1024 else 256
    n_blk = n_assign // tmb + n_e
    n_rows = n_blk * tmb

    e_flat = sel[:, 0:TOP_K].astype(jnp.int32).reshape(n_assign)
    g_flat = sel[:, TOP_K:2 * TOP_K].reshape(n_assign)
    onehot = (e_flat[:, None] == jnp.arange(n_e, dtype=jnp.int32)[None, :]).astype(jnp.int32)
    counts = jnp.sum(onehot, axis=0)
    rank = jnp.sum((jnp.cumsum(onehot, axis=0) - onehot) * onehot, axis=1)
    blks_e = (counts + tmb - 1) // tmb
    blk_end = jnp.cumsum(blks_e)
    blk_start = blk_end - blks_e
    pos = blk_start[e_flat] * tmb + rank
    src_tok = jnp.zeros((n_rows,), jnp.int32).at[pos].set(jnp.arange(n_assign, dtype=jnp.int32) // TOP_K)
    gate_sorted = jnp.zeros((n_rows,), F32).at[pos].set(g_flat)
    blk_ids = jnp.arange(n_blk, dtype=jnp.int32)
    n_active = blk_end[-1]
    blk_e = jnp.sum((blk_ids[:, None] >= blk_end[None, :]).astype(jnp.int32), axis=1)
    blk_e = jnp.minimum(blk_e, blk_e[jnp.maximum(n_active - 1, 0)])
    blk_rows = jnp.clip(counts[blk_e] - (blk_ids - blk_start[blk_e]) * tmb, 0, tmb)
    blk_rows = jnp.where(blk_ids < n_active, blk_rows, 0).astype(jnp.int32)

    tn = _tile(d_fe, 256)
    nj = d_fe // tn
    idx2d = src_tok.reshape(n_blk, 1, tmb)
    col = lambda bi, j, be, br: jnp.where(br[bi] > 0, j, nj - 1)
    h = pl.pallas_call(
        functools.partial(_moe_gate_up_kernel, tmb=tmb),
        grid_spec=pltpu.PrefetchScalarGridSpec(
            num_scalar_prefetch=2,
            grid=(n_blk, nj),
            in_specs=[pl.BlockSpec((None, 1, tmb), lambda bi, j, be, br: (bi, 0, 0),
                                   memory_space=pltpu.SMEM),
                      pl.BlockSpec((None, 1, tmb),
                                   lambda bi, j, be, br: (jnp.minimum(bi + 1, n_blk - 1), 0, 0),
                                   memory_space=pltpu.SMEM),
                      pl.BlockSpec(memory_space=pl.ANY),
                      pl.BlockSpec((None, None, d, tn), lambda bi, j, be, br: (jl, be[bi], 0, col(bi, j, be, br))),
                      pl.BlockSpec((None, None, d, tn), lambda bi, j, be, br: (jl, be[bi], 0, col(bi, j, be, br)))],
            out_specs=pl.BlockSpec((tmb, tn), lambda bi, j, be, br: (bi, j)),
            scratch_shapes=[pltpu.VMEM((2, tmb, d), F32),
                            pltpu.VMEM((tmb, d), BF16),
                            pltpu.SemaphoreType.DMA((2,))]),
        out_shape=jax.ShapeDtypeStruct((n_rows, d_fe), BF16),
        compiler_params=_params(2),
        name="moe_gate_up",
    )(blk_e, blk_rows, idx2d, idx2d, xn, w_gate, w_up)

    tn = _tile(d, 512)
    nj = d // tn
    ys = pl.pallas_call(
        functools.partial(_moe_down_kernel, tmb=tmb),
        grid_spec=pltpu.PrefetchScalarGridSpec(
            num_scalar_prefetch=2,
            grid=(n_blk, nj),
            in_specs=[pl.BlockSpec((tmb, d_fe), lambda bi, j, be, br: (bi, 0)),
                      pl.BlockSpec((None, None, d_fe, tn), lambda bi, j, be, br: (jl, be[bi], 0, col(bi, j, be, br))),
                      pl.BlockSpec((tmb, 1), lambda bi, j, be, br: (bi, 0))],
            out_specs=pl.BlockSpec((tmb, tn), lambda bi, j, be, br: (bi, j))),
        out_shape=jax.ShapeDtypeStruct((n_rows, d), F32),
        compiler_params=_params(2),
        name="moe_down",
    )(blk_e, blk_rows, h, w_down, gate_sorted.reshape(n_rows, 1))

    tc = _tile(t, 512)
    n_i = t // tc
    pos_blk = pos.reshape(n_i, tc, TOP_K).transpose(0, 2, 1).reshape(n_i, 1, TOP_K * tc)
    return pl.pallas_call(
        functools.partial(_moe_combine_kernel, tc=tc),
        grid=(n_i,),
        in_specs=[pl.BlockSpec((None, 1, TOP_K * tc), lambda i: (i, 0, 0), memory_space=pltpu.SMEM),
                  pl.BlockSpec((None, 1, TOP_K * tc), lambda i: (jnp.minimum(i + 1, n_i - 1), 0, 0),
                               memory_space=pltpu.SMEM),
                  pl.BlockSpec((tc, d), lambda i: (i, 0)),
                  pl.BlockSpec(memory_space=pl.ANY)],
        out_specs=pl.BlockSpec((tc, d), lambda i: (i, 0)),
        out_shape=jax.ShapeDtypeStruct((t, d), F32),
        scratch_shapes=[pltpu.VMEM((2, TOP_K * tc, d), F32),
                        pltpu.SemaphoreType.DMA((2,))],
        compiler_params=_params(1),
        name="moe_combine",
    )(pos_blk, pos_blk, x, ys)


def fused_matmul(name, grid, lhs, terms, extras, outs, epilogue):
    n_lhs, n_terms, n_ex, n_out = len(lhs), len(terms), len(extras), len(outs)
    staged = [k for k, (a, _, _) in enumerate(lhs) if a.dtype != BF16]

    def kernel(*refs):
        lhs_refs = refs[:n_lhs]
        w_refs = refs[n_lhs:n_lhs + n_terms]
        ex_refs = refs[n_lhs + n_terms:n_lhs + n_terms + n_ex]
        out_refs = refs[n_lhs + n_terms + n_ex:n_lhs + n_terms + n_ex + n_out]
        scr_refs = refs[n_lhs + n_terms + n_ex + n_out:]
        pids = [pl.program_id(a) for a in range(len(grid))]
        if staged:
            first = pids[1] == 0
            for p in pids[2:]:
                first = jnp.logical_and(first, p == 0)

            @pl.when(first)
            def _():
                for s, k in enumerate(staged):
                    scr_refs[s][...] = lhs_refs[k][...].astype(BF16)

        accs = []
        for (li, _, _, _), w_ref in zip(terms, w_refs):
            a = scr_refs[staged.index(li)][...] if li in staged else lhs_refs[li][...]
            accs.append(jnp.dot(a, w_ref[...].astype(BF16), preferred_element_type=F32))
        epilogue(accs, ex_refs, out_refs, pids)

    in_specs = ([pl.BlockSpec(bs, im) for _, bs, im in lhs]
                + [pl.BlockSpec(bs, im) for _, _, bs, im in terms]
                + [pl.BlockSpec(bs, im) for _, bs, im in extras])
    args = [a for a, _, _ in lhs] + [w for _, w, _, _ in terms] + [a for a, _, _ in extras]
    scratch = [pltpu.VMEM(tuple(b for b in lhs[k][1] if b is not None), BF16) for k in staged]
    res = pl.pallas_call(
        kernel,
        grid=grid,
        in_specs=in_specs,
        out_specs=[pl.BlockSpec(bs, im) for _, bs, im in outs],
        out_shape=[sd for sd, _, _ in outs],
        scratch_shapes=scratch,
        compiler_params=_params(len(grid)),
        name=name,
    )(*args)
    return res


def _segment_mean64(sq):
    r = lax.broadcasted_iota(jnp.int32, (LANES, LANES), 0) >> _log2(HEAD_DIM)
    c = lax.broadcasted_iota(jnp.int32, (LANES, LANES), 1) >> _log2(HEAD_DIM)
    ones = (r == c).astype(BF16)
    hi, mid, lo = _split3(sq)
    dot = functools.partial(jnp.dot, preferred_element_type=F32)
    cols = []
    for j in range(sq.shape[1] // LANES):
        sl = slice(j * LANES, (j + 1) * LANES)
        cols.append(dot(hi[:, sl], ones) + dot(mid[:, sl], ones) + dot(lo[:, sl], ones))
    return jnp.concatenate(cols, axis=1) * (1.0 / HEAD_DIM)


def _attn_prompt_kernel(lam_ref, q_ref, k_ref, v_ref, g_ref, o_ref, *, tq, out_scale):
    qi = pl.program_id(2)
    lam = lam_ref[0]
    q = q_ref[...].astype(F32)
    lane = lax.broadcasted_iota(jnp.int32, q.shape, 1)
    qs = jnp.concatenate([jnp.where(lane < HEAD_DIM, q, 0.0),
                          jnp.where(lane >= HEAD_DIM, q, 0.0)], axis=0).astype(BF16)
    row = lax.broadcasted_iota(jnp.int32, (2 * tq, tq), 0)
    row = jnp.where(row >= tq, row - tq, row)
    col = lax.broadcasted_iota(jnp.int32, (2 * tq, tq), 1)
    delta = col - row

    def body(kb, carry):
        m, l, acc = carry
        start = pl.multiple_of(kb * tq, tq)
        k = k_ref[pl.ds(start, tq), :].astype(BF16)
        v = v_ref[pl.ds(start, tq), :].astype(BF16)
        s = lax.dot_general(qs, k, (((1,), (1,)), ((), ())), preferred_element_type=F32)
        s = jnp.where(delta <= (qi - kb) * tq, s, NEG_INF)
        m_new = jnp.maximum(m, jnp.max(s, axis=-1, keepdims=True))
        alpha = jnp.exp(m - m_new)
        p = jnp.exp(s - m_new)
        l = alpha * l + jnp.sum(p, axis=-1, keepdims=True)
        acc = alpha * acc + jnp.dot(p.astype(BF16), v, preferred_element_type=F32)
        return m_new, l, acc

    m0 = jnp.full((2 * tq, 1), NEG_INF, F32)
    l0 = jnp.zeros((2 * tq, 1), F32)
    acc0 = jnp.zeros((2 * tq, HEAD_LANES), F32)
    _, l, acc = lax.fori_loop(0, qi + 1, body, (m0, l0, acc0))
    o = acc / l
    o = o[:tq] - lam * o[tq:]
    o = o * lax.rsqrt(jnp.mean(o * o, axis=-1, keepdims=True) + EPS) * g_ref[...] * out_scale
    o_ref[...] = o.astype(o_ref.dtype)


def attn_prompt(lam, qn, kn, vu, subln, out_scale, n_batch, seq, n_heads):
    tq = _tile(seq, 256)
    nq = seq // tq
    return pl.pallas_call(
        functools.partial(_attn_prompt_kernel, tq=tq, out_scale=out_scale),
        grid=(n_batch, n_heads, nq),
        in_specs=[pl.BlockSpec(memory_space=pltpu.SMEM),
                  pl.BlockSpec((tq, HEAD_LANES), lambda b, h, i: (b * nq + i, h)),
                  pl.BlockSpec((seq, HEAD_LANES), lambda b, h, i: (b, h)),
                  pl.BlockSpec((seq, HEAD_LANES), lambda b, h, i: (b, h)),
                  pl.BlockSpec((1, HEAD_LANES), lambda b, h, i: (0, 0))],
        out_specs=pl.BlockSpec((tq, HEAD_LANES), lambda b, h, i: (b * nq + i, h)),
        out_shape=jax.ShapeDtypeStruct((n_batch * seq, n_heads * HEAD_LANES), BF16),
        compiler_params=_params(3),
        name="attn_prompt",
    )(lam.reshape(1), qn, kn, vu, subln.reshape(1, HEAD_LANES))


def _attn_sample_kernel(pt_ref, lam_ref, q_ref, kn_ref, vn_ref, kc_ref, vc_ref, g_ref, o_ref,
                        q_scr, bias_scr, m_scr, l_scr, acc_scr, *, n_heads, t_new, out_scale):
    del pt_ref
    p = pl.program_id(1)
    n_pages = pl.num_programs(1)
    rows = 2 * n_heads * t_new
    page_rows = kc_ref.shape[0]
    nt = (((1,), (1,)), ((), ()))
    log_t, head_mask = _log2(t_new), n_heads - 1

    @pl.when(p == 0)
    def _():
        q = q_ref[0]
        lane = lax.broadcasted_iota(jnp.int32, (t_new, HEAD_LANES), 1)
        pieces = []
        for half in range(2):
            for h in range(n_heads):
                qh = q[:, h * HEAD_LANES:(h + 1) * HEAD_LANES]
                pieces.append(jnp.where((lane >= HEAD_DIM) == bool(half), qh, 0.0))
        q_scr[...] = jnp.concatenate(pieces, axis=0).astype(BF16)
        r = lax.broadcasted_iota(jnp.int32, (rows, page_rows), 0)
        c = lax.broadcasted_iota(jnp.int32, (rows, page_rows), 1)
        bias_scr[...] = jnp.where(((r >> log_t) & head_mask) == (c & head_mask), 0.0, NEG_INF)
        m_scr[...] = jnp.full(m_scr.shape, NEG_INF, F32)
        l_scr[...] = jnp.zeros(l_scr.shape, F32)
        acc_scr[...] = jnp.zeros(acc_scr.shape, F32)

    def update(k, v, bias):
        s = lax.dot_general(q_scr[...], k, nt, preferred_element_type=F32) + bias
        m_old = m_scr[...]
        m_new = jnp.maximum(m_old, jnp.max(s, axis=-1, keepdims=True))
        alpha = jnp.exp(m_old - m_new)
        pe = jnp.exp(s - m_new)
        l_scr[...] = alpha * l_scr[...] + jnp.sum(pe, axis=-1, keepdims=True)
        acc_scr[...] = alpha * acc_scr[...] + jnp.dot(pe.astype(BF16), v, preferred_element_type=F32)
        m_scr[...] = m_new

    update(kc_ref[...].astype(BF16), vc_ref[...].astype(BF16), bias_scr[...])

    @pl.when(p == n_pages - 1)
    def _():
        new_rows = t_new * n_heads
        pad = jnp.zeros((LANES - new_rows, HEAD_LANES), F32)
        k_new = jnp.concatenate([kn_ref[0], pad], axis=0).astype(BF16)
        v_new = jnp.concatenate([vn_ref[0], pad], axis=0).astype(BF16)
        r = lax.broadcasted_iota(jnp.int32, (rows, LANES), 0)
        c = lax.broadcasted_iota(jnp.int32, (rows, LANES), 1)
        ok = jnp.logical_and(((r >> log_t) & head_mask) == (c & head_mask),
                             (c >> _log2(n_heads)) <= (r & (t_new - 1)))
        update(k_new, v_new, jnp.where(ok, 0.0, NEG_INF))
        lam = lam_ref[0]
        inv_l = 1.0 / l_scr[...]
        outs = []
        for h in range(n_heads):
            r1 = slice(h * t_new, (h + 1) * t_new)
            r2 = slice((n_heads + h) * t_new, (n_heads + h + 1) * t_new)
            o = acc_scr[r1, :] * inv_l[r1] - lam * (acc_scr[r2, :] * inv_l[r2])
            o = o * lax.rsqrt(jnp.mean(o * o, axis=-1, keepdims=True) + EPS) * g_ref[...] * out_scale
            outs.append(o)
        o_ref[0] = jnp.concatenate(outs, axis=1).astype(o_ref.dtype)


def attn_sample(lam, page_table, q_s, k_s, v_s, cache_k, cache_v, layer, subln, out_scale, n_heads):
    db, t_new, width = q_s.shape
    page_rows = cache_k.shape[2]
    new_rows = t_new * n_heads
    n_pages = page_table.shape[1]
    rows = 2 * n_heads * t_new
    assert new_rows <= LANES
    grid_spec = pltpu.PrefetchScalarGridSpec(
        num_scalar_prefetch=1,
        grid=(db, n_pages),
        in_specs=[pl.BlockSpec(memory_space=pltpu.SMEM),
                  pl.BlockSpec((1, t_new, width), lambda b, p, pt: (b, 0, 0)),
                  pl.BlockSpec((1, new_rows, HEAD_LANES), lambda b, p, pt: (b, 0, 0)),
                  pl.BlockSpec((1, new_rows, HEAD_LANES), lambda b, p, pt: (b, 0, 0)),
                  pl.BlockSpec((None, None, page_rows, HEAD_LANES),
                               lambda b, p, pt: (layer, pt[b, p], 0, 0)),
                  pl.BlockSpec((None, None, page_rows, HEAD_LANES),
                               lambda b, p, pt: (layer, pt[b, p], 0, 0)),
                  pl.BlockSpec((1, HEAD_LANES), lambda b, p, pt: (0, 0))],
        out_specs=pl.BlockSpec((1, t_new, width), lambda b, p, pt: (b, 0, 0)),
        scratch_shapes=[pltpu.VMEM((rows, HEAD_LANES), BF16),
                        pltpu.VMEM((rows, page_rows), F32),
                        pltpu.VMEM((rows, 1), F32),
                        pltpu.VMEM((rows, 1), F32),
                        pltpu.VMEM((rows, HEAD_LANES), F32)],
    )
    return pl.pallas_call(
        functools.partial(_attn_sample_kernel, n_heads=n_heads, t_new=t_new, out_scale=out_scale),
        grid_spec=grid_spec,
        out_shape=jax.ShapeDtypeStruct((db, t_new, width), F32),
        compiler_params=_params(2),
        name="attn_sample",
    )(page_table, lam.reshape(1), q_s, k_s, v_s, cache_k, cache_v, subln.reshape(1, HEAD_LANES))


def _ssm_kernel(u_ref, h0re_ref, h0im_ref, are_ref, aim_ref, b_ref, c_ref, d_ref,
                y_ref, hre_ref, him_ref, bu_scr, *, nb, tc):
    c_idx = pl.program_id(1)
    sw = are_ref.shape[1]

    @pl.when(c_idx == 0)
    def _():
        hre_ref[...] = h0re_ref[...]
        him_ref[...] = h0im_ref[...]

    u = u_ref[...]
    bu_scr[...] = jnp.dot(u.astype(BF16), b_ref[...], preferred_element_type=F32)
    a_re = are_ref[...]
    a_im = aim_ref[...]
    for r in range(nb // SUBLANES):
        rs = slice(r * SUBLANES, (r + 1) * SUBLANES)

        def body(t, carry, r=r):
            h_re, h_im = carry
            row = pl.multiple_of(t * nb + r * SUBLANES, SUBLANES)
            n_re = a_re * h_re - a_im * h_im + bu_scr[pl.ds(row, SUBLANES), 0:sw]
            n_im = a_re * h_im + a_im * h_re + bu_scr[pl.ds(row, SUBLANES), sw:2 * sw]
            bu_scr[pl.ds(row, SUBLANES), 0:sw] = n_re
            bu_scr[pl.ds(row, SUBLANES), sw:2 * sw] = n_im
            return n_re, n_im

        h_re, h_im = lax.fori_loop(0, tc, body, (hre_ref[rs, :], him_ref[rs, :]))
        hre_ref[rs, :] = h_re
        him_ref[rs, :] = h_im
    y = jnp.dot(bu_scr[...].astype(BF16), c_ref[...], preferred_element_type=F32) + d_ref[...] * u
    y_ref[...] = jax.nn.gelu(y)


def ssm_scan(u_tm, h0_re, h0_im, a_re, a_im, b_blk, c_blk, d, nb, tc):
    rows, ch = u_tm.shape
    n_gb = ch // SSM_BLOCK_CH
    sw = h0_re.shape[1] // n_gb
    n_chunks = rows // (tc * nb)
    blk_rows = tc * nb
    return pl.pallas_call(
        functools.partial(_ssm_kernel, nb=nb, tc=tc),
        grid=(n_gb, n_chunks),
        in_specs=[pl.BlockSpec((blk_rows, SSM_BLOCK_CH), lambda g, c: (c, g)),
                  pl.BlockSpec((nb, sw), lambda g, c: (0, g)),
                  pl.BlockSpec((nb, sw), lambda g, c: (0, g)),
                  pl.BlockSpec((SUBLANES, sw), lambda g, c: (0, g)),
                  pl.BlockSpec((SUBLANES, sw), lambda g, c: (0, g)),
                  pl.BlockSpec((None, SSM_BLOCK_CH, 2 * sw), lambda g, c: (g, 0, 0)),
                  pl.BlockSpec((None, 2 * sw, SSM_BLOCK_CH), lambda g, c: (g, 0, 0)),
                  pl.BlockSpec((1, SSM_BLOCK_CH), lambda g, c: (0, g))],
        out_specs=[pl.BlockSpec((blk_rows, SSM_BLOCK_CH), lambda g, c: (c, g)),
                   pl.BlockSpec((nb, sw), lambda g, c: (0, g)),
                   pl.BlockSpec((nb, sw), lambda g, c: (0, g))],
        out_shape=[jax.ShapeDtypeStruct((rows, ch), F32),
                   jax.ShapeDtypeStruct(h0_re.shape, F32),
                   jax.ShapeDtypeStruct(h0_im.shape, F32)],
        scratch_shapes=[pltpu.VMEM((blk_rows, 2 * sw), F32)],
        compiler_params=_params(2),
        name="ssm_scan",
    )(u_tm, h0_re, h0_im, a_re, a_im, b_blk, c_blk, d)


def _ssm_weights(a_re, a_im, b_re, b_im, c_re, c_im, log_dt):
    n_groups, state = a_re.shape
    gpb = SSM_BLOCK_CH // GROUP_CH
    n_gb = n_groups // gpb
    dt = jnp.exp(log_dt)[:, None]
    mag = jnp.exp(dt * a_re)
    ab_re = mag * jnp.cos(dt * a_im)
    ab_im = mag * jnp.sin(dt * a_im)
    den = a_re * a_re + a_im * a_im
    nr = ab_re - 1.0
    coef_re = (nr * a_re + ab_im * a_im) / den
    coef_im = (ab_im * a_re - nr * a_im) / den
    bb_re = coef_re[..., None] * b_re - coef_im[..., None] * b_im
    bb_im = coef_re[..., None] * b_im + coef_im[..., None] * b_re
    eye = jnp.eye(gpb, dtype=F32)

    def b_block(bb):
        bb = bb.reshape(n_gb, gpb, state, GROUP_CH)
        return jnp.einsum("bgpc,gh->bgchp", bb, eye).reshape(n_gb, gpb * GROUP_CH, gpb * state)

    def c_block(cc):
        cc = cc.reshape(n_gb, gpb, GROUP_CH, state)
        return jnp.einsum("bgcp,gh->bgphc", cc, eye).reshape(n_gb, gpb * state, gpb * GROUP_CH)

    b_blk = jnp.concatenate([b_block(bb_re), b_block(bb_im)], axis=2).astype(BF16)
    c_blk = jnp.concatenate([c_block(c_re), c_block(-c_im)], axis=1).astype(BF16)
    bc = lambda a: jnp.broadcast_to(a.reshape(1, n_groups * state), (SUBLANES, n_groups * state))
    return bc(ab_re), bc(ab_im), b_blk, c_blk


def _lambda_init(layer):
    return 0.8 - 0.6 * math.exp(-0.3 * layer)


def kernel(x_prompt, x_sample, cache_k, cache_v, state_ssm_re, state_ssm_im, page_table, norm_mix, w_in, q_norm, k_norm, lambda_q1, lambda_k1, lambda_q2, lambda_k2, subln, ssm_a_re, ssm_a_im, ssm_b_re, ssm_b_im, ssm_c_re, ssm_c_im, ssm_d, ssm_log_dt, w_glu, w_proj_attn, w_proj_ssm, w_out, norm_ffn, ffn_w_gate, ffn_w_up, ffn_w_down, router, moe_w_gate, moe_w_up, moe_w_down):
    n_b, seq, d = x_prompt.shape
    db, t_new, _ = x_sample.shape
    depth = w_in.shape[0]
    n_heads = cache_k.shape[3]
    qk_w = n_heads * HEAD_LANES
    ssm_w = w_glu.shape[1]
    n_groups, state = ssm_a_re.shape[1:]
    n_state = n_groups * state
    page = cache_k.shape[2]
    tp = n_b * seq
    ts = db * t_new
    t_all = tp + ts
    assert n_b == SUBLANES and db % SUBLANES == 0 and ssm_w % SSM_BLOCK_CH == 0
    assert w_in.shape[2] == 3 * qk_w + ssm_w + 2 * d and ssm_w == qk_w

    tm = _tile(math.gcd(seq, ts), 1024)
    n_i = t_all // tm
    cache_k = cache_k.reshape(cache_k.shape[0], cache_k.shape[1], page * n_heads, HEAD_LANES)
    cache_v = cache_v.reshape(cache_v.shape[0], cache_v.shape[1], page * n_heads, HEAD_LANES)
    x = jnp.concatenate([x_prompt.reshape(tp, d), x_sample.reshape(ts, d)], axis=0)
    zeros_h0 = jnp.zeros((n_b, n_state), F32)
    scale = HEAD_DIM ** -0.5

    def row_lhs(a):
        return (a, (tm, a.shape[1]), lambda i, j: (i, 0))

    def store_epilogue(fn=None):
        def ep(accs, ex, out, pids):
            out[0][...] = (accs[0] if fn is None else fn(accs[0])).astype(out[0].dtype)
        return ep

    k_out, v_out, hrp_out, hip_out, hrs_out, his_out = [], [], [], [], [], []
    for layer in range(depth):
        lam_0 = _lambda_init(layer)
        lam = (jnp.exp(jnp.sum(lambda_q1[layer] * lambda_k1[layer]))
               - jnp.exp(jnp.sum(lambda_q2[layer] * lambda_k2[layer])) + lam_0).astype(F32)
        xn = rmsnorm_cast(x, norm_mix[layer], _tile(t_all, 512))

        tn = _tile(qk_w, 512)
        w_spec = lambda off: ((None, d, tn), lambda i, j: (layer, 0, j + off))

        def head_norm(gain, mult):
            g_row = jnp.tile(gain, tn // HEAD_DIM).reshape(1, tn)

            def ep(accs, ex, out, pids):
                y = accs[0]
                y = y * lax.rsqrt(_segment_mean64(y * y) + EPS) * ex[0][...]
                out[0][...] = (y * mult).astype(out[0].dtype)
            return ep, (g_row, (1, tn), lambda i, j: (0, 0))

        ep_q, ex_q = head_norm(q_norm[layer], scale)
        qn, = fused_matmul(
            "proj_q", (n_i, qk_w // tn), [row_lhs(xn)], [(0, w_in) + w_spec(0)], [ex_q],
            [(jax.ShapeDtypeStruct((t_all, qk_w), BF16), (tm, tn), lambda i, j: (i, j))], ep_q)
        ep_k, ex_k = head_norm(k_norm[layer], 1.0)
        kn, = fused_matmul(
            "proj_k", (n_i, qk_w // tn), [row_lhs(xn)], [(0, w_in) + w_spec(qk_w // tn)], [ex_k],
            [(jax.ShapeDtypeStruct((t_all, qk_w), F32), (tm, tn), lambda i, j: (i, j))], ep_k)
        vu, = fused_matmul(
            "proj_vu", (n_i, (qk_w + ssm_w) // tn), [row_lhs(xn)],
            [(0, w_in) + w_spec(2 * qk_w // tn)], [],
            [(jax.ShapeDtypeStruct((t_all, qk_w + ssm_w), F32), (tm, tn), lambda i, j: (i, j))],
            store_epilogue())
        gates, = fused_matmul(
            "proj_gates", (n_i, 2 * d // tn), [row_lhs(xn)],
            [(0, w_in) + w_spec((3 * qk_w + ssm_w) // tn)], [],
            [(jax.ShapeDtypeStruct((t_all, 2 * d), F32), (tm, tn), lambda i, j: (i, j))],
            store_epilogue(jax.nn.sigmoid))

        out_scale = 1.0 - lam_0
        o_p = attn_prompt(lam, qn, kn, vu, subln[layer], out_scale, n_b, seq, n_heads)
        k_s = kn[tp:].reshape(db, t_new * n_heads, HEAD_LANES)
        v_all = vu[:, :qk_w]
        v_s = v_all[tp:].reshape(db, t_new * n_heads, HEAD_LANES)
        q_s = qn[tp:].astype(F32).reshape(db, t_new, qk_w)
        o_s = attn_sample(lam, page_table, q_s, k_s, v_s, cache_k, cache_v, layer, subln[layer],
                          out_scale, n_heads)
        o_all = jnp.concatenate([o_p, o_s.reshape(ts, qk_w).astype(BF16)], axis=0)

        a_re, a_im, b_blk, c_blk = _ssm_weights(
            ssm_a_re[layer], ssm_a_im[layer], ssm_b_re[layer], ssm_b_im[layer],
            ssm_c_re[layer], ssm_c_im[layer], ssm_log_dt[layer])
        d_row = ssm_d[layer].reshape(1, ssm_w)
        u_all = vu[:, qk_w:]
        u_p = u_all[:tp].reshape(n_b, seq, ssm_w).transpose(1, 0, 2).reshape(tp, ssm_w)
        u_s = u_all[tp:].reshape(db, t_new, ssm_w).transpose(1, 0, 2).reshape(ts, ssm_w)
        yg_p, hrp, hip = ssm_scan(u_p, zeros_h0, zeros_h0, a_re, a_im, b_blk, c_blk, d_row,
                                  n_b, _tile(seq, 256))
        yg_s, hrs, his = ssm_scan(u_s, state_ssm_re[layer].reshape(db, n_state),
                                  state_ssm_im[layer].reshape(db, n_state),
                                  a_re, a_im, b_blk, c_blk, d_row, db, t_new)
        yg = jnp.concatenate(
            [yg_p.reshape(seq, n_b, ssm_w).transpose(1, 0, 2).reshape(tp, ssm_w),
             yg_s.reshape(t_new, db, ssm_w).transpose(1, 0, 2).reshape(ts, ssm_w)], axis=0)

        def ep_glu(accs, ex, out, pids):
            out[0][...] = (ex[0][...] * jax.nn.sigmoid(accs[0])).astype(out[0].dtype)

        tn = _tile(ssm_w, 512)
        y_ssm, = fused_matmul(
            "ssm_glu", (n_i, ssm_w // tn), [row_lhs(yg)],
            [(0, w_glu, (None, ssm_w, tn), lambda i, j: (layer, 0, j))],
            [(yg, (tm, tn), lambda i, j: (i, j))],
            [(jax.ShapeDtypeStruct((t_all, ssm_w), BF16), (tm, tn), lambda i, j: (i, j))], ep_glu)

        tn = _tile(d, 512)

        def ep_merge(accs, ex, out, pids):
            out[0][...] = (ex[0][...] * accs[0] + ex[1][...] * accs[1]).astype(out[0].dtype)

        merged, = fused_matmul(
            "merge", (n_i, d // tn), [row_lhs(o_all), row_lhs(y_ssm)],
            [(0, w_proj_attn, (None, qk_w, tn), lambda i, j: (layer, 0, j)),
             (1, w_proj_ssm, (None, ssm_w, tn), lambda i, j: (layer, 0, j))],
            [(gates, (tm, tn), lambda i, j: (i, j)),
             (gates, (tm, tn), lambda i, j: (i, j + d // tn))],
            [(jax.ShapeDtypeStruct((t_all, d), BF16), (tm, tn), lambda i, j: (i, j))], ep_merge)

        def ep_residual(accs, ex, out, pids):
            out[0][...] = ex[0][...] + accs[0]

        x, = fused_matmul(
            "out_proj", (n_i, d // tn), [row_lhs(merged)],
            [(0, w_out, (None, d, tn), lambda i, j: (layer, 0, j))],
            [(x, (tm, tn), lambda i, j: (i, j))],
            [(jax.ShapeDtypeStruct((t_all, d), F32), (tm, tn), lambda i, j: (i, j))], ep_residual)

        def ep_swiglu(accs, ex, out, pids):
            out[0][...] = (jax.nn.silu(accs[0]) * accs[1]).astype(out[0].dtype)

        jl = layer // 2
        if layer % 2 == 0:
            xn2 = rmsnorm_cast(x, norm_ffn[layer], _tile(t_all, 512))
            d_ff = ffn_w_gate.shape[2]
            tn = _tile(d_ff, 512)
            h, = fused_matmul(
                "ffn_gate_up", (n_i, d_ff // tn), [row_lhs(xn2)],
                [(0, ffn_w_gate, (None, d, tn), lambda i, j: (jl, 0, j)),
                 (0, ffn_w_up, (None, d, tn), lambda i, j: (jl, 0, j))], [],
                [(jax.ShapeDtypeStruct((t_all, d_ff), BF16), (tm, tn), lambda i, j: (i, j))],
                ep_swiglu)
            tn = _tile(d, 256)
            x, = fused_matmul(
                "ffn_down", (n_i, d // tn), [row_lhs(h)],
                [(0, ffn_w_down, (None, d_ff, tn), lambda i, j: (jl, 0, j))],
                [(x, (tm, tn), lambda i, j: (i, j))],
                [(jax.ShapeDtypeStruct((t_all, d), F32), (tm, tn), lambda i, j: (i, j))],
                ep_residual)
        else:
            xn2, sel = rmsnorm_router(x, norm_ffn[layer], router[jl], _tile(t_all, 512))
            x = moe_sparse(x, xn2, sel, moe_w_gate, moe_w_up, moe_w_down, jl)

        k_out.append(kn)
        v_out.append(v_all)
        hrp_out.append(hrp)
        hip_out.append(hip)
        hrs_out.append(hrs)
        his_out.append(his)

    def split(parts, lo, hi, shape):
        return jnp.stack([p[lo:hi].reshape(shape) for p in parts], axis=0)

    kv_p = (n_b, seq, n_heads, HEAD_LANES)
    kv_s = (db, t_new, n_heads, HEAD_LANES)
    st = lambda parts, n: jnp.stack([p.reshape(n, n_groups, state) for p in parts], axis=0)
    return (x[:tp].reshape(n_b, seq, d), x[tp:].reshape(db, t_new, d),
            split(k_out, 0, tp, kv_p), split(v_out, 0, tp, kv_p),
            st(hrp_out, n_b), st(hip_out, n_b),
            split(k_out, tp, t_all, kv_s), split(v_out, tp, t_all, kv_s),
            st(hrs_out, db), st(his_out, db))
```

```python
import functools
import math

import jax
import jax.numpy as jnp
from jax import lax
from jax.experimental import pallas as pl
from jax.experimental.pallas import tpu as pltpu

F32 = jnp.float32
BF16 = jnp.bfloat16

EPS = 1e-6
NEG_INF = -1e30
HEAD_DIM = 64
HEAD_LANES = 2 * HEAD_DIM
LANES = 128
SUBLANES = 8
GROUP_CH = 16
SSM_BLOCK_CH = 256
TOP_K = 2
VMEM_LIMIT_BYTES = 56 * 1024 * 1024


def _tile(dim, pref):
    t = min(pref, dim)
    while dim % t:
        t //= 2
    return t


def _log2(n):
    assert n & (n - 1) == 0
    return n.bit_length() - 1


def _params(n_axes):
    return pltpu.CompilerParams(dimension_semantics=("arbitrary",) * n_axes,
                                vmem_limit_bytes=VMEM_LIMIT_BYTES)


def _rmsnorm_kernel(x_ref, g_ref, o_ref):
    x = x_ref[...]
    ms = jnp.mean(x * x, axis=-1, keepdims=True)
    o_ref[...] = (x * lax.rsqrt(ms + EPS) * g_ref[...]).astype(o_ref.dtype)


def rmsnorm_cast(x, g, tm):
    t, d = x.shape
    return pl.pallas_call(
        _rmsnorm_kernel,
        grid=(t // tm,),
        in_specs=[pl.BlockSpec((tm, d), lambda i: (i, 0)),
                  pl.BlockSpec((1, d), lambda i: (0, 0))],
        out_specs=pl.BlockSpec((tm, d), lambda i: (i, 0)),
        out_shape=jax.ShapeDtypeStruct((t, d), BF16),
        compiler_params=_params(1),
        name="rmsnorm_cast",
    )(x, g.reshape(1, d))


def _split3(a):
    hi = a.astype(BF16)
    r1 = a - hi.astype(F32)
    mid = r1.astype(BF16)
    lo = (r1 - mid.astype(F32)).astype(BF16)
    return hi, mid, lo


def _rmsnorm_router_kernel(x_ref, g_ref, rw_ref, o_ref, gates_ref, *, n_experts):
    x = x_ref[...]
    ms = jnp.mean(x * x, axis=-1, keepdims=True)
    xn = x * lax.rsqrt(ms + EPS) * g_ref[...]
    o_ref[...] = xn.astype(o_ref.dtype)
    xh, xm, xl = _split3(xn)
    wh, wm, wl = _split3(rw_ref[...])
    dot = functools.partial(jnp.dot, preferred_element_type=F32)
    logits = (dot(xh, wh) + (dot(xh, wm) + dot(xm, wh))
              + (dot(xh, wl) + dot(xm, wm) + dot(xl, wh)))
    lane = lax.broadcasted_iota(jnp.int32, logits.shape, 1).astype(F32)
    logits = jnp.where(lane < n_experts, logits, -jnp.inf)
    m1 = jnp.max(logits, axis=-1, keepdims=True)
    i1 = jnp.min(jnp.where(logits == m1, lane, float(LANES)), axis=-1, keepdims=True)
    rest = jnp.where(lane == i1, -jnp.inf, logits)
    m2 = jnp.max(rest, axis=-1, keepdims=True)
    i2 = jnp.min(jnp.where(rest == m2, lane, float(LANES)), axis=-1, keepdims=True)
    e2 = jnp.exp(m2 - m1)
    g1 = 1.0 / (1.0 + e2)
    g2 = e2 / (1.0 + e2)
    gates_ref[...] = jnp.where(lane == 0.0, i1, jnp.where(lane == 1.0, i2, jnp.where(
        lane == 2.0, g1, jnp.where(lane == 3.0, g2, 0.0))))


def rmsnorm_router(x, g, router_w, tm):
    t, d = x.shape
    n_experts = router_w.shape[1]
    rw = jnp.zeros((d, LANES), F32).at[:, :n_experts].set(router_w)
    return pl.pallas_call(
        functools.partial(_rmsnorm_router_kernel, n_experts=n_experts),
        grid=(t // tm,),
        in_specs=[pl.BlockSpec((tm, d), lambda i: (i, 0)),
                  pl.BlockSpec((1, d), lambda i: (0, 0)),
                  pl.BlockSpec((d, LANES), lambda i: (0, 0))],
        out_specs=[pl.BlockSpec((tm, d), lambda i: (i, 0)),
                   pl.BlockSpec((tm, LANES), lambda i: (i, 0))],
        out_shape=[jax.ShapeDtypeStruct((t, d), F32),
                   jax.ShapeDtypeStruct((t, LANES), F32)],
        compiler_params=_params(1),
        name="rmsnorm_router",
    )(x, g.reshape(1, d), rw)


def _issue_row_gather(idx_ref, n_rows, src_hbm, dst_buf, slot, sem):
    def body(r, carry):
        row = idx_ref[0, r]
        pltpu.make_async_copy(src_hbm.at[pl.ds(row, 1)], dst_buf.at[slot, pl.ds(r, 1)],
                              sem.at[slot]).start()
        return carry
    lax.fori_loop(0, n_rows, body, 0, unroll=8)


def _wait_row_gather(n_rows, src_hbm, dst_buf, slot, sem):
    pltpu.make_async_copy(src_hbm.at[pl.ds(0, n_rows)], dst_buf.at[slot], sem.at[slot]).wait()


def _gathered_rows(step, n_steps, first, idx_ref, idx_next_ref, src_hbm, buf, sem, n_rows):
    slot = step % 2

    @pl.when(first)
    def _():
        @pl.when(step == 0)
        def _():
            _issue_row_gather(idx_ref, n_rows, src_hbm, buf, 0, sem)

        _wait_row_gather(n_rows, src_hbm, buf, slot, sem)

        @pl.when(step + 1 < n_steps)
        def _():
            _issue_row_gather(idx_next_ref, n_rows, src_hbm, buf, 1 - slot, sem)

    return slot


def _moe_gate_up_kernel(blk_e_ref, blk_rows_ref, idx_ref, idx_next_ref, xn_hbm, wg_ref, wu_ref,
                        h_ref, buf, lhs_scr, sem, *, tmb):
    del blk_e_ref
    bi, j = pl.program_id(0), pl.program_id(1)
    slot = _gathered_rows(bi, pl.num_programs(0), j == 0, idx_ref, idx_next_ref, xn_hbm, buf, sem, tmb)

    @pl.when(j == 0)
    def _():
        lhs_scr[...] = buf[slot].astype(BF16)

    n_valid = blk_rows_ref[bi]
    half = tmb // 2
    wg = wg_ref[...].astype(BF16)
    wu = wu_ref[...].astype(BF16)
    for k in range(2):
        rs = slice(k * half, (k + 1) * half)

        @pl.when(n_valid > k * half)
        def _(rs=rs):
            a = lhs_scr[rs, :]
            gate = jnp.dot(a, wg, preferred_element_type=F32)
            up = jnp.dot(a, wu, preferred_element_type=F32)
            h_ref[rs, :] = (jax.nn.silu(gate) * up).astype(h_ref.dtype)

        @pl.when(n_valid <= k * half)
        def _(rs=rs):
            h_ref[rs, :] = jnp.zeros((half, h_ref.shape[1]), h_ref.dtype)


def _moe_down_kernel(blk_e_ref, blk_rows_ref, h_ref, w_ref, y_ref, *, tmb):
    del blk_e_ref
    bi = pl.program_id(0)
    n_valid = blk_rows_ref[bi]
    half = tmb // 2
    w = w_ref[...].astype(BF16)
    for k in range(2):
        rs = slice(k * half, (k + 1) * half)

        @pl.when(n_valid > k * half)
        def _(rs=rs):
            y_ref[rs, :] = jnp.dot(h_ref[rs, :], w, preferred_element_type=F32)

        @pl.when(n_valid <= k * half)
        def _(rs=rs):
            y_ref[rs, :] = jnp.zeros((half, y_ref.shape[1]), F32)


def _moe_combine_kernel(idx_ref, idx_next_ref, x_ref, sel_ref, ys_hbm, *rest, tc, n_first):
    o_refs, (buf, sem) = rest[:-2], rest[-2:]
    i = pl.program_id(0)
    slot = _gathered_rows(i, pl.num_programs(0), i >= 0, idx_ref, idx_next_ref, ys_hbm, buf, sem, 2 * tc)
    g1 = sel_ref[:, TOP_K:TOP_K + 1]
    g2 = sel_ref[:, TOP_K + 1:TOP_K + 2]
    y = x_ref[...] + (g1 * buf[slot, 0:tc, :] + g2 * buf[slot, tc:2 * tc, :])
    if len(o_refs) == 1:
        o_refs[0][...] = y
    else:
        @pl.when(i < n_first)
        def _():
            o_refs[0][...] = y

        @pl.when(i >= n_first)
        def _():
            o_refs[1][...] = y


def moe_sparse(x, xn, sel, w_gate, w_up, w_down, jl, split_rows=None):
    t, d = x.shape
    n_e, _, d_fe = w_gate.shape[1:]
    n_assign = TOP_K * t
    tmb = 1024 if n_assign >= 8 * 1024 else 256
    n_blk = n_assign // tmb + n_e
    n_rows = n_blk * tmb

    e_flat = sel[:, 0:TOP_K].astype(jnp.int32).reshape(n_assign)
    onehot = (e_flat[:, None] == jnp.arange(n_e, dtype=jnp.int32)[None, :]).astype(jnp.int32)
    counts = jnp.sum(onehot, axis=0)
    rank = jnp.sum((jnp.cumsum(onehot, axis=0) - onehot) * onehot, axis=1)
    blks_e = (counts + tmb - 1) // tmb
    blk_end = jnp.cumsum(blks_e)
    blk_start = blk_end - blks_e
    pos = blk_start[e_flat] * tmb + rank
    src_tok = jnp.zeros((n_rows,), jnp.int32).at[pos].set(jnp.arange(n_assign, dtype=jnp.int32) // TOP_K)
    blk_ids = jnp.arange(n_blk, dtype=jnp.int32)
    n_active = blk_end[-1]
    blk_e = jnp.sum((blk_ids[:, None] >= blk_end[None, :]).astype(jnp.int32), axis=1)
    blk_e = jnp.minimum(blk_e, blk_e[jnp.maximum(n_active - 1, 0)])
    blk_rows = jnp.clip(counts[blk_e] - (blk_ids - blk_start[blk_e]) * tmb, 0, tmb)
    blk_rows = jnp.where(blk_ids < n_active, blk_rows, 0).astype(jnp.int32)

    tn = _tile(d_fe, 256)
    nj = d_fe // tn
    idx2d = src_tok.reshape(n_blk, 1, tmb)
    col = lambda bi, j, be, br: jnp.where(br[bi] > 0, j, nj - 1)
    h = pl.pallas_call(
        functools.partial(_moe_gate_up_kernel, tmb=tmb),
        grid_spec=pltpu.PrefetchScalarGridSpec(
            num_scalar_prefetch=2,
            grid=(n_blk, nj),
            in_specs=[pl.BlockSpec((None, 1, tmb), lambda bi, j, be, br: (bi, 0, 0),
                                   memory_space=pltpu.SMEM),
                      pl.BlockSpec((None, 1, tmb),
                                   lambda bi, j, be, br: (jnp.minimum(bi + 1, n_blk - 1), 0, 0),
                                   memory_space=pltpu.SMEM),
                      pl.BlockSpec(memory_space=pl.ANY),
                      pl.BlockSpec((None, None, d, tn), lambda bi, j, be, br: (jl, be[bi], 0, col(bi, j, be, br))),
                      pl.BlockSpec((None, None, d, tn), lambda bi, j, be, br: (jl, be[bi], 0, col(bi, j, be, br)))],
            out_specs=pl.BlockSpec((tmb, tn), lambda bi, j, be, br: (bi, j)),
            scratch_shapes=[pltpu.VMEM((2, tmb, d), F32),
                            pltpu.VMEM((tmb, d), BF16),
                            pltpu.SemaphoreType.DMA((2,))]),
        out_shape=jax.ShapeDtypeStruct((n_rows, d_fe), BF16),
        compiler_params=_params(2),
        name="moe_gate_up",
    )(blk_e, blk_rows, idx2d, idx2d, xn, w_gate, w_up)

    tn = _tile(d, 512)
    nj = d // tn
    ys = pl.pallas_call(
        functools.partial(_moe_down_kernel, tmb=tmb),
        grid_spec=pltpu.PrefetchScalarGridSpec(
            num_scalar_prefetch=2,
            grid=(n_blk, nj),
            in_specs=[pl.BlockSpec((tmb, d_fe), lambda bi, j, be, br: (bi, 0)),
                      pl.BlockSpec((None, None, d_fe, tn), lambda bi, j, be, br: (jl, be[bi], 0, col(bi, j, be, br)))],
            out_specs=pl.BlockSpec((tmb, tn), lambda bi, j, be, br: (bi, j))),
        out_shape=jax.ShapeDtypeStruct((n_rows, d), F32),
        compiler_params=_params(2),
        name="moe_down",
    )(blk_e, blk_rows, h, w_down)

    tc = _tile(t, 512)
    n_i = t // tc
    pos_blk = pos.reshape(n_i, tc, TOP_K).transpose(0, 2, 1).reshape(n_i, 1, TOP_K * tc)
    if split_rows is None:
        n_first = n_i
        out_specs = [pl.BlockSpec((tc, d), lambda i: (i, 0))]
        out_shape = [jax.ShapeDtypeStruct((t, d), F32)]
    else:
        n_first = split_rows // tc
        out_specs = [pl.BlockSpec((tc, d), lambda i: (jnp.minimum(i, n_first - 1), 0)),
                     pl.BlockSpec((tc, d), lambda i: (jnp.maximum(i - n_first, 0), 0))]
        out_shape = [jax.ShapeDtypeStruct((split_rows, d), F32),
                     jax.ShapeDtypeStruct((t - split_rows, d), F32)]
    return pl.pallas_call(
        functools.partial(_moe_combine_kernel, tc=tc, n_first=n_first),
        grid=(n_i,),
        in_specs=[pl.BlockSpec((None, 1, TOP_K * tc), lambda i: (i, 0, 0), memory_space=pltpu.SMEM),
                  pl.BlockSpec((None, 1, TOP_K * tc), lambda i: (jnp.minimum(i + 1, n_i - 1), 0, 0),
                               memory_space=pltpu.SMEM),
                  pl.BlockSpec((tc, d), lambda i: (i, 0)),
                  pl.BlockSpec((tc, LANES), lambda i: (i, 0)),
                  pl.BlockSpec(memory_space=pl.ANY)],
        out_specs=out_specs,
        out_shape=out_shape,
        scratch_shapes=[pltpu.VMEM((2, TOP_K * tc, d), F32),
                        pltpu.SemaphoreType.DMA((2,))],
        compiler_params=_params(1),
        name="moe_combine",
    )(pos_blk, pos_blk, x, sel, ys)


def fused_matmul(name, grid, lhs, terms, extras, outs, epilogue, which=None):
    lhs = [alts if isinstance(alts, list) else [alts] for alts in lhs]
    flat_lhs = [alt for alts in lhs for alt in alts]
    first_ref = [sum(len(a) for a in lhs[:k]) for k in range(len(lhs))]
    n_lhs, n_terms, n_ex, n_out = len(flat_lhs), len(terms), len(extras), len(outs)
    staged = [k for k, alts in enumerate(lhs) if len(alts) > 1 or alts[0][0].dtype != BF16]

    def kernel(*refs):
        lhs_refs = refs[:n_lhs]
        w_refs = refs[n_lhs:n_lhs + n_terms]
        ex_refs = refs[n_lhs + n_terms:n_lhs + n_terms + n_ex]
        out_refs = refs[n_lhs + n_terms + n_ex:n_lhs + n_terms + n_ex + n_out]
        scr_refs = refs[n_lhs + n_terms + n_ex + n_out:]
        pids = [pl.program_id(a) for a in range(len(grid))]
        if staged:
            first = pids[1] == 0
            for p in pids[2:]:
                first = jnp.logical_and(first, p == 0)
            for s, k in enumerate(staged):
                for a in range(len(lhs[k])):
                    use = first if len(lhs[k]) == 1 else jnp.logical_and(first, which(pids) == a)

                    @pl.when(use)
                    def _(s=s, r=first_ref[k] + a):
                        scr_refs[s][...] = lhs_refs[r][...].astype(BF16)

        accs = []
        for (li, _, _, _), w_ref in zip(terms, w_refs):
            a = scr_refs[staged.index(li)][...] if li in staged else lhs_refs[first_ref[li]][...]
            accs.append(jnp.dot(a, w_ref[...].astype(BF16), preferred_element_type=F32))
        epilogue(accs, ex_refs, out_refs, pids)

    in_specs = ([pl.BlockSpec(bs, im) for _, bs, im in flat_lhs]
                + [pl.BlockSpec(bs, im) for _, _, bs, im in terms]
                + [pl.BlockSpec(bs, im) for _, bs, im in extras])
    args = [a for a, _, _ in flat_lhs] + [w for _, w, _, _ in terms] + [a for a, _, _ in extras]
    scratch = [pltpu.VMEM(tuple(b for b in lhs[k][0][1] if b is not None), BF16) for k in staged]
    res = pl.pallas_call(
        kernel,
        grid=grid,
        in_specs=in_specs,
        out_specs=[pl.BlockSpec(bs, im) for _, bs, im in outs],
        out_shape=[sd for sd, _, _ in outs],
        scratch_shapes=scratch,
        compiler_params=_params(len(grid)),
        name=name,
    )(*args)
    return res


def _segment_mean64(sq):
    r = lax.broadcasted_iota(jnp.int32, (LANES, LANES), 0) >> _log2(HEAD_DIM)
    c = lax.broadcasted_iota(jnp.int32, (LANES, LANES), 1) >> _log2(HEAD_DIM)
    ones = (r == c).astype(BF16)
    hi, mid, lo = _split3(sq)
    dot = functools.partial(jnp.dot, preferred_element_type=F32)
    cols = []
    for j in range(sq.shape[1] // LANES):
        sl = slice(j * LANES, (j + 1) * LANES)
        cols.append(dot(hi[:, sl], ones) + dot(mid[:, sl], ones) + dot(lo[:, sl], ones))
    return jnp.concatenate(cols, axis=1) * (1.0 / HEAD_DIM)


def _attn_prompt_kernel(lam_ref, q_ref, k_ref, v_ref, g_ref, o_ref, *, tq, out_scale):
    qi = pl.program_id(2)
    lam = lam_ref[0]
    q = q_ref[...].astype(F32)
    lane = lax.broadcasted_iota(jnp.int32, q.shape, 1)
    qs = jnp.concatenate([jnp.where(lane < HEAD_DIM, q, 0.0),
                          jnp.where(lane >= HEAD_DIM, q, 0.0)], axis=0).astype(BF16)
    row = lax.broadcasted_iota(jnp.int32, (2 * tq, tq), 0)
    row = jnp.where(row >= tq, row - tq, row)
    col = lax.broadcasted_iota(jnp.int32, (2 * tq, tq), 1)
    delta = col - row

    def body(kb, carry):
        m, l, acc = carry
        start = pl.multiple_of(kb * tq, tq)
        k = k_ref[pl.ds(start, tq), :].astype(BF16)
        v = v_ref[pl.ds(start, tq), :].astype(BF16)
        s = lax.dot_general(qs, k, (((1,), (1,)), ((), ())), preferred_element_type=F32)
        s = jnp.where(delta <= (qi - kb) * tq, s, NEG_INF)
        m_new = jnp.maximum(m, jnp.max(s, axis=-1, keepdims=True))
        alpha = jnp.exp(m - m_new)
        p = jnp.exp(s - m_new)
        l = alpha * l + jnp.sum(p, axis=-1, keepdims=True)
        acc = alpha * acc + jnp.dot(p.astype(BF16), v, preferred_element_type=F32)
        return m_new, l, acc

    m0 = jnp.full((2 * tq, 1), NEG_INF, F32)
    l0 = jnp.zeros((2 * tq, 1), F32)
    acc0 = jnp.zeros((2 * tq, HEAD_LANES), F32)
    _, l, acc = lax.fori_loop(0, qi + 1, body, (m0, l0, acc0))
    o = acc / l
    o = o[:tq] - lam * o[tq:]
    o = o * lax.rsqrt(jnp.mean(o * o, axis=-1, keepdims=True) + EPS) * g_ref[...] * out_scale
    o_ref[...] = o.astype(o_ref.dtype)


def attn_prompt(lam, qn, kn, vu, subln, out_scale, n_batch, seq, n_heads):
    tq = _tile(seq, 256)
    nq = seq // tq
    return pl.pallas_call(
        functools.partial(_attn_prompt_kernel, tq=tq, out_scale=out_scale),
        grid=(n_batch, n_heads, nq),
        in_specs=[pl.BlockSpec(memory_space=pltpu.SMEM),
                  pl.BlockSpec((tq, HEAD_LANES), lambda b, h, i: (b * nq + i, h)),
                  pl.BlockSpec((seq, HEAD_LANES), lambda b, h, i: (b, h)),
                  pl.BlockSpec((seq, HEAD_LANES), lambda b, h, i: (b, h)),
                  pl.BlockSpec((1, HEAD_LANES), lambda b, h, i: (0, 0))],
        out_specs=pl.BlockSpec((tq, HEAD_LANES), lambda b, h, i: (b * nq + i, h)),
        out_shape=jax.ShapeDtypeStruct((n_batch * seq, n_heads * HEAD_LANES), BF16),
        compiler_params=_params(3),
        name="attn_prompt",
    )(lam.reshape(1), qn, kn, vu, subln.reshape(1, HEAD_LANES))


def _attn_sample_kernel(pt_ref, lam_ref, q_ref, kn_ref, vn_ref, *rest, n_heads, t_new, out_scale,
                        pages_per_step):
    del pt_ref
    kc_refs = rest[:pages_per_step]
    vc_refs = rest[pages_per_step:2 * pages_per_step]
    g_ref, o_ref, q_scr, bias_scr, s_scr, m_scr, l_scr, acc_scr = rest[2 * pages_per_step:]
    p = pl.program_id(1)
    n_steps = pl.num_programs(1)
    rows = 2 * n_heads * t_new
    page_rows = kc_refs[0].shape[0]
    nt = (((1,), (1,)), ((), ()))
    log_t, head_mask = _log2(t_new), n_heads - 1

    @pl.when(p == 0)
    def _():
        q = q_ref[0]
        lane = lax.broadcasted_iota(jnp.int32, (t_new, HEAD_LANES), 1)
        pieces = []
        for half in range(2):
            for h in range(n_heads):
                qh = q[:, h * HEAD_LANES:(h + 1) * HEAD_LANES]
                pieces.append(jnp.where((lane >= HEAD_DIM) == bool(half), qh, 0.0))
        q_scr[...] = jnp.concatenate(pieces, axis=0).astype(BF16)
        r = lax.broadcasted_iota(jnp.int32, (rows, page_rows), 0)
        c = lax.broadcasted_iota(jnp.int32, (rows, page_rows), 1)
        bias_scr[...] = jnp.where(((r >> log_t) & head_mask) == (c & head_mask), 0.0, NEG_INF)
        m_scr[...] = jnp.full(m_scr.shape, NEG_INF, F32)
        l_scr[...] = jnp.zeros(l_scr.shape, F32)
        acc_scr[...] = jnp.zeros(acc_scr.shape, F32)

    def update(keys, values, bias):
        q = q_scr[...]
        m_old = m_scr[...]
        m_new = m_old
        for i, k in enumerate(keys):
            s = lax.dot_general(q, k(), nt, preferred_element_type=F32) + bias
            s_scr[i, :, 0:s.shape[1]] = s
            m_new = jnp.maximum(m_new, jnp.max(s, axis=-1, keepdims=True))
        alpha = jnp.exp(m_old - m_new)
        l = alpha * l_scr[...]
        acc = alpha * acc_scr[...]
        for i, v in enumerate(values):
            pe = jnp.exp(s_scr[i, :, 0:bias.shape[1]] - m_new)
            l = l + jnp.sum(pe, axis=-1, keepdims=True)
            acc = acc + jnp.dot(pe.astype(BF16), v(), preferred_element_type=F32)
        l_scr[...] = l
        acc_scr[...] = acc
        m_scr[...] = m_new

    update([lambda r=r: r[...].astype(BF16) for r in kc_refs],
           [lambda r=r: r[...].astype(BF16) for r in vc_refs], bias_scr[...])

    @pl.when(p == n_steps - 1)
    def _():
        new_rows = t_new * n_heads
        pad = jnp.zeros((LANES - new_rows, HEAD_LANES), F32)
        k_new = jnp.concatenate([kn_ref[0], pad], axis=0).astype(BF16)
        v_new = jnp.concatenate([vn_ref[0], pad], axis=0).astype(BF16)
        r = lax.broadcasted_iota(jnp.int32, (rows, LANES), 0)
        c = lax.broadcasted_iota(jnp.int32, (rows, LANES), 1)
        ok = jnp.logical_and(((r >> log_t) & head_mask) == (c & head_mask),
                             (c >> _log2(n_heads)) <= (r & (t_new - 1)))
        update([lambda: k_new], [lambda: v_new], jnp.where(ok, 0.0, NEG_INF))
        lam = lam_ref[0]
        inv_l = 1.0 / l_scr[...]
        outs = []
        for h in range(n_heads):
            r1 = slice(h * t_new, (h + 1) * t_new)
            r2 = slice((n_heads + h) * t_new, (n_heads + h + 1) * t_new)
            o = acc_scr[r1, :] * inv_l[r1] - lam * (acc_scr[r2, :] * inv_l[r2])
            o = o * lax.rsqrt(jnp.mean(o * o, axis=-1, keepdims=True) + EPS) * g_ref[...] * out_scale
            outs.append(o)
        o_ref[0] = jnp.concatenate(outs, axis=1).astype(o_ref.dtype)


def attn_sample(lam, page_table, q_s, k_s, v_s, cache_k, cache_v, layer, subln, out_scale, n_heads):
    db, t_new, width = q_s.shape
    page_rows = cache_k.shape[2]
    new_rows = t_new * n_heads
    n_pages = page_table.shape[1]
    rows = 2 * n_heads * t_new
    assert new_rows <= LANES
    pps = _tile(n_pages, 8)

    def page_spec(g):
        return pl.BlockSpec((None, None, page_rows, HEAD_LANES),
                            lambda b, p, pt: (layer, pt[b, p * pps + g], 0, 0))

    grid_spec = pltpu.PrefetchScalarGridSpec(
        num_scalar_prefetch=1,
        grid=(db, n_pages // pps),
        in_specs=([pl.BlockSpec(memory_space=pltpu.SMEM),
                   pl.BlockSpec((1, t_new, width), lambda b, p, pt: (b, 0, 0)),
                   pl.BlockSpec((1, new_rows, HEAD_LANES), lambda b, p, pt: (b, 0, 0)),
                   pl.BlockSpec((1, new_rows, HEAD_LANES), lambda b, p, pt: (b, 0, 0))]
                  + [page_spec(g) for g in range(pps)] + [page_spec(g) for g in range(pps)]
                  + [pl.BlockSpec((1, HEAD_LANES), lambda b, p, pt: (0, 0))]),
        out_specs=pl.BlockSpec((1, t_new, width), lambda b, p, pt: (b, 0, 0)),
        scratch_shapes=[pltpu.VMEM((rows, HEAD_LANES), BF16),
                        pltpu.VMEM((rows, page_rows), F32),
                        pltpu.VMEM((pps, rows, page_rows), F32),
                        pltpu.VMEM((rows, 1), F32),
                        pltpu.VMEM((rows, 1), F32),
                        pltpu.VMEM((rows, HEAD_LANES), F32)],
    )
    return pl.pallas_call(
        functools.partial(_attn_sample_kernel, n_heads=n_heads, t_new=t_new, out_scale=out_scale,
                          pages_per_step=pps),
        grid_spec=grid_spec,
        out_shape=jax.ShapeDtypeStruct((db, t_new, width), F32),
        compiler_params=_params(2),
        name="attn_sample",
    )(page_table, lam.reshape(1), q_s, k_s, v_s, *([cache_k] * pps), *([cache_v] * pps),
      subln.reshape(1, HEAD_LANES))


def _ssm_kernel(u_ref, h0re_ref, h0im_ref, are_ref, aim_ref, b_ref, c_ref, d_ref,
                y_ref, hre_ref, him_ref, bu_scr, *, nb, tc):
    c_idx = pl.program_id(1)
    sw = are_ref.shape[1]

    @pl.when(c_idx == 0)
    def _():
        hre_ref[...] = h0re_ref[...]
        him_ref[...] = h0im_ref[...]

    u = u_ref[...]
    bu_scr[...] = jnp.dot(u.astype(BF16), b_ref[...], preferred_element_type=F32)
    a_re = are_ref[...]
    a_im = aim_ref[...]
    for r in range(nb // SUBLANES):
        rs = slice(r * SUBLANES, (r + 1) * SUBLANES)

        def body(t, carry, r=r):
            h_re, h_im = carry
            row = pl.multiple_of(t * nb + r * SUBLANES, SUBLANES)
            n_re = a_re * h_re - a_im * h_im + bu_scr[pl.ds(row, SUBLANES), 0:sw]
            n_im = a_re * h_im + a_im * h_re + bu_scr[pl.ds(row, SUBLANES), sw:2 * sw]
            bu_scr[pl.ds(row, SUBLANES), 0:sw] = n_re
            bu_scr[pl.ds(row, SUBLANES), sw:2 * sw] = n_im
            return n_re, n_im

        h_re, h_im = lax.fori_loop(0, tc, body, (hre_ref[rs, :], him_ref[rs, :]))
        hre_ref[rs, :] = h_re
        him_ref[rs, :] = h_im
    y = jnp.dot(bu_scr[...].astype(BF16), c_ref[...], preferred_element_type=F32) + d_ref[...] * u
    y_ref[...] = jax.nn.gelu(y)


def ssm_scan(u_tm, h0_re, h0_im, a_re, a_im, b_blk, c_blk, d, nb, tc):
    rows, ch = u_tm.shape
    n_gb = ch // SSM_BLOCK_CH
    sw = h0_re.shape[1] // n_gb
    n_chunks = rows // (tc * nb)
    blk_rows = tc * nb
    return pl.pallas_call(
        functools.partial(_ssm_kernel, nb=nb, tc=tc),
        grid=(n_gb, n_chunks),
        in_specs=[pl.BlockSpec((blk_rows, SSM_BLOCK_CH), lambda g, c: (c, g)),
                  pl.BlockSpec((nb, sw), lambda g, c: (0, g)),
                  pl.BlockSpec((nb, sw), lambda g, c: (0, g)),
                  pl.BlockSpec((SUBLANES, sw), lambda g, c: (0, g)),
                  pl.BlockSpec((SUBLANES, sw), lambda g, c: (0, g)),
                  pl.BlockSpec((None, SSM_BLOCK_CH, 2 * sw), lambda g, c: (g, 0, 0)),
                  pl.BlockSpec((None, 2 * sw, SSM_BLOCK_CH), lambda g, c: (g, 0, 0)),
                  pl.BlockSpec((1, SSM_BLOCK_CH), lambda g, c: (0, g))],
        out_specs=[pl.BlockSpec((blk_rows, SSM_BLOCK_CH), lambda g, c: (c, g)),
                   pl.BlockSpec((nb, sw), lambda g, c: (0, g)),
                   pl.BlockSpec((nb, sw), lambda g, c: (0, g))],
        out_shape=[jax.ShapeDtypeStruct((rows, ch), F32),
                   jax.ShapeDtypeStruct(h0_re.shape, F32),
                   jax.ShapeDtypeStruct(h0_im.shape, F32)],
        scratch_shapes=[pltpu.VMEM((blk_rows, 2 * sw), F32)],
        compiler_params=_params(2),
        name="ssm_scan",
    )(u_tm, h0_re, h0_im, a_re, a_im, b_blk, c_blk, d)


def _ssm_weights(a_re, a_im, b_re, b_im, c_re, c_im, log_dt):
    n_groups, state = a_re.shape
    gpb = SSM_BLOCK_CH // GROUP_CH
    n_gb = n_groups // gpb
    dt = jnp.exp(log_dt)[:, None]
    mag = jnp.exp(dt * a_re)
    ab_re = mag * jnp.cos(dt * a_im)
    ab_im = mag * jnp.sin(dt * a_im)
    den = a_re * a_re + a_im * a_im
    nr = ab_re - 1.0
    coef_re = (nr * a_re + ab_im * a_im) / den
    coef_im = (ab_im * a_re - nr * a_im) / den
    bb_re = coef_re[..., None] * b_re - coef_im[..., None] * b_im
    bb_im = coef_re[..., None] * b_im + coef_im[..., None] * b_re
    eye = jnp.eye(gpb, dtype=F32)

    def b_block(bb):
        bb = bb.reshape(n_gb, gpb, state, GROUP_CH)
        return jnp.einsum("bgpc,gh->bgchp", bb, eye).reshape(n_gb, gpb * GROUP_CH, gpb * state)

    def c_block(cc):
        cc = cc.reshape(n_gb, gpb, GROUP_CH, state)
        return jnp.einsum("bgcp,gh->bgphc", cc, eye).reshape(n_gb, gpb * state, gpb * GROUP_CH)

    b_blk = jnp.concatenate([b_block(bb_re), b_block(bb_im)], axis=2).astype(BF16)
    c_blk = jnp.concatenate([c_block(c_re), c_block(-c_im)], axis=1).astype(BF16)
    bc = lambda a: jnp.broadcast_to(a.reshape(1, n_groups * state), (SUBLANES, n_groups * state))
    return bc(ab_re), bc(ab_im), b_blk, c_blk


def _lambda_init(layer):
    return 0.8 - 0.6 * math.exp(-0.3 * layer)


def kernel(x_prompt, x_sample, cache_k, cache_v, state_ssm_re, state_ssm_im, page_table, norm_mix, w_in, q_norm, k_norm, lambda_q1, lambda_k1, lambda_q2, lambda_k2, subln, ssm_a_re, ssm_a_im, ssm_b_re, ssm_b_im, ssm_c_re, ssm_c_im, ssm_d, ssm_log_dt, w_glu, w_proj_attn, w_proj_ssm, w_out, norm_ffn, ffn_w_gate, ffn_w_up, ffn_w_down, router, moe_w_gate, moe_w_up, moe_w_down):
    n_b, seq, d = x_prompt.shape
    db, t_new, _ = x_sample.shape
    depth = w_in.shape[0]
    n_heads = cache_k.shape[3]
    qk_w = n_heads * HEAD_LANES
    ssm_w = w_glu.shape[1]
    n_groups, state = ssm_a_re.shape[1:]
    n_state = n_groups * state
    page = cache_k.shape[2]
    tp = n_b * seq
    ts = db * t_new
    t_all = tp + ts
    assert n_b == SUBLANES and db % SUBLANES == 0 and ssm_w % SSM_BLOCK_CH == 0
    assert w_in.shape[2] == 3 * qk_w + ssm_w + 2 * d and ssm_w == qk_w

    tm = _tile(math.gcd(seq, ts), 1024)
    n_i = t_all // tm
    cache_k = cache_k.reshape(cache_k.shape[0], cache_k.shape[1], page * n_heads, HEAD_LANES)
    cache_v = cache_v.reshape(cache_v.shape[0], cache_v.shape[1], page * n_heads, HEAD_LANES)
    x = jnp.concatenate([x_prompt.reshape(tp, d), x_sample.reshape(ts, d)], axis=0)
    zeros_h0 = jnp.zeros((n_b, n_state), F32)
    scale = HEAD_DIM ** -0.5

    n_ip = tp // tm
    prow = lambda i: jnp.minimum(i, n_ip - 1)
    srow = lambda i: jnp.maximum(i - n_ip, 0)
    which_part = lambda pids: jnp.where(pids[0] < n_ip, 0, 1)

    def row_lhs(a):
        return (a, (tm, a.shape[1]), lambda i, j: (i, 0))

    def split_lhs(a_p, a_s):
        return [(a_p, (tm, a_p.shape[1]), lambda i, j: (prow(i), 0)),
                (a_s, (tm, a_s.shape[1]), lambda i, j: (srow(i), 0))]

    def store_epilogue(fn=None):
        def ep(accs, ex, out, pids):
            out[0][...] = (accs[0] if fn is None else fn(accs[0])).astype(out[0].dtype)
        return ep

    k_out, v_out, hrp_out, hip_out, hrs_out, his_out = [], [], [], [], [], []
    for layer in range(depth):
        lam_0 = _lambda_init(layer)
        lam = (jnp.exp(jnp.sum(lambda_q1[layer] * lambda_k1[layer]))
               - jnp.exp(jnp.sum(lambda_q2[layer] * lambda_k2[layer])) + lam_0).astype(F32)
        xn = rmsnorm_cast(x, norm_mix[layer], _tile(t_all, 512))

        tn = _tile(qk_w, 512)
        w_spec = lambda off: ((None, d, tn), lambda i, j: (layer, 0, j + off))

        def head_norm(gain, mult):
            g_row = jnp.tile(gain, tn // HEAD_DIM).reshape(1, tn)

            def ep(accs, ex, out, pids):
                y = accs[0]
                y = y * lax.rsqrt(_segment_mean64(y * y) + EPS) * ex[0][...]
                out[0][...] = (y * mult).astype(out[0].dtype)
            return ep, (g_row, (1, tn), lambda i, j: (0, 0))

        ep_q, ex_q = head_norm(q_norm[layer], scale)
        qn, = fused_matmul(
            "proj_q", (n_i, qk_w // tn), [row_lhs(xn)], [(0, w_in) + w_spec(0)], [ex_q],
            [(jax.ShapeDtypeStruct((t_all, qk_w), BF16), (tm, tn), lambda i, j: (i, j))], ep_q)
        ep_k, ex_k = head_norm(k_norm[layer], 1.0)
        kn, = fused_matmul(
            "proj_k", (n_i, qk_w // tn), [row_lhs(xn)], [(0, w_in) + w_spec(qk_w // tn)], [ex_k],
            [(jax.ShapeDtypeStruct((t_all, qk_w), F32), (tm, tn), lambda i, j: (i, j))], ep_k)
        v_all, = fused_matmul(
            "proj_v", (n_i, qk_w // tn), [row_lhs(xn)], [(0, w_in) + w_spec(2 * qk_w // tn)], [],
            [(jax.ShapeDtypeStruct((t_all, qk_w), F32), (tm, tn), lambda i, j: (i, j))],
            store_epilogue())
        u_all, = fused_matmul(
            "proj_u", (n_i, ssm_w // tn), [row_lhs(xn)], [(0, w_in) + w_spec(3 * qk_w // tn)], [],
            [(jax.ShapeDtypeStruct((t_all, ssm_w), F32), (tm, tn), lambda i, j: (i, j))],
            store_epilogue())
        gates, = fused_matmul(
            "proj_gates", (n_i, 2 * d // tn), [row_lhs(xn)],
            [(0, w_in) + w_spec((3 * qk_w + ssm_w) // tn)], [],
            [(jax.ShapeDtypeStruct((t_all, 2 * d), F32), (tm, tn), lambda i, j: (i, j))],
            store_epilogue(jax.nn.sigmoid))

        out_scale = 1.0 - lam_0
        o_p = attn_prompt(lam, qn, kn, v_all, subln[layer], out_scale, n_b, seq, n_heads)
        k_s = kn[tp:].reshape(db, t_new * n_heads, HEAD_LANES)
        v_s = v_all[tp:].reshape(db, t_new * n_heads, HEAD_LANES)
        q_s = qn[tp:].astype(F32).reshape(db, t_new, qk_w)
        o_s = attn_sample(lam, page_table, q_s, k_s, v_s, cache_k, cache_v, layer, subln[layer],
                          out_scale, n_heads).reshape(ts, qk_w)

        a_re, a_im, b_blk, c_blk = _ssm_weights(
            ssm_a_re[layer], ssm_a_im[layer], ssm_b_re[layer], ssm_b_im[layer],
            ssm_c_re[layer], ssm_c_im[layer], ssm_log_dt[layer])
        d_row = ssm_d[layer].reshape(1, ssm_w)
        u_p = u_all[:tp].reshape(n_b, seq, ssm_w).transpose(1, 0, 2).reshape(tp, ssm_w)
        u_s = u_all[tp:].reshape(db, t_new, ssm_w).transpose(1, 0, 2).reshape(ts, ssm_w)
        yg_p, hrp, hip = ssm_scan(u_p, zeros_h0, zeros_h0, a_re, a_im, b_blk, c_blk, d_row,
                                  n_b, _tile(seq, 256))
        yg_s, hrs, his = ssm_scan(u_s, state_ssm_re[layer].reshape(db, n_state),
                                  state_ssm_im[layer].reshape(db, n_state),
                                  a_re, a_im, b_blk, c_blk, d_row, db, t_new)
        yg_p = yg_p.reshape(seq, n_b, ssm_w).transpose(1, 0, 2).reshape(tp, ssm_w)
        yg_s = yg_s.reshape(t_new, db, ssm_w).transpose(1, 0, 2).reshape(ts, ssm_w)

        def ep_glu(accs, ex, out, pids):
            yg_tile = jnp.where(pids[0] < n_ip, ex[0][...], ex[1][...])
            out[0][...] = (yg_tile * jax.nn.sigmoid(accs[0])).astype(out[0].dtype)

        tn = _tile(ssm_w, 512)
        y_ssm, = fused_matmul(
            "ssm_glu", (n_i, ssm_w // tn), [split_lhs(yg_p, yg_s)],
            [(0, w_glu, (None, ssm_w, tn), lambda i, j: (layer, 0, j))],
            [(yg_p, (tm, tn), lambda i, j: (prow(i), j)),
             (yg_s, (tm, tn), lambda i, j: (srow(i), j))],
            [(jax.ShapeDtypeStruct((t_all, ssm_w), BF16), (tm, tn), lambda i, j: (i, j))], ep_glu,
            which=which_part)

        tn = _tile(d, 512)

        def ep_merge(accs, ex, out, pids):
            out[0][...] = (ex[0][...] * accs[0] + ex[1][...] * accs[1]).astype(out[0].dtype)

        merged, = fused_matmul(
            "merge", (n_i, d // tn), [split_lhs(o_p, o_s), row_lhs(y_ssm)],
            [(0, w_proj_attn, (None, qk_w, tn), lambda i, j: (layer, 0, j)),
             (1, w_proj_ssm, (None, ssm_w, tn), lambda i, j: (layer, 0, j))],
            [(gates, (tm, tn), lambda i, j: (i, j)),
             (gates, (tm, tn), lambda i, j: (i, j + d // tn))],
            [(jax.ShapeDtypeStruct((t_all, d), BF16), (tm, tn), lambda i, j: (i, j))], ep_merge,
            which=which_part)

        def ep_residual(accs, ex, out, pids):
            out[0][...] = ex[0][...] + accs[0]

        x, = fused_matmul(
            "out_proj", (n_i, d // tn), [row_lhs(merged)],
            [(0, w_out, (None, d, tn), lambda i, j: (layer, 0, j))],
            [(x, (tm, tn), lambda i, j: (i, j))],
            [(jax.ShapeDtypeStruct((t_all, d), F32), (tm, tn), lambda i, j: (i, j))], ep_residual)

        def ep_swiglu(accs, ex, out, pids):
            out[0][...] = (jax.nn.silu(accs[0]) * accs[1]).astype(out[0].dtype)

        jl = layer // 2
        if layer % 2 == 0:
            xn2 = rmsnorm_cast(x, norm_ffn[layer], _tile(t_all, 512))
            d_ff = ffn_w_gate.shape[2]
            tn = _tile(d_ff, 512)
            h, = fused_matmul(
                "ffn_gate_up", (n_i, d_ff // tn), [row_lhs(xn2)],
                [(0, ffn_w_gate, (None, d, tn), lambda i, j: (jl, 0, j)),
                 (0, ffn_w_up, (None, d, tn), lambda i, j: (jl, 0, j))], [],
                [(jax.ShapeDtypeStruct((t_all, d_ff), BF16), (tm, tn), lambda i, j: (i, j))],
                ep_swiglu)
            tn = _tile(d, 256)
            x, = fused_matmul(
                "ffn_down", (n_i, d // tn), [row_lhs(h)],
                [(0, ffn_w_down, (None, d_ff, tn), lambda i, j: (jl, 0, j))],
                [(x, (tm, tn), lambda i, j: (i, j))],
                [(jax.ShapeDtypeStruct((t_all, d), F32), (tm, tn), lambda i, j: (i, j))],
                ep_residual)
        else:
            xn2, sel = rmsnorm_router(x, norm_ffn[layer], router[jl], _tile(t_all, 512))
            last = layer == depth - 1
            x_parts = moe_sparse(x, xn2, sel, moe_w_gate, moe_w_up, moe_w_down, jl,
                                 split_rows=tp if last else None)
            x = x_parts[0]

        k_out.append(kn)
        v_out.append(v_all)
        hrp_out.append(hrp)
        hip_out.append(hip)
        hrs_out.append(hrs)
        his_out.append(his)

    def split(parts, lo, hi, shape):
        return jnp.stack([p[lo:hi].reshape(shape) for p in parts], axis=0)

    kv_p = (n_b, seq, n_heads, HEAD_LANES)
    kv_s = (db, t_new, n_heads, HEAD_LANES)
    st = lambda parts, n: jnp.stack([p.reshape(n, n_groups, state) for p in parts], axis=0)
    if depth % 2 == 0:
        x_p, x_s = x_parts
    else:
        x_p, x_s = x[:tp], x[tp:]
    return (x_p.reshape(n_b, seq, d), x_s.reshape(db, t_new, d),
            split(k_out, 0, tp, kv_p), split(v_out, 0, tp, kv_p),
            st(hrp_out, n_b), st(hip_out, n_b),
            split(k_out, tp, t_all, kv_s), split(v_out, tp, t_all, kv_s),
            st(hrs_out, db), st(his_out, db))
```

```python
import functools
import math

import jax
import jax.numpy as jnp
from jax import lax
from jax.experimental import pallas as pl
from jax.experimental.pallas import tpu as pltpu

F32 = jnp.float32
BF16 = jnp.bfloat16

EPS = 1e-6
NEG_INF = -1e30
HEAD_DIM = 64
HEAD_LANES = 2 * HEAD_DIM
LANES = 128
SUBLANES = 8
GROUP_CH = 16
SSM_BLOCK_CH = 256
TOP_K = 2
VMEM_LIMIT_BYTES = 56 * 1024 * 1024


def _tile(dim, pref):
    t = min(pref, dim)
    while dim % t:
        t //= 2
    return t


def _log2(n):
    assert n & (n - 1) == 0
    return n.bit_length() - 1


def _params(n_axes):
    return pltpu.CompilerParams(dimension_semantics=("arbitrary",) * n_axes,
                                vmem_limit_bytes=VMEM_LIMIT_BYTES)


def _rmsnorm_kernel(x_ref, g_ref, o_ref):
    x = x_ref[...]
    ms = jnp.mean(x * x, axis=-1, keepdims=True)
    o_ref[...] = (x * lax.rsqrt(ms + EPS) * g_ref[...]).astype(o_ref.dtype)


def rmsnorm_cast(x, g, tm):
    t, d = x.shape
    return pl.pallas_call(
        _rmsnorm_kernel,
        grid=(t // tm,),
        in_specs=[pl.BlockSpec((tm, d), lambda i: (i, 0)),
                  pl.BlockSpec((1, d), lambda i: (0, 0))],
        out_specs=pl.BlockSpec((tm, d), lambda i: (i, 0)),
        out_shape=jax.ShapeDtypeStruct((t, d), BF16),
        compiler_params=_params(1),
        name="rmsnorm_cast",
    )(x, g.reshape(1, d))


def _split3(a):
    hi = a.astype(BF16)
    r1 = a - hi.astype(F32)
    mid = r1.astype(BF16)
    lo = (r1 - mid.astype(F32)).astype(BF16)
    return hi, mid, lo


def _rmsnorm_router_kernel(x_ref, g_ref, rw_ref, o_ref, gates_ref, *, n_experts):
    x = x_ref[...]
    ms = jnp.mean(x * x, axis=-1, keepdims=True)
    xn = x * lax.rsqrt(ms + EPS) * g_ref[...]
    o_ref[...] = xn.astype(o_ref.dtype)
    xh, xm, xl = _split3(xn)
    wh, wm, wl = _split3(rw_ref[...])
    dot = functools.partial(jnp.dot, preferred_element_type=F32)
    logits = (dot(xh, wh) + (dot(xh, wm) + dot(xm, wh))
              + (dot(xh, wl) + dot(xm, wm) + dot(xl, wh)))
    lane = lax.broadcasted_iota(jnp.int32, logits.shape, 1).astype(F32)
    logits = jnp.where(lane < n_experts, logits, -jnp.inf)
    m1 = jnp.max(logits, axis=-1, keepdims=True)
    i1 = jnp.min(jnp.where(logits == m1, lane, float(LANES)), axis=-1, keepdims=True)
    rest = jnp.where(lane == i1, -jnp.inf, logits)
    m2 = jnp.max(rest, axis=-1, keepdims=True)
    i2 = jnp.min(jnp.where(rest == m2, lane, float(LANES)), axis=-1, keepdims=True)
    e2 = jnp.exp(m2 - m1)
    g1 = 1.0 / (1.0 + e2)
    g2 = e2 / (1.0 + e2)
    gates_ref[...] = jnp.where(lane == 0.0, i1, jnp.where(lane == 1.0, i2, jnp.where(
        lane == 2.0, g1, jnp.where(lane == 3.0, g2, 0.0))))


def rmsnorm_router(x, g, router_w, tm):
    t, d = x.shape
    n_experts = router_w.shape[1]
    rw = jnp.zeros((d, LANES), F32).at[:, :n_experts].set(router_w)
    return pl.pallas_call(
        functools.partial(_rmsnorm_router_kernel, n_experts=n_experts),
        grid=(t // tm,),
        in_specs=[pl.BlockSpec((tm, d), lambda i: (i, 0)),
                  pl.BlockSpec((1, d), lambda i: (0, 0)),
                  pl.BlockSpec((d, LANES), lambda i: (0, 0))],
        out_specs=[pl.BlockSpec((tm, d), lambda i: (i, 0)),
                   pl.BlockSpec((tm, LANES), lambda i: (i, 0))],
        out_shape=[jax.ShapeDtypeStruct((t, d), F32),
                   jax.ShapeDtypeStruct((t, LANES), F32)],
        compiler_params=_params(1),
        name="rmsnorm_router",
    )(x, g.reshape(1, d), rw)


def _issue_row_gather(idx_ref, n_rows, src_hbm, dst_buf, slot, sem):
    def body(r, carry):
        row = idx_ref[0, r]
        pltpu.make_async_copy(src_hbm.at[pl.ds(row, 1)], dst_buf.at[slot, pl.ds(r, 1)],
                              sem.at[slot]).start()
        return carry
    lax.fori_loop(0, n_rows, body, 0, unroll=8)


def _wait_row_gather(n_rows, src_hbm, dst_buf, slot, sem):
    pltpu.make_async_copy(src_hbm.at[pl.ds(0, n_rows)], dst_buf.at[slot], sem.at[slot]).wait()


def _gathered_rows(step, n_steps, first, idx_ref, idx_next_ref, src_hbm, buf, sem, n_rows):
    slot = step % 2

    @pl.when(first)
    def _():
        @pl.when(step == 0)
        def _():
            _issue_row_gather(idx_ref, n_rows, src_hbm, buf, 0, sem)

        _wait_row_gather(n_rows, src_hbm, buf, slot, sem)

        @pl.when(step + 1 < n_steps)
        def _():
            _issue_row_gather(idx_next_ref, n_rows, src_hbm, buf, 1 - slot, sem)

    return slot


def _moe_gather_kernel(idx_ref, idx_next_ref, xn_hbm, o_ref, buf, sem, *, tmb):
    bi = pl.program_id(0)
    slot = _gathered_rows(bi, pl.num_programs(0), bi >= 0, idx_ref, idx_next_ref, xn_hbm, buf, sem, tmb)
    o_ref[...] = buf[slot].astype(o_ref.dtype)


def _moe_gate_up_kernel(blk_e_ref, blk_rows_ref, x_ref, wg_ref, wu_ref, h_ref, *, tmb):
    del blk_e_ref
    bi = pl.program_id(0)
    n_valid = blk_rows_ref[bi]
    half = tmb // 2
    wg = wg_ref[...].astype(BF16)
    wu = wu_ref[...].astype(BF16)
    for k in range(2):
        rs = slice(k * half, (k + 1) * half)

        @pl.when(n_valid > k * half)
        def _(rs=rs):
            a = x_ref[rs, :]
            gate = jnp.dot(a, wg, preferred_element_type=F32)
            up = jnp.dot(a, wu, preferred_element_type=F32)
            h_ref[rs, :] = (jax.nn.silu(gate) * up).astype(h_ref.dtype)

        @pl.when(n_valid <= k * half)
        def _(rs=rs):
            h_ref[rs, :] = jnp.zeros((half, h_ref.shape[1]), h_ref.dtype)


def _moe_down_kernel(blk_e_ref, blk_rows_ref, h_ref, w_ref, y_ref, *, tmb):
    del blk_e_ref
    bi = pl.program_id(0)
    n_valid = blk_rows_ref[bi]
    half = tmb // 2
    w = w_ref[...].astype(BF16)
    for k in range(2):
        rs = slice(k * half, (k + 1) * half)

        @pl.when(n_valid > k * half)
        def _(rs=rs):
            y_ref[rs, :] = jnp.dot(h_ref[rs, :], w, preferred_element_type=F32)

        @pl.when(n_valid <= k * half)
        def _(rs=rs):
            y_ref[rs, :] = jnp.zeros((half, y_ref.shape[1]), F32)


def _moe_combine_kernel(idx_ref, idx_next_ref, x_ref, sel_ref, ys_hbm, *rest, tc, n_first):
    o_refs, (buf, sem) = rest[:-2], rest[-2:]
    i = pl.program_id(0)
    slot = _gathered_rows(i, pl.num_programs(0), i >= 0, idx_ref, idx_next_ref, ys_hbm, buf, sem, 2 * tc)
    g1 = sel_ref[:, TOP_K:TOP_K + 1]
    g2 = sel_ref[:, TOP_K + 1:TOP_K + 2]
    y = x_ref[...] + (g1 * buf[slot, 0:tc, :] + g2 * buf[slot, tc:2 * tc, :])
    if len(o_refs) == 1:
        o_refs[0][...] = y
    else:
        @pl.when(i < n_first)
        def _():
            o_refs[0][...] = y

        @pl.when(i >= n_first)
        def _():
            o_refs[1][...] = y


def moe_sparse(x, xn, sel, w_gate, w_up, w_down, jl, split_rows=None):
    t, d = x.shape
    n_e, _, d_fe = w_gate.shape[1:]
    n_assign = TOP_K * t
    tmb = 1024 if n_assign >= 8 * 1024 else 256
    n_blk = n_assign // tmb + n_e
    n_rows = n_blk * tmb

    e_flat = sel[:, 0:TOP_K].astype(jnp.int32).reshape(n_assign)
    onehot = (e_flat[:, None] == jnp.arange(n_e, dtype=jnp.int32)[None, :]).astype(jnp.int32)
    counts = jnp.sum(onehot, axis=0)
    rank = jnp.sum((jnp.cumsum(onehot, axis=0) - onehot) * onehot, axis=1)
    blks_e = (counts + tmb - 1) // tmb
    blk_end = jnp.cumsum(blks_e)
    blk_start = blk_end - blks_e
    pos = blk_start[e_flat] * tmb + rank
    src_tok = jnp.zeros((n_rows,), jnp.int32).at[pos].set(jnp.arange(n_assign, dtype=jnp.int32) // TOP_K)
    blk_ids = jnp.arange(n_blk, dtype=jnp.int32)
    n_active = blk_end[-1]
    blk_e = jnp.sum((blk_ids[:, None] >= blk_end[None, :]).astype(jnp.int32), axis=1)
    blk_e = jnp.minimum(blk_e, blk_e[jnp.maximum(n_active - 1, 0)])
    blk_rows = jnp.clip(counts[blk_e] - (blk_ids - blk_start[blk_e]) * tmb, 0, tmb)
    blk_rows = jnp.where(blk_ids < n_active, blk_rows, 0).astype(jnp.int32)

    idx2d = src_tok.reshape(n_blk, 1, tmb)
    xs = pl.pallas_call(
        functools.partial(_moe_gather_kernel, tmb=tmb),
        grid=(n_blk,),
        in_specs=[pl.BlockSpec((None, 1, tmb), lambda bi: (bi, 0, 0), memory_space=pltpu.SMEM),
                  pl.BlockSpec((None, 1, tmb), lambda bi: (jnp.minimum(bi + 1, n_blk - 1), 0, 0),
                               memory_space=pltpu.SMEM),
                  pl.BlockSpec(memory_space=pl.ANY)],
        out_specs=pl.BlockSpec((tmb, d), lambda bi: (bi, 0)),
        out_shape=jax.ShapeDtypeStruct((n_rows, d), BF16),
        scratch_shapes=[pltpu.VMEM((2, tmb, d), F32),
                        pltpu.SemaphoreType.DMA((2,))],
        compiler_params=_params(1),
        name="moe_gather",
    )(idx2d, idx2d, xn)

    tn = _tile(d_fe, 256)
    nj = d_fe // tn
    col = lambda bi, j, be, br: jnp.where(br[bi] > 0, j, nj - 1)
    h = pl.pallas_call(
        functools.partial(_moe_gate_up_kernel, tmb=tmb),
        grid_spec=pltpu.PrefetchScalarGridSpec(
            num_scalar_prefetch=2,
            grid=(n_blk, nj),
            in_specs=[pl.BlockSpec((tmb, d), lambda bi, j, be, br: (bi, 0)),
                      pl.BlockSpec((None, None, d, tn), lambda bi, j, be, br: (jl, be[bi], 0, col(bi, j, be, br))),
                      pl.BlockSpec((None, None, d, tn), lambda bi, j, be, br: (jl, be[bi], 0, col(bi, j, be, br)))],
            out_specs=pl.BlockSpec((tmb, tn), lambda bi, j, be, br: (bi, j))),
        out_shape=jax.ShapeDtypeStruct((n_rows, d_fe), BF16),
        compiler_params=_params(2),
        name="moe_gate_up",
    )(blk_e, blk_rows, xs, w_gate, w_up)

    tn = _tile(d, 512)
    nj = d // tn
    ys = pl.pallas_call(
        functools.partial(_moe_down_kernel, tmb=tmb),
        grid_spec=pltpu.PrefetchScalarGridSpec(
            num_scalar_prefetch=2,
            grid=(n_blk, nj),
            in_specs=[pl.BlockSpec((tmb, d_fe), lambda bi, j, be, br: (bi, 0)),
                      pl.BlockSpec((None, None, d_fe, tn), lambda bi, j, be, br: (jl, be[bi], 0, col(bi, j, be, br)))],
            out_specs=pl.BlockSpec((tmb, tn), lambda bi, j, be, br: (bi, j))),
        out_shape=jax.ShapeDtypeStruct((n_rows, d), F32),
        compiler_params=_params(2),
        name="moe_down",
    )(blk_e, blk_rows, h, w_down)

    tc = _tile(t, 512)
    n_i = t // tc
    pos_blk = pos.reshape(n_i, tc, TOP_K).transpose(0, 2, 1).reshape(n_i, 1, TOP_K * tc)
    if split_rows is None:
        n_first = n_i
        out_specs = [pl.BlockSpec((tc, d), lambda i: (i, 0))]
        out_shape = [jax.ShapeDtypeStruct((t, d), F32)]
    else:
        n_first = split_rows // tc
        out_specs = [pl.BlockSpec((tc, d), lambda i: (jnp.minimum(i, n_first - 1), 0)),
                     pl.BlockSpec((tc, d), lambda i: (jnp.maximum(i - n_first, 0), 0))]
        out_shape = [jax.ShapeDtypeStruct((split_rows, d), F32),
                     jax.ShapeDtypeStruct((t - split_rows, d), F32)]
    return pl.pallas_call(
        functools.partial(_moe_combine_kernel, tc=tc, n_first=n_first),
        grid=(n_i,),
        in_specs=[pl.BlockSpec((None, 1, TOP_K * tc), lambda i: (i, 0, 0), memory_space=pltpu.SMEM),
                  pl.BlockSpec((None, 1, TOP_K * tc), lambda i: (jnp.minimum(i + 1, n_i - 1), 0, 0),
                               memory_space=pltpu.SMEM),
                  pl.BlockSpec((tc, d), lambda i: (i, 0)),
                  pl.BlockSpec((tc, LANES), lambda i: (i, 0)),
                  pl.BlockSpec(memory_space=pl.ANY)],
        out_specs=out_specs,
        out_shape=out_shape,
        scratch_shapes=[pltpu.VMEM((2, TOP_K * tc, d), F32),
                        pltpu.SemaphoreType.DMA((2,))],
        compiler_params=_params(1),
        name="moe_combine",
    )(pos_blk, pos_blk, x, sel, ys)


def fused_matmul(name, grid, lhs, terms, extras, outs, epilogue, which=None):
    lhs = [alts if isinstance(alts, list) else [alts] for alts in lhs]
    flat_lhs = [alt for alts in lhs for alt in alts]
    first_ref = [sum(len(a) for a in lhs[:k]) for k in range(len(lhs))]
    n_lhs, n_terms, n_ex, n_out = len(flat_lhs), len(terms), len(extras), len(outs)
    staged = [k for k, alts in enumerate(lhs) if len(alts) > 1 or alts[0][0].dtype != BF16]

    def kernel(*refs):
        lhs_refs = refs[:n_lhs]
        w_refs = refs[n_lhs:n_lhs + n_terms]
        ex_refs = refs[n_lhs + n_terms:n_lhs + n_terms + n_ex]
        out_refs = refs[n_lhs + n_terms + n_ex:n_lhs + n_terms + n_ex + n_out]
        scr_refs = refs[n_lhs + n_terms + n_ex + n_out:]
        pids = [pl.program_id(a) for a in range(len(grid))]
        if staged:
            first = pids[1] == 0
            for p in pids[2:]:
                first = jnp.logical_and(first, p == 0)
            for s, k in enumerate(staged):
                for a in range(len(lhs[k])):
                    use = first if len(lhs[k]) == 1 else jnp.logical_and(first, which(pids) == a)

                    @pl.when(use)
                    def _(s=s, r=first_ref[k] + a):
                        scr_refs[s][...] = lhs_refs[r][...].astype(BF16)

        accs = []
        for (li, _, _, _), w_ref in zip(terms, w_refs):
            a = scr_refs[staged.index(li)][...] if li in staged else lhs_refs[first_ref[li]][...]
            accs.append(jnp.dot(a, w_ref[...].astype(BF16), preferred_element_type=F32))
        epilogue(accs, ex_refs, out_refs, pids)

    in_specs = ([pl.BlockSpec(bs, im) for _, bs, im in flat_lhs]
                + [pl.BlockSpec(bs, im) for _, _, bs, im in terms]
                + [pl.BlockSpec(bs, im) for _, bs, im in extras])
    args = [a for a, _, _ in flat_lhs] + [w for _, w, _, _ in terms] + [a for a, _, _ in extras]
    scratch = [pltpu.VMEM(tuple(b for b in lhs[k][0][1] if b is not None), BF16) for k in staged]
    res = pl.pallas_call(
        kernel,
        grid=grid,
        in_specs=in_specs,
        out_specs=[pl.BlockSpec(bs, im) for _, bs, im in outs],
        out_shape=[sd for sd, _, _ in outs],
        scratch_shapes=scratch,
        compiler_params=_params(len(grid)),
        name=name,
    )(*args)
    return res


def _segment_mean64(sq):
    r = lax.broadcasted_iota(jnp.int32, (LANES, LANES), 0) >> _log2(HEAD_DIM)
    c = lax.broadcasted_iota(jnp.int32, (LANES, LANES), 1) >> _log2(HEAD_DIM)
    ones = (r == c).astype(BF16)
    hi, mid, lo = _split3(sq)
    dot = functools.partial(jnp.dot, preferred_element_type=F32)
    cols = []
    for j in range(sq.shape[1] // LANES):
        sl = slice(j * LANES, (j + 1) * LANES)
        cols.append(dot(hi[:, sl], ones) + dot(mid[:, sl], ones) + dot(lo[:, sl], ones))
    return jnp.concatenate(cols, axis=1) * (1.0 / HEAD_DIM)


ATTN_HEADS_PER_STEP = 2


def _attn_prompt_kernel(lam_ref, q_ref, k_ref, v_ref, g_ref, o_ref, vt_scr, *, tq, out_scale):
    qi = pl.program_id(2)
    lam = lam_ref[0]
    cols = 2 * tq
    n_h = q_ref.shape[1] // HEAD_LANES
    head = lambda h: slice(h * HEAD_LANES, (h + 1) * HEAD_LANES)

    @pl.when(qi == 0)
    def _():
        vt_scr[...] = v_ref[...].astype(F32).T.astype(BF16)

    lane = lax.broadcasted_iota(jnp.int32, (tq, HEAD_LANES), 1)
    qs = []
    for h in range(n_h):
        q = q_ref[:, head(h)].astype(F32)
        qs.append(jnp.concatenate([jnp.where(lane < HEAD_DIM, q, 0.0),
                                   jnp.where(lane >= HEAD_DIM, q, 0.0)], axis=0).astype(BF16))

    def block(kb, carry, diagonal):
        start = pl.multiple_of(kb * tq, tq)
        out = []
        for h in range(n_h):
            m, l, acc = carry[h]
            k = k_ref[pl.ds(start, tq), head(h)]
            s = lax.dot_general(k, qs[h], (((1,), (1,)), ((), ())), preferred_element_type=F32)
            if diagonal:
                key = lax.broadcasted_iota(jnp.int32, s.shape, 0)
                qry = lax.broadcasted_iota(jnp.int32, s.shape, 1)
                qry = jnp.where(qry >= tq, qry - tq, qry)
                s = jnp.where(key <= qry, s, NEG_INF)
            m_new = jnp.maximum(m, jnp.max(s, axis=0, keepdims=True))
            alpha = jnp.exp(m - m_new)
            p = jnp.exp(s - m_new)
            l = alpha * l + jnp.sum(p, axis=0, keepdims=True)
            pv = jnp.dot(vt_scr[head(h), pl.ds(start, tq)], p.astype(BF16),
                         preferred_element_type=F32)
            out.append((m_new, l, alpha * acc + pv))
        return tuple(out)

    carry = tuple((jnp.full((1, cols), NEG_INF, F32), jnp.zeros((1, cols), F32),
                   jnp.zeros((HEAD_LANES, cols), F32)) for _ in range(n_h))
    carry = lax.fori_loop(0, qi, lambda kb, c: block(kb, c, False), carry)
    carry = block(qi, carry, True)
    for h in range(n_h):
        _, l, acc = carry[h]
        ot = acc / l
        o = (ot[:, :tq] - lam * ot[:, tq:]).T
        o = o * lax.rsqrt(jnp.mean(o * o, axis=-1, keepdims=True) + EPS) * g_ref[...] * out_scale
        o_ref[:, head(h)] = o.astype(o_ref.dtype)


def attn_prompt(lam, qn, kb, vb, subln, out_scale, n_batch, seq, n_heads):
    tq = _tile(seq, 256)
    nq = seq // tq
    hw = ATTN_HEADS_PER_STEP * HEAD_LANES
    assert n_heads % ATTN_HEADS_PER_STEP == 0
    return pl.pallas_call(
        functools.partial(_attn_prompt_kernel, tq=tq, out_scale=out_scale),
        grid=(n_batch, n_heads // ATTN_HEADS_PER_STEP, nq),
        in_specs=[pl.BlockSpec(memory_space=pltpu.SMEM),
                  pl.BlockSpec((tq, hw), lambda b, h, i: (b * nq + i, h)),
                  pl.BlockSpec((seq, hw), lambda b, h, i: (b, h)),
                  pl.BlockSpec((seq, hw), lambda b, h, i: (b, h)),
                  pl.BlockSpec((1, HEAD_LANES), lambda b, h, i: (0, 0))],
        out_specs=pl.BlockSpec((tq, hw), lambda b, h, i: (b * nq + i, h)),
        out_shape=jax.ShapeDtypeStruct((n_batch * seq, n_heads * HEAD_LANES), BF16),
        scratch_shapes=[pltpu.VMEM((hw, seq), BF16)],
        compiler_params=_params(3),
        name="attn_prompt",
    )(lam.reshape(1), qn, kb, vb, subln.reshape(1, HEAD_LANES))


def _attn_sample_kernel(pt_ref, lam_ref, q_ref, kn_ref, vn_ref, *rest, n_heads, t_new, out_scale,
                        pages_per_step):
    del pt_ref
    kc_refs = rest[:pages_per_step]
    vc_refs = rest[pages_per_step:2 * pages_per_step]
    g_ref, o_ref, q_scr, bias_scr, s_scr, m_scr, l_scr, acc_scr = rest[2 * pages_per_step:]
    p = pl.program_id(1)
    n_steps = pl.num_programs(1)
    rows = 2 * n_heads * t_new
    page_rows = kc_refs[0].shape[0]
    nt = (((1,), (1,)), ((), ()))
    log_t, head_mask = _log2(t_new), n_heads - 1

    @pl.when(p == 0)
    def _():
        q = q_ref[0]
        lane = lax.broadcasted_iota(jnp.int32, (t_new, HEAD_LANES), 1)
        pieces = []
        for half in range(2):
            for h in range(n_heads):
                qh = q[:, h * HEAD_LANES:(h + 1) * HEAD_LANES]
                pieces.append(jnp.where((lane >= HEAD_DIM) == bool(half), qh, 0.0))
        q_scr[...] = jnp.concatenate(pieces, axis=0).astype(BF16)
        r = lax.broadcasted_iota(jnp.int32, (rows, page_rows), 0)
        c = lax.broadcasted_iota(jnp.int32, (rows, page_rows), 1)
        bias_scr[...] = jnp.where(((r >> log_t) & head_mask) == (c & head_mask), 0.0, NEG_INF)
        m_scr[...] = jnp.full(m_scr.shape, NEG_INF, F32)
        l_scr[...] = jnp.zeros(l_scr.shape, F32)
        acc_scr[...] = jnp.zeros(acc_scr.shape, F32)

    def update(keys, values, bias):
        q = q_scr[...]
        m_old = m_scr[...]
        m_new = m_old
        for i, k in enumerate(keys):
            s = lax.dot_general(q, k(), nt, preferred_element_type=F32) + bias
            s_scr[i, :, 0:s.shape[1]] = s
            m_new = jnp.maximum(m_new, jnp.max(s, axis=-1, keepdims=True))
        alpha = jnp.exp(m_old - m_new)
        l = alpha * l_scr[...]
        acc = alpha * acc_scr[...]
        for i, v in enumerate(values):
            pe = jnp.exp(s_scr[i, :, 0:bias.shape[1]] - m_new)
            l = l + jnp.sum(pe, axis=-1, keepdims=True)
            acc = acc + jnp.dot(pe.astype(BF16), v(), preferred_element_type=F32)
        l_scr[...] = l
        acc_scr[...] = acc
        m_scr[...] = m_new

    update([lambda r=r: r[...].astype(BF16) for r in kc_refs],
           [lambda r=r: r[...].astype(BF16) for r in vc_refs], bias_scr[...])

    @pl.when(p == n_steps - 1)
    def _():
        new_rows = t_new * n_heads
        pad = jnp.zeros((LANES - new_rows, HEAD_LANES), F32)
        k_new = jnp.concatenate([kn_ref[0], pad], axis=0).astype(BF16)
        v_new = jnp.concatenate([vn_ref[0], pad], axis=0).astype(BF16)
        r = lax.broadcasted_iota(jnp.int32, (rows, LANES), 0)
        c = lax.broadcasted_iota(jnp.int32, (rows, LANES), 1)
        ok = jnp.logical_and(((r >> log_t) & head_mask) == (c & head_mask),
                             (c >> _log2(n_heads)) <= (r & (t_new - 1)))
        update([lambda: k_new], [lambda: v_new], jnp.where(ok, 0.0, NEG_INF))
        lam = lam_ref[0]
        inv_l = 1.0 / l_scr[...]
        outs = []
        for h in range(n_heads):
            r1 = slice(h * t_new, (h + 1) * t_new)
            r2 = slice((n_heads + h) * t_new, (n_heads + h + 1) * t_new)
            o = acc_scr[r1, :] * inv_l[r1] - lam * (acc_scr[r2, :] * inv_l[r2])
            o = o * lax.rsqrt(jnp.mean(o * o, axis=-1, keepdims=True) + EPS) * g_ref[...] * out_scale
            outs.append(o)
        o_ref[0] = jnp.concatenate(outs, axis=1).astype(o_ref.dtype)


def attn_sample(lam, page_table, q_s, k_s, v_s, cache_k, cache_v, layer, subln, out_scale, n_heads):
    db, t_new, width = q_s.shape
    page_rows = cache_k.shape[2]
    new_rows = t_new * n_heads
    n_pages = page_table.shape[1]
    rows = 2 * n_heads * t_new
    assert new_rows <= LANES
    pps = _tile(n_pages, 8)

    def page_spec(g):
        return pl.BlockSpec((None, None, page_rows, HEAD_LANES),
                            lambda b, p, pt: (layer, pt[b, p * pps + g], 0, 0))

    grid_spec = pltpu.PrefetchScalarGridSpec(
        num_scalar_prefetch=1,
        grid=(db, n_pages // pps),
        in_specs=([pl.BlockSpec(memory_space=pltpu.SMEM),
                   pl.BlockSpec((1, t_new, width), lambda b, p, pt: (b, 0, 0)),
                   pl.BlockSpec((1, new_rows, HEAD_LANES), lambda b, p, pt: (b, 0, 0)),
                   pl.BlockSpec((1, new_rows, HEAD_LANES), lambda b, p, pt: (b, 0, 0))]
                  + [page_spec(g) for g in range(pps)] + [page_spec(g) for g in range(pps)]
                  + [pl.BlockSpec((1, HEAD_LANES), lambda b, p, pt: (0, 0))]),
        out_specs=pl.BlockSpec((1, t_new, width), lambda b, p, pt: (b, 0, 0)),
        scratch_shapes=[pltpu.VMEM((rows, HEAD_LANES), BF16),
                        pltpu.VMEM((rows, page_rows), F32),
                        pltpu.VMEM((pps, rows, page_rows), F32),
                        pltpu.VMEM((rows, 1), F32),
                        pltpu.VMEM((rows, 1), F32),
                        pltpu.VMEM((rows, HEAD_LANES), F32)],
    )
    return pl.pallas_call(
        functools.partial(_attn_sample_kernel, n_heads=n_heads, t_new=t_new, out_scale=out_scale,
                          pages_per_step=pps),
        grid_spec=grid_spec,
        out_shape=jax.ShapeDtypeStruct((db, t_new, width), F32),
        compiler_params=_params(2),
        name="attn_sample",
    )(page_table, lam.reshape(1), q_s, k_s, v_s, *([cache_k] * pps), *([cache_v] * pps),
      subln.reshape(1, HEAD_LANES))


def _ssm_kernel(u_ref, h0re_ref, h0im_ref, are_ref, aim_ref, b_ref, c_ref, d_ref,
                y_ref, hre_ref, him_ref, bu_scr, *, nb, tc):
    c_idx = pl.program_id(1)
    sw = are_ref.shape[1]

    @pl.when(c_idx == 0)
    def _():
        hre_ref[...] = h0re_ref[...]
        him_ref[...] = h0im_ref[...]

    u = u_ref[...]
    bu_scr[...] = jnp.dot(u.astype(BF16), b_ref[...], preferred_element_type=F32)
    a_re = are_ref[...]
    a_im = aim_ref[...]
    for r in range(nb // SUBLANES):
        rs = slice(r * SUBLANES, (r + 1) * SUBLANES)

        def body(t, carry, r=r):
            h_re, h_im = carry
            row = pl.multiple_of(t * nb + r * SUBLANES, SUBLANES)
            n_re = a_re * h_re - a_im * h_im + bu_scr[pl.ds(row, SUBLANES), 0:sw]
            n_im = a_re * h_im + a_im * h_re + bu_scr[pl.ds(row, SUBLANES), sw:2 * sw]
            bu_scr[pl.ds(row, SUBLANES), 0:sw] = n_re
            bu_scr[pl.ds(row, SUBLANES), sw:2 * sw] = n_im
            return n_re, n_im

        h_re, h_im = lax.fori_loop(0, tc, body, (hre_ref[rs, :], him_ref[rs, :]))
        hre_ref[rs, :] = h_re
        him_ref[rs, :] = h_im
    y = jnp.dot(bu_scr[...].astype(BF16), c_ref[...], preferred_element_type=F32) + d_ref[...] * u
    y_ref[...] = jax.nn.gelu(y)


def ssm_scan(u_tm, h0_re, h0_im, a_re, a_im, b_blk, c_blk, d, nb, tc):
    rows, ch = u_tm.shape
    n_gb = ch // SSM_BLOCK_CH
    sw = h0_re.shape[1] // n_gb
    n_chunks = rows // (tc * nb)
    blk_rows = tc * nb
    return pl.pallas_call(
        functools.partial(_ssm_kernel, nb=nb, tc=tc),
        grid=(n_gb, n_chunks),
        in_specs=[pl.BlockSpec((blk_rows, SSM_BLOCK_CH), lambda g, c: (c, g)),
                  pl.BlockSpec((nb, sw), lambda g, c: (0, g)),
                  pl.BlockSpec((nb, sw), lambda g, c: (0, g)),
                  pl.BlockSpec((SUBLANES, sw), lambda g, c: (0, g)),
                  pl.BlockSpec((SUBLANES, sw), lambda g, c: (0, g)),
                  pl.BlockSpec((None, SSM_BLOCK_CH, 2 * sw), lambda g, c: (g, 0, 0)),
                  pl.BlockSpec((None, 2 * sw, SSM_BLOCK_CH), lambda g, c: (g, 0, 0)),
                  pl.BlockSpec((1, SSM_BLOCK_CH), lambda g, c: (0, g))],
        out_specs=[pl.BlockSpec((blk_rows, SSM_BLOCK_CH), lambda g, c: (c, g)),
                   pl.BlockSpec((nb, sw), lambda g, c: (0, g)),
                   pl.BlockSpec((nb, sw), lambda g, c: (0, g))],
        out_shape=[jax.ShapeDtypeStruct((rows, ch), F32),
                   jax.ShapeDtypeStruct(h0_re.shape, F32),
                   jax.ShapeDtypeStruct(h0_im.shape, F32)],
        scratch_shapes=[pltpu.VMEM((blk_rows, 2 * sw), F32)],
        compiler_params=_params(2),
        name="ssm_scan",
    )(u_tm, h0_re, h0_im, a_re, a_im, b_blk, c_blk, d)


def _ssm_weights(a_re, a_im, b_re, b_im, c_re, c_im, log_dt):
    n_groups, state = a_re.shape
    gpb = SSM_BLOCK_CH // GROUP_CH
    n_gb = n_groups // gpb
    dt = jnp.exp(log_dt)[:, None]
    mag = jnp.exp(dt * a_re)
    ab_re = mag * jnp.cos(dt * a_im)
    ab_im = mag * jnp.sin(dt * a_im)
    den = a_re * a_re + a_im * a_im
    nr = ab_re - 1.0
    coef_re = (nr * a_re + ab_im * a_im) / den
    coef_im = (ab_im * a_re - nr * a_im) / den
    bb_re = coef_re[..., None] * b_re - coef_im[..., None] * b_im
    bb_im = coef_re[..., None] * b_im + coef_im[..., None] * b_re
    eye = jnp.eye(gpb, dtype=F32)

    def b_block(bb):
        bb = bb.reshape(n_gb, gpb, state, GROUP_CH)
        return jnp.einsum("bgpc,gh->bgchp", bb, eye).reshape(n_gb, gpb * GROUP_CH, gpb * state)

    def c_block(cc):
        cc = cc.reshape(n_gb, gpb, GROUP_CH, state)
        return jnp.einsum("bgcp,gh->bgphc", cc, eye).reshape(n_gb, gpb * state, gpb * GROUP_CH)

    b_blk = jnp.concatenate([b_block(bb_re), b_block(bb_im)], axis=2).astype(BF16)
    c_blk = jnp.concatenate([c_block(c_re), c_block(-c_im)], axis=1).astype(BF16)
    bc = lambda a: jnp.broadcast_to(a.reshape(1, n_groups * state), (SUBLANES, n_groups * state))
    return bc(ab_re), bc(ab_im), b_blk, c_blk


def _lambda_init(layer):
    return 0.8 - 0.6 * math.exp(-0.3 * layer)


def kernel(x_prompt, x_sample, cache_k, cache_v, state_ssm_re, state_ssm_im, page_table, norm_mix, w_in, q_norm, k_norm, lambda_q1, lambda_k1, lambda_q2, lambda_k2, subln, ssm_a_re, ssm_a_im, ssm_b_re, ssm_b_im, ssm_c_re, ssm_c_im, ssm_d, ssm_log_dt, w_glu, w_proj_attn, w_proj_ssm, w_out, norm_ffn, ffn_w_gate, ffn_w_up, ffn_w_down, router, moe_w_gate, moe_w_up, moe_w_down):
    n_b, seq, d = x_prompt.shape
    db, t_new, _ = x_sample.shape
    depth = w_in.shape[0]
    n_heads = cache_k.shape[3]
    qk_w = n_heads * HEAD_LANES
    ssm_w = w_glu.shape[1]
    n_groups, state = ssm_a_re.shape[1:]
    n_state = n_groups * state
    page = cache_k.shape[2]
    tp = n_b * seq
    ts = db * t_new
    t_all = tp + ts
    assert n_b == SUBLANES and db % SUBLANES == 0 and ssm_w % SSM_BLOCK_CH == 0
    assert w_in.shape[2] == 3 * qk_w + ssm_w + 2 * d and ssm_w == qk_w

    tm = _tile(math.gcd(seq, ts), 1024)
    n_i = t_all // tm
    cache_k = cache_k.reshape(cache_k.shape[0], cache_k.shape[1], page * n_heads, HEAD_LANES)
    cache_v = cache_v.reshape(cache_v.shape[0], cache_v.shape[1], page * n_heads, HEAD_LANES)
    x = jnp.concatenate([x_prompt.reshape(tp, d), x_sample.reshape(ts, d)], axis=0)
    zeros_h0 = jnp.zeros((n_b, n_state), F32)
    scale = HEAD_DIM ** -0.5

    n_ip = tp // tm
    prow = lambda i: jnp.minimum(i, n_ip - 1)
    srow = lambda i: jnp.maximum(i - n_ip, 0)
    which_part = lambda pids: jnp.where(pids[0] < n_ip, 0, 1)

    def row_lhs(a):
        return (a, (tm, a.shape[1]), lambda i, j: (i, 0))

    def split_lhs(a_p, a_s):
        return [(a_p, (tm, a_p.shape[1]), lambda i, j: (prow(i), 0)),
                (a_s, (tm, a_s.shape[1]), lambda i, j: (srow(i), 0))]

    def store_epilogue(fn=None):
        def ep(accs, ex, out, pids):
            y = accs[0] if fn is None else fn(accs[0])
            for o in out:
                o[...] = y.astype(o.dtype)
        return ep

    def tile_out(width, dtype, rows=None):
        return (jax.ShapeDtypeStruct((t_all if rows is None else rows, width), dtype), (tm, tn),
                lambda i, j: (i, j))

    k_out, v_out, hrp_out, hip_out, hrs_out, his_out = [], [], [], [], [], []
    for layer in range(depth):
        lam_0 = _lambda_init(layer)
        lam = (jnp.exp(jnp.sum(lambda_q1[layer] * lambda_k1[layer]))
               - jnp.exp(jnp.sum(lambda_q2[layer] * lambda_k2[layer])) + lam_0).astype(F32)
        xn = rmsnorm_cast(x, norm_mix[layer], _tile(t_all, 512))

        tn = _tile(qk_w, 512)
        w_spec = lambda off: ((None, d, tn), lambda i, j: (layer, 0, j + off))

        def head_norm(gain, mult):
            g_row = jnp.tile(gain, tn // HEAD_DIM).reshape(1, tn)

            def ep(accs, ex, out, pids):
                y = accs[0]
                y = y * lax.rsqrt(_segment_mean64(y * y) + EPS) * ex[0][...] * mult
                for o in out:
                    o[...] = y.astype(o.dtype)
            return ep, (g_row, (1, tn), lambda i, j: (0, 0))

        ep_q, ex_q = head_norm(q_norm[layer], scale)
        qn, = fused_matmul(
            "proj_q", (n_i, qk_w // tn), [row_lhs(xn)], [(0, w_in) + w_spec(0)], [ex_q],
            [tile_out(qk_w, BF16)], ep_q)
        ep_k, ex_k = head_norm(k_norm[layer], 1.0)
        kn, kn_b = fused_matmul(
            "proj_k", (n_i, qk_w // tn), [row_lhs(xn)], [(0, w_in) + w_spec(qk_w // tn)], [ex_k],
            [tile_out(qk_w, F32), tile_out(qk_w, BF16)], ep_k)
        v_all, v_b = fused_matmul(
            "proj_v", (n_i, qk_w // tn), [row_lhs(xn)], [(0, w_in) + w_spec(2 * qk_w // tn)], [],
            [tile_out(qk_w, F32), tile_out(qk_w, BF16)], store_epilogue())
        u_all, = fused_matmul(
            "proj_u", (n_i, ssm_w // tn), [row_lhs(xn)], [(0, w_in) + w_spec(3 * qk_w // tn)], [],
            [tile_out(ssm_w, F32)], store_epilogue())
        gates, = fused_matmul(
            "proj_gates", (n_i, 2 * d // tn), [row_lhs(xn)],
            [(0, w_in) + w_spec((3 * qk_w + ssm_w) // tn)], [],
            [tile_out(2 * d, F32)], store_epilogue(jax.nn.sigmoid))

        out_scale = 1.0 - lam_0
        o_p = attn_prompt(lam, qn, kn_b, v_b, subln[layer], out_scale, n_b, seq, n_heads)
        k_s = kn[tp:].reshape(db, t_new * n_heads, HEAD_LANES)
        v_s = v_all[tp:].reshape(db, t_new * n_heads, HEAD_LANES)
        q_s = qn[tp:].astype(F32).reshape(db, t_new, qk_w)
        o_s = attn_sample(lam, page_table, q_s, k_s, v_s, cache_k, cache_v, layer, subln[layer],
                          out_scale, n_heads).reshape(ts, qk_w)

        a_re, a_im, b_blk, c_blk = _ssm_weights(
            ssm_a_re[layer], ssm_a_im[layer], ssm_b_re[layer], ssm_b_im[layer],
            ssm_c_re[layer], ssm_c_im[layer], ssm_log_dt[layer])
        d_row = ssm_d[layer].reshape(1, ssm_w)
        u_p = u_all[:tp].reshape(n_b, seq, ssm_w).transpose(1, 0, 2).reshape(tp, ssm_w)
        u_s = u_all[tp:].reshape(db, t_new, ssm_w).transpose(1, 0, 2).reshape(ts, ssm_w)
        yg_p, hrp, hip = ssm_scan(u_p, zeros_h0, zeros_h0, a_re, a_im, b_blk, c_blk, d_row,
                                  n_b, _tile(seq, 256))
        yg_s, hrs, his = ssm_scan(u_s, state_ssm_re[layer].reshape(db, n_state),
                                  state_ssm_im[layer].reshape(db, n_state),
                                  a_re, a_im, b_blk, c_blk, d_row, db, t_new)
        yg_p = yg_p.reshape(seq, n_b, ssm_w).transpose(1, 0, 2).reshape(tp, ssm_w)
        yg_s = yg_s.reshape(t_new, db, ssm_w).transpose(1, 0, 2).reshape(ts, ssm_w)

        def ep_glu(accs, ex, out, pids):
            yg_tile = jnp.where(pids[0] < n_ip, ex[0][...], ex[1][...])
            out[0][...] = (yg_tile * jax.nn.sigmoid(accs[0])).astype(out[0].dtype)

        tn = _tile(ssm_w, 512)
        y_ssm, = fused_matmul(
            "ssm_glu", (n_i, ssm_w // tn), [split_lhs(yg_p, yg_s)],
            [(0, w_glu, (None, ssm_w, tn), lambda i, j: (layer, 0, j))],
            [(yg_p, (tm, tn), lambda i, j: (prow(i), j)),
             (yg_s, (tm, tn), lambda i, j: (srow(i), j))],
            [(jax.ShapeDtypeStruct((t_all, ssm_w), BF16), (tm, tn), lambda i, j: (i, j))], ep_glu,
            which=which_part)

        tn = _tile(d, 512)

        def ep_merge(accs, ex, out, pids):
            out[0][...] = (ex[0][...] * accs[0] + ex[1][...] * accs[1]).astype(out[0].dtype)

        merged, = fused_matmul(
            "merge", (n_i, d // tn), [split_lhs(o_p, o_s), row_lhs(y_ssm)],
            [(0, w_proj_attn, (None, qk_w, tn), lambda i, j: (layer, 0, j)),
             (1, w_proj_ssm, (None, ssm_w, tn), lambda i, j: (layer, 0, j))],
            [(gates, (tm, tn), lambda i, j: (i, j)),
             (gates, (tm, tn), lambda i, j: (i, j + d // tn))],
            [(jax.ShapeDtypeStruct((t_all, d), BF16), (tm, tn), lambda i, j: (i, j))], ep_merge,
            which=which_part)

        def ep_residual(accs, ex, out, pids):
            out[0][...] = ex[0][...] + accs[0]

        x, = fused_matmul(
            "out_proj", (n_i, d // tn), [row_lhs(merged)],
            [(0, w_out, (None, d, tn), lambda i, j: (layer, 0, j))],
            [(x, (tm, tn), lambda i, j: (i, j))],
            [(jax.ShapeDtypeStruct((t_all, d), F32), (tm, tn), lambda i, j: (i, j))], ep_residual)

        def ep_swiglu(accs, ex, out, pids):
            out[0][...] = (jax.nn.silu(accs[0]) * accs[1]).astype(out[0].dtype)

        jl = layer // 2
        if layer % 2 == 0:
            xn2 = rmsnorm_cast(x, norm_ffn[layer], _tile(t_all, 512))
            d_ff = ffn_w_gate.shape[2]
            tn = _tile(d_ff, 512)
            h, = fused_matmul(
                "ffn_gate_up", (n_i, d_ff // tn), [row_lhs(xn2)],
                [(0, ffn_w_gate, (None, d, tn), lambda i, j: (jl, 0, j)),
                 (0, ffn_w_up, (None, d, tn), lambda i, j: (jl, 0, j))], [],
                [(jax.ShapeDtypeStruct((t_all, d_ff), BF16), (tm, tn), lambda i, j: (i, j))],
                ep_swiglu)
            tn = _tile(d, 256)
            x, = fused_matmul(
                "ffn_down", (n_i, d // tn), [row_lhs(h)],
                [(0, ffn_w_down, (None, d_ff, tn), lambda i, j: (jl, 0, j))],
                [(x, (tm, tn), lambda i, j: (i, j))],
                [(jax.ShapeDtypeStruct((t_all, d), F32), (tm, tn), lambda i, j: (i, j))],
                ep_residual)
        else:
            xn2, sel = rmsnorm_router(x, norm_ffn[layer], router[jl], _tile(t_all, 512))
            last = layer == depth - 1
            x_parts = moe_sparse(x, xn2, sel, moe_w_gate, moe_w_up, moe_w_down, jl,
                                 split_rows=tp if last else None)
            x = x_parts[0]

        k_out.append(kn)
        v_out.append(v_all)
        hrp_out.append(hrp)
        hip_out.append(hip)
        hrs_out.append(hrs)
        his_out.append(his)

    def split(parts, lo, hi, shape):
        return jnp.stack([p[lo:hi].reshape(shape) for p in parts], axis=0)

    kv_p = (n_b, seq, n_heads, HEAD_LANES)
    kv_s = (db, t_new, n_heads, HEAD_LANES)
    st = lambda parts, n: jnp.stack([p.reshape(n, n_groups, state) for p in parts], axis=0)
    if depth % 2 == 0:
        x_p, x_s = x_parts
    else:
        x_p, x_s = x[:tp], x[tp:]
    return (x_p.reshape(n_b, seq, d), x_s.reshape(db, t_new, d),
            split(k_out, 0, tp, kv_p), split(v_out, 0, tp, kv_p),
            st(hrp_out, n_b), st(hip_out, n_b),
            split(k_out, tp, t_all, kv_s), split(v_out, tp, t_all, kv_s),
            st(hrs_out, db), st(his_out, db))
```

```python
import functools
import math

import jax
import jax.numpy as jnp
from jax import lax
from jax.experimental import pallas as pl
from jax.experimental.pallas import tpu as pltpu

F32 = jnp.float32
BF16 = jnp.bfloat16

EPS = 1e-6
NEG_INF = -1e30
HEAD_DIM = 64
HEAD_LANES = 2 * HEAD_DIM
LANES = 128
SUBLANES = 8
GROUP_CH = 16
SSM_BLOCK_CH = 256
TOP_K = 2
VMEM_LIMIT_BYTES = 56 * 1024 * 1024


def _tile(dim, pref):
    t = min(pref, dim)
    while dim % t:
        t //= 2
    return t


def _log2(n):
    assert n & (n - 1) == 0
    return n.bit_length() - 1


def _params(n_axes):
    return pltpu.CompilerParams(dimension_semantics=("arbitrary",) * n_axes,
                                vmem_limit_bytes=VMEM_LIMIT_BYTES)


def _rmsnorm_kernel(*refs, n_first):
    x_refs, g_ref, o_ref = refs[:-2], refs[-2], refs[-1]

    def emit(x_ref):
        x = x_ref[...]
        ms = jnp.mean(x * x, axis=-1, keepdims=True)
        o_ref[...] = (x * lax.rsqrt(ms + EPS) * g_ref[...]).astype(o_ref.dtype)

    if len(x_refs) == 1:
        emit(x_refs[0])
    else:
        i = pl.program_id(0)
        pl.when(i < n_first)(lambda: emit(x_refs[0]))
        pl.when(i >= n_first)(lambda: emit(x_refs[1]))


def rmsnorm_cast(parts, g, tm):
    d = parts[0].shape[1]
    t = sum(p.shape[0] for p in parts)
    n_first = parts[0].shape[0] // tm
    if len(parts) == 1:
        x_specs = [pl.BlockSpec((tm, d), lambda i: (i, 0))]
    else:
        assert len(parts) == 2 and all(p.shape[0] % tm == 0 for p in parts)
        x_specs = [pl.BlockSpec((tm, d), lambda i: (jnp.minimum(i, n_first - 1), 0)),
                   pl.BlockSpec((tm, d), lambda i: (jnp.maximum(i - n_first, 0), 0))]
    return pl.pallas_call(
        functools.partial(_rmsnorm_kernel, n_first=n_first),
        grid=(t // tm,),
        in_specs=x_specs + [pl.BlockSpec((1, d), lambda i: (0, 0))],
        out_specs=pl.BlockSpec((tm, d), lambda i: (i, 0)),
        out_shape=jax.ShapeDtypeStruct((t, d), BF16),
        compiler_params=_params(1),
        name="rmsnorm_cast",
    )(*parts, g.reshape(1, d))


def _split3(a):
    hi = a.astype(BF16)
    r1 = a - hi.astype(F32)
    mid = r1.astype(BF16)
    lo = (r1 - mid.astype(F32)).astype(BF16)
    return hi, mid, lo


def _rmsnorm_router_kernel(x_ref, g_ref, rw_ref, o_ref, gates_ref, *, n_experts):
    x = x_ref[...]
    ms = jnp.mean(x * x, axis=-1, keepdims=True)
    xn = x * lax.rsqrt(ms + EPS) * g_ref[...]
    o_ref[...] = xn.astype(o_ref.dtype)
    xh, xm, xl = _split3(xn)
    wh, wm, wl = _split3(rw_ref[...])
    dot = functools.partial(jnp.dot, preferred_element_type=F32)
    logits = (dot(xh, wh) + (dot(xh, wm) + dot(xm, wh))
              + (dot(xh, wl) + dot(xm, wm) + dot(xl, wh)))
    lane = lax.broadcasted_iota(jnp.int32, logits.shape, 1).astype(F32)
    logits = jnp.where(lane < n_experts, logits, -jnp.inf)
    m1 = jnp.max(logits, axis=-1, keepdims=True)
    i1 = jnp.min(jnp.where(logits == m1, lane, float(LANES)), axis=-1, keepdims=True)
    rest = jnp.where(lane == i1, -jnp.inf, logits)
    m2 = jnp.max(rest, axis=-1, keepdims=True)
    i2 = jnp.min(jnp.where(rest == m2, lane, float(LANES)), axis=-1, keepdims=True)
    e2 = jnp.exp(m2 - m1)
    g1 = 1.0 / (1.0 + e2)
    g2 = e2 / (1.0 + e2)
    gates_ref[...] = jnp.where(lane == 0.0, i1, jnp.where(lane == 1.0, i2, jnp.where(
        lane == 2.0, g1, jnp.where(lane == 3.0, g2, 0.0))))


def rmsnorm_router(x, g, router_w, tm):
    t, d = x.shape
    n_experts = router_w.shape[1]
    rw = jnp.zeros((d, LANES), F32).at[:, :n_experts].set(router_w)
    return pl.pallas_call(
        functools.partial(_rmsnorm_router_kernel, n_experts=n_experts),
        grid=(t // tm,),
        in_specs=[pl.BlockSpec((tm, d), lambda i: (i, 0)),
                  pl.BlockSpec((1, d), lambda i: (0, 0)),
                  pl.BlockSpec((d, LANES), lambda i: (0, 0))],
        out_specs=[pl.BlockSpec((tm, d), lambda i: (i, 0)),
                   pl.BlockSpec((tm, LANES), lambda i: (i, 0))],
        out_shape=[jax.ShapeDtypeStruct((t, d), F32),
                   jax.ShapeDtypeStruct((t, LANES), F32)],
        compiler_params=_params(1),
        name="rmsnorm_router",
    )(x, g.reshape(1, d), rw)


def _issue_row_gather(idx_ref, n_rows, src_hbm, dst_buf, slot, sem):
    def body(r, carry):
        row = idx_ref[0, r]
        pltpu.make_async_copy(src_hbm.at[pl.ds(row, 1)], dst_buf.at[slot, pl.ds(r, 1)],
                              sem.at[slot]).start()
        return carry
    lax.fori_loop(0, n_rows, body, 0, unroll=8)


def _wait_row_gather(n_rows, src_hbm, dst_buf, slot, sem):
    pltpu.make_async_copy(src_hbm.at[pl.ds(0, n_rows)], dst_buf.at[slot], sem.at[slot]).wait()


def _gathered_rows(step, n_steps, first, idx_ref, idx_next_ref, src_hbm, buf, sem, n_rows):
    slot = step % 2

    @pl.when(first)
    def _():
        @pl.when(step == 0)
        def _():
            _issue_row_gather(idx_ref, n_rows, src_hbm, buf, 0, sem)

        _wait_row_gather(n_rows, src_hbm, buf, slot, sem)

        @pl.when(step + 1 < n_steps)
        def _():
            _issue_row_gather(idx_next_ref, n_rows, src_hbm, buf, 1 - slot, sem)

    return slot


def _moe_gather_kernel(idx_ref, idx_next_ref, xn_hbm, o_ref, buf, sem, *, tmb):
    bi = pl.program_id(0)
    slot = _gathered_rows(bi, pl.num_programs(0), bi >= 0, idx_ref, idx_next_ref, xn_hbm, buf, sem, tmb)
    o_ref[...] = buf[slot].astype(o_ref.dtype)


def _moe_gate_up_kernel(blk_e_ref, blk_rows_ref, x_ref, wg_ref, wu_ref, h_ref, *, tmb):
    del blk_e_ref
    bi = pl.program_id(0)
    n_valid = blk_rows_ref[bi]
    half = tmb // 2
    wg = wg_ref[...].astype(BF16)
    wu = wu_ref[...].astype(BF16)
    for k in range(2):
        rs = slice(k * half, (k + 1) * half)

        @pl.when(n_valid > k * half)
        def _(rs=rs):
            a = x_ref[rs, :]
            gate = jnp.dot(a, wg, preferred_element_type=F32)
            up = jnp.dot(a, wu, preferred_element_type=F32)
            h_ref[rs, :] = (jax.nn.silu(gate) * up).astype(h_ref.dtype)

        @pl.when(n_valid <= k * half)
        def _(rs=rs):
            h_ref[rs, :] = jnp.zeros((half, h_ref.shape[1]), h_ref.dtype)


def _moe_down_kernel(blk_e_ref, blk_rows_ref, h_ref, w_ref, y_ref, *, tmb):
    del blk_e_ref
    bi = pl.program_id(0)
    n_valid = blk_rows_ref[bi]
    half = tmb // 2
    w = w_ref[...].astype(BF16)
    for k in range(2):
        rs = slice(k * half, (k + 1) * half)

        @pl.when(n_valid > k * half)
        def _(rs=rs):
            y_ref[rs, :] = jnp.dot(h_ref[rs, :], w, preferred_element_type=F32)

        @pl.when(n_valid <= k * half)
        def _(rs=rs):
            y_ref[rs, :] = jnp.zeros((half, y_ref.shape[1]), F32)


def _moe_combine_kernel(idx_ref, idx_next_ref, x_ref, sel_ref, ys_hbm, *rest, tc, n_first):
    o_refs, (buf, sem) = rest[:-2], rest[-2:]
    i = pl.program_id(0)
    slot = _gathered_rows(i, pl.num_programs(0), i >= 0, idx_ref, idx_next_ref, ys_hbm, buf, sem, 2 * tc)
    g1 = sel_ref[:, TOP_K:TOP_K + 1]
    g2 = sel_ref[:, TOP_K + 1:TOP_K + 2]
    y = x_ref[...] + (g1 * buf[slot, 0:tc, :] + g2 * buf[slot, tc:2 * tc, :])
    if len(o_refs) == 1:
        o_refs[0][...] = y
    else:
        @pl.when(i < n_first)
        def _():
            o_refs[0][...] = y

        @pl.when(i >= n_first)
        def _():
            o_refs[1][...] = y


def moe_sparse(x, xn, sel, w_gate, w_up, w_down, jl, split_rows=None):
    t, d = x.shape
    n_e, _, d_fe = w_gate.shape[1:]
    n_assign = TOP_K * t
    tmb = 1024 if n_assign >= 8 * 1024 else 256
    n_blk = n_assign // tmb + n_e
    n_rows = n_blk * tmb

    e_flat = sel[:, 0:TOP_K].astype(jnp.int32).reshape(n_assign)
    onehot = (e_flat[:, None] == jnp.arange(n_e, dtype=jnp.int32)[None, :]).astype(jnp.int32)
    counts = jnp.sum(onehot, axis=0)
    rank = jnp.sum((jnp.cumsum(onehot, axis=0) - onehot) * onehot, axis=1)
    blks_e = (counts + tmb - 1) // tmb
    blk_end = jnp.cumsum(blks_e)
    blk_start = blk_end - blks_e
    pos = blk_start[e_flat] * tmb + rank
    src_tok = (jnp.arange(n_rows, dtype=jnp.int32) % t).at[pos].set(
        jnp.arange(n_assign, dtype=jnp.int32) // TOP_K)
    blk_ids = jnp.arange(n_blk, dtype=jnp.int32)
    n_active = blk_end[-1]
    blk_e = jnp.sum((blk_ids[:, None] >= blk_end[None, :]).astype(jnp.int32), axis=1)
    blk_e = jnp.minimum(blk_e, blk_e[jnp.maximum(n_active - 1, 0)])
    blk_rows = jnp.clip(counts[blk_e] - (blk_ids - blk_start[blk_e]) * tmb, 0, tmb)
    blk_rows = jnp.where(blk_ids < n_active, blk_rows, 0).astype(jnp.int32)

    idx2d = src_tok.reshape(n_blk, 1, tmb)
    xs = pl.pallas_call(
        functools.partial(_moe_gather_kernel, tmb=tmb),
        grid=(n_blk,),
        in_specs=[pl.BlockSpec((None, 1, tmb), lambda bi: (bi, 0, 0), memory_space=pltpu.SMEM),
                  pl.BlockSpec((None, 1, tmb), lambda bi: (jnp.minimum(bi + 1, n_blk - 1), 0, 0),
                               memory_space=pltpu.SMEM),
                  pl.BlockSpec(memory_space=pl.ANY)],
        out_specs=pl.BlockSpec((tmb, d), lambda bi: (bi, 0)),
        out_shape=jax.ShapeDtypeStruct((n_rows, d), BF16),
        scratch_shapes=[pltpu.VMEM((2, tmb, d), F32),
                        pltpu.SemaphoreType.DMA((2,))],
        compiler_params=_params(1),
        name="moe_gather",
    )(idx2d, idx2d, xn)

    tn = _tile(d_fe, 256)
    nj = d_fe // tn
    col = lambda bi, j, be, br: jnp.where(br[bi] > 0, j, nj - 1)
    h = pl.pallas_call(
        functools.partial(_moe_gate_up_kernel, tmb=tmb),
        grid_spec=pltpu.PrefetchScalarGridSpec(
            num_scalar_prefetch=2,
            grid=(n_blk, nj),
            in_specs=[pl.BlockSpec((tmb, d), lambda bi, j, be, br: (bi, 0)),
                      pl.BlockSpec((None, None, d, tn), lambda bi, j, be, br: (jl, be[bi], 0, col(bi, j, be, br))),
                      pl.BlockSpec((None, None, d, tn), lambda bi, j, be, br: (jl, be[bi], 0, col(bi, j, be, br)))],
            out_specs=pl.BlockSpec((tmb, tn), lambda bi, j, be, br: (bi, j))),
        out_shape=jax.ShapeDtypeStruct((n_rows, d_fe), BF16),
        compiler_params=_params(2),
        name="moe_gate_up",
    )(blk_e, blk_rows, xs, w_gate, w_up)

    tn = _tile(d, 512)
    nj = d // tn
    ys = pl.pallas_call(
        functools.partial(_moe_down_kernel, tmb=tmb),
        grid_spec=pltpu.PrefetchScalarGridSpec(
            num_scalar_prefetch=2,
            grid=(n_blk, nj),
            in_specs=[pl.BlockSpec((tmb, d_fe), lambda bi, j, be, br: (bi, 0)),
                      pl.BlockSpec((None, None, d_fe, tn), lambda bi, j, be, br: (jl, be[bi], 0, col(bi, j, be, br)))],
            out_specs=pl.BlockSpec((tmb, tn), lambda bi, j, be, br: (bi, j))),
        out_shape=jax.ShapeDtypeStruct((n_rows, d), F32),
        compiler_params=_params(2),
        name="moe_down",
    )(blk_e, blk_rows, h, w_down)

    tc = _tile(t, 512)
    n_i = t // tc
    pos_blk = pos.reshape(n_i, tc, TOP_K).transpose(0, 2, 1).reshape(n_i, 1, TOP_K * tc)
    if split_rows is None:
        n_first = n_i
        out_specs = [pl.BlockSpec((tc, d), lambda i: (i, 0))]
        out_shape = [jax.ShapeDtypeStruct((t, d), F32)]
    else:
        n_first = split_rows // tc
        out_specs = [pl.BlockSpec((tc, d), lambda i: (jnp.minimum(i, n_first - 1), 0)),
                     pl.BlockSpec((tc, d), lambda i: (jnp.maximum(i - n_first, 0), 0))]
        out_shape = [jax.ShapeDtypeStruct((split_rows, d), F32),
                     jax.ShapeDtypeStruct((t - split_rows, d), F32)]
    return pl.pallas_call(
        functools.partial(_moe_combine_kernel, tc=tc, n_first=n_first),
        grid=(n_i,),
        in_specs=[pl.BlockSpec((None, 1, TOP_K * tc), lambda i: (i, 0, 0), memory_space=pltpu.SMEM),
                  pl.BlockSpec((None, 1, TOP_K * tc), lambda i: (jnp.minimum(i + 1, n_i - 1), 0, 0),
                               memory_space=pltpu.SMEM),
                  pl.BlockSpec((tc, d), lambda i: (i, 0)),
                  pl.BlockSpec((tc, LANES), lambda i: (i, 0)),
                  pl.BlockSpec(memory_space=pl.ANY)],
        out_specs=out_specs,
        out_shape=out_shape,
        scratch_shapes=[pltpu.VMEM((2, TOP_K * tc, d), F32),
                        pltpu.SemaphoreType.DMA((2,))],
        compiler_params=_params(1),
        name="moe_combine",
    )(pos_blk, pos_blk, x, sel, ys)


def fused_matmul(name, grid, lhs, terms, extras, outs, epilogue, which=None, prep=None):
    lhs = [alts if isinstance(alts, list) else [alts] for alts in lhs]
    flat_lhs = [alt for alts in lhs for alt in alts]
    first_ref = [sum(len(a) for a in lhs[:k]) for k in range(len(lhs))]
    n_lhs, n_terms, n_ex, n_out = len(flat_lhs), len(terms), len(extras), len(outs)
    staged = [k for k, alts in enumerate(lhs) if len(alts) > 1 or alts[0][0].dtype != BF16]

    def kernel(*refs):
        lhs_refs = refs[:n_lhs]
        w_refs = refs[n_lhs:n_lhs + n_terms]
        ex_refs = refs[n_lhs + n_terms:n_lhs + n_terms + n_ex]
        out_refs = refs[n_lhs + n_terms + n_ex:n_lhs + n_terms + n_ex + n_out]
        scr_refs = refs[n_lhs + n_terms + n_ex + n_out:]
        pids = [pl.program_id(a) for a in range(len(grid))]
        if staged:
            first = pids[1] == 0
            for p in pids[2:]:
                first = jnp.logical_and(first, p == 0)
            for s, k in enumerate(staged):
                for a in range(len(lhs[k])):
                    use = first if len(lhs[k]) == 1 else jnp.logical_and(first, which(pids) == a)

                    @pl.when(use)
                    def _(s=s, k=k, r=first_ref[k] + a):
                        block = lhs_refs[r][...]
                        if prep and k in prep:
                            block = prep[k](block.astype(F32), ex_refs)
                        scr_refs[s][...] = block.astype(BF16)

        accs = []
        for (li, _, _, _), w_ref in zip(terms, w_refs):
            a = scr_refs[staged.index(li)][...] if li in staged else lhs_refs[first_ref[li]][...]
            accs.append(jnp.dot(a, w_ref[...].astype(BF16), preferred_element_type=F32))
        epilogue(accs, ex_refs, out_refs, pids)

    in_specs = ([pl.BlockSpec(bs, im) for _, bs, im in flat_lhs]
                + [pl.BlockSpec(bs, im) for _, _, bs, im in terms]
                + [pl.BlockSpec(bs, im) for _, bs, im in extras])
    args = [a for a, _, _ in flat_lhs] + [w for _, w, _, _ in terms] + [a for a, _, _ in extras]
    scratch = [pltpu.VMEM(tuple(b for b in lhs[k][0][1] if b is not None), BF16) for k in staged]
    res = pl.pallas_call(
        kernel,
        grid=grid,
        in_specs=in_specs,
        out_specs=[pl.BlockSpec(bs, im) for _, bs, im in outs],
        out_shape=[sd for sd, _, _ in outs],
        scratch_shapes=scratch,
        compiler_params=_params(len(grid)),
        name=name,
    )(*args)
    return res


def _segment_mean64(sq):
    r = lax.broadcasted_iota(jnp.int32, (LANES, LANES), 0) >> _log2(HEAD_DIM)
    c = lax.broadcasted_iota(jnp.int32, (LANES, LANES), 1) >> _log2(HEAD_DIM)
    ones = (r == c).astype(BF16)
    hi, mid, lo = _split3(sq)
    dot = functools.partial(jnp.dot, preferred_element_type=F32)
    cols = []
    for j in range(sq.shape[1] // LANES):
        sl = slice(j * LANES, (j + 1) * LANES)
        cols.append(dot(hi[:, sl], ones) + dot(mid[:, sl], ones) + dot(lo[:, sl], ones))
    return jnp.concatenate(cols, axis=1) * (1.0 / HEAD_DIM)


ATTN_HEADS_PER_STEP = 4


def _attn_prompt_kernel(lam_ref, q_ref, k_ref, v_ref, g_ref, o_ref, vt_scr, *, tq, out_scale):
    qi = pl.program_id(2)
    lam = lam_ref[0]
    cols = 2 * tq
    n_h = q_ref.shape[1] // HEAD_LANES
    head = lambda h: slice(h * HEAD_LANES, (h + 1) * HEAD_LANES)

    @pl.when(qi == 0)
    def _():
        vt_scr[...] = v_ref[...].astype(F32).T.astype(BF16)

    lane = lax.broadcasted_iota(jnp.int32, (tq, HEAD_LANES), 1)
    qs = []
    for h in range(n_h):
        q = q_ref[:, head(h)].astype(F32)
        qs.append(jnp.concatenate([jnp.where(lane < HEAD_DIM, q, 0.0),
                                   jnp.where(lane >= HEAD_DIM, q, 0.0)], axis=0).astype(BF16))

    def block(kb, carry, diagonal):
        start = pl.multiple_of(kb * tq, tq)
        out = []
        for h in range(n_h):
            m, l, acc = carry[h]
            k = k_ref[pl.ds(start, tq), head(h)]
            s = lax.dot_general(k, qs[h], (((1,), (1,)), ((), ())), preferred_element_type=F32)
            if diagonal:
                key = lax.broadcasted_iota(jnp.int32, s.shape, 0)
                qry = lax.broadcasted_iota(jnp.int32, s.shape, 1)
                qry = jnp.where(qry >= tq, qry - tq, qry)
                s = jnp.where(key <= qry, s, NEG_INF)
            m_new = jnp.maximum(m, jnp.max(s, axis=0, keepdims=True))
            alpha = jnp.exp(m - m_new)
            p = jnp.exp(s - m_new)
            l = alpha * l + jnp.sum(p, axis=0, keepdims=True)
            pv = jnp.dot(vt_scr[head(h), pl.ds(start, tq)], p.astype(BF16),
                         preferred_element_type=F32)
            out.append((m_new, l, alpha * acc + pv))
        return tuple(out)

    carry = tuple((jnp.full((1, cols), NEG_INF, F32), jnp.zeros((1, cols), F32),
                   jnp.zeros((HEAD_LANES, cols), F32)) for _ in range(n_h))
    carry = lax.fori_loop(0, qi, lambda kb, c: block(kb, c, False), carry)
    carry = block(qi, carry, True)
    for h in range(n_h):
        _, l, acc = carry[h]
        ot = acc / l
        o = (ot[:, :tq] - lam * ot[:, tq:]).T
        o = o * lax.rsqrt(jnp.mean(o * o, axis=-1, keepdims=True) + EPS) * g_ref[...] * out_scale
        o_ref[:, head(h)] = o.astype(o_ref.dtype)


def attn_prompt(lam, qn, kb, vb, subln, out_scale, n_batch, seq, n_heads):
    tq = _tile(seq, 256)
    nq = seq // tq
    hw = ATTN_HEADS_PER_STEP * HEAD_LANES
    assert n_heads % ATTN_HEADS_PER_STEP == 0
    return pl.pallas_call(
        functools.partial(_attn_prompt_kernel, tq=tq, out_scale=out_scale),
        grid=(n_batch, n_heads // ATTN_HEADS_PER_STEP, nq),
        in_specs=[pl.BlockSpec(memory_space=pltpu.SMEM),
                  pl.BlockSpec((tq, hw), lambda b, h, i: (b * nq + i, h)),
                  pl.BlockSpec((seq, hw), lambda b, h, i: (b, h)),
                  pl.BlockSpec((seq, hw), lambda b, h, i: (b, h)),
                  pl.BlockSpec((1, HEAD_LANES), lambda b, h, i: (0, 0))],
        out_specs=pl.BlockSpec((tq, hw), lambda b, h, i: (b * nq + i, h)),
        out_shape=jax.ShapeDtypeStruct((n_batch * seq, n_heads * HEAD_LANES), BF16),
        scratch_shapes=[pltpu.VMEM((hw, seq), BF16)],
        compiler_params=_params(3),
        name="attn_prompt",
    )(lam.reshape(1), qn, kb, vb, subln.reshape(1, HEAD_LANES))


def _attn_sample_kernel(pt_ref, lam_ref, q_ref, kn_ref, vn_ref, *rest, n_heads, t_new, out_scale,
                        pages_per_step):
    del pt_ref
    kc_refs = rest[:pages_per_step]
    vc_refs = rest[pages_per_step:2 * pages_per_step]
    g_ref, o_ref, q_scr, bias_scr, s_scr, m_scr, l_scr, acc_scr = rest[2 * pages_per_step:]
    p = pl.program_id(1)
    n_steps = pl.num_programs(1)
    rows = 2 * n_heads * t_new
    page_rows = kc_refs[0].shape[0]
    nt = (((1,), (1,)), ((), ()))
    log_t, head_mask = _log2(t_new), n_heads - 1

    @pl.when(p == 0)
    def _():
        q = q_ref[0]
        lane = lax.broadcasted_iota(jnp.int32, (t_new, HEAD_LANES), 1)
        pieces = []
        for half in range(2):
            for h in range(n_heads):
                qh = q[:, h * HEAD_LANES:(h + 1) * HEAD_LANES]
                pieces.append(jnp.where((lane >= HEAD_DIM) == bool(half), qh, 0.0))
        q_scr[...] = jnp.concatenate(pieces, axis=0).astype(BF16)
        r = lax.broadcasted_iota(jnp.int32, (rows, page_rows), 0)
        c = lax.broadcasted_iota(jnp.int32, (rows, page_rows), 1)
        bias_scr[...] = jnp.where(((r >> log_t) & head_mask) == (c & head_mask), 0.0, NEG_INF)
        m_scr[...] = jnp.full(m_scr.shape, NEG_INF, F32)
        l_scr[...] = jnp.zeros(l_scr.shape, F32)
        acc_scr[...] = jnp.zeros(acc_scr.shape, F32)

    def update(keys, values, bias):
        q = q_scr[...]
        m_old = m_scr[...]
        m_new = m_old
        for i, k in enumerate(keys):
            s = lax.dot_general(q, k(), nt, preferred_element_type=F32) + bias
            s_scr[i, :, 0:s.shape[1]] = s
            m_new = jnp.maximum(m_new, jnp.max(s, axis=-1, keepdims=True))
        alpha = jnp.exp(m_old - m_new)
        l = alpha * l_scr[...]
        acc = alpha * acc_scr[...]
        for i, v in enumerate(values):
            pe = jnp.exp(s_scr[i, :, 0:bias.shape[1]] - m_new)
            l = l + jnp.sum(pe, axis=-1, keepdims=True)
            acc = acc + jnp.dot(pe.astype(BF16), v(), preferred_element_type=F32)
        l_scr[...] = l
        acc_scr[...] = acc
        m_scr[...] = m_new

    update([lambda r=r: r[...].astype(BF16) for r in kc_refs],
           [lambda r=r: r[...].astype(BF16) for r in vc_refs], bias_scr[...])

    @pl.when(p == n_steps - 1)
    def _():
        new_rows = t_new * n_heads
        pad = jnp.zeros((LANES - new_rows, HEAD_LANES), F32)
        k_new = jnp.concatenate([kn_ref[0], pad], axis=0).astype(BF16)
        v_new = jnp.concatenate([vn_ref[0], pad], axis=0).astype(BF16)
        r = lax.broadcasted_iota(jnp.int32, (rows, LANES), 0)
        c = lax.broadcasted_iota(jnp.int32, (rows, LANES), 1)
        ok = jnp.logical_and(((r >> log_t) & head_mask) == (c & head_mask),
                             (c >> _log2(n_heads)) <= (r & (t_new - 1)))
        update([lambda: k_new], [lambda: v_new], jnp.where(ok, 0.0, NEG_INF))
        lam = lam_ref[0]
        inv_l = 1.0 / l_scr[...]
        outs = []
        for h in range(n_heads):
            r1 = slice(h * t_new, (h + 1) * t_new)
            r2 = slice((n_heads + h) * t_new, (n_heads + h + 1) * t_new)
            o = acc_scr[r1, :] * inv_l[r1] - lam * (acc_scr[r2, :] * inv_l[r2])
            o = o * lax.rsqrt(jnp.mean(o * o, axis=-1, keepdims=True) + EPS) * g_ref[...] * out_scale
            outs.append(o)
        o_ref[0] = jnp.concatenate(outs, axis=1).astype(o_ref.dtype)


def attn_sample(lam, page_table, q_s, k_s, v_s, cache_k, cache_v, layer, subln, out_scale, n_heads):
    db, t_new, width = q_s.shape
    page_rows = cache_k.shape[2]
    new_rows = t_new * n_heads
    n_pages = page_table.shape[1]
    rows = 2 * n_heads * t_new
    assert new_rows <= LANES
    pps = _tile(n_pages, 8)

    def page_spec(g):
        return pl.BlockSpec((None, None, page_rows, HEAD_LANES),
                            lambda b, p, pt: (layer, pt[b, p * pps + g], 0, 0))

    grid_spec = pltpu.PrefetchScalarGridSpec(
        num_scalar_prefetch=1,
        grid=(db, n_pages // pps),
        in_specs=([pl.BlockSpec(memory_space=pltpu.SMEM),
                   pl.BlockSpec((1, t_new, width), lambda b, p, pt: (b, 0, 0)),
                   pl.BlockSpec((1, new_rows, HEAD_LANES), lambda b, p, pt: (b, 0, 0)),
                   pl.BlockSpec((1, new_rows, HEAD_LANES), lambda b, p, pt: (b, 0, 0))]
                  + [page_spec(g) for g in range(pps)] + [page_spec(g) for g in range(pps)]
                  + [pl.BlockSpec((1, HEAD_LANES), lambda b, p, pt: (0, 0))]),
        out_specs=pl.BlockSpec((1, t_new, width), lambda b, p, pt: (b, 0, 0)),
        scratch_shapes=[pltpu.VMEM((rows, HEAD_LANES), BF16),
                        pltpu.VMEM((rows, page_rows), F32),
                        pltpu.VMEM((pps, rows, page_rows), F32),
                        pltpu.VMEM((rows, 1), F32),
                        pltpu.VMEM((rows, 1), F32),
                        pltpu.VMEM((rows, HEAD_LANES), F32)],
    )
    return pl.pallas_call(
        functools.partial(_attn_sample_kernel, n_heads=n_heads, t_new=t_new, out_scale=out_scale,
                          pages_per_step=pps),
        grid_spec=grid_spec,
        out_shape=jax.ShapeDtypeStruct((db, t_new, width), F32),
        compiler_params=_params(2),
        name="attn_sample",
    )(page_table, lam.reshape(1), q_s, k_s, v_s, *([cache_k] * pps), *([cache_v] * pps),
      subln.reshape(1, HEAD_LANES))


def _ssm_kernel(u_ref, h0re_ref, h0im_ref, are_ref, aim_ref, b_ref, c_ref, d_ref,
                y_ref, hre_ref, him_ref, bu_scr, *, nb, tc):
    c_idx = pl.program_id(1)
    sw = are_ref.shape[1]

    @pl.when(c_idx == 0)
    def _():
        hre_ref[...] = h0re_ref[...]
        him_ref[...] = h0im_ref[...]

    u = u_ref[...]
    bu_scr[...] = jnp.dot(u.astype(BF16), b_ref[...], preferred_element_type=F32)
    a_re = are_ref[...]
    a_im = aim_ref[...]
    for r in range(nb // SUBLANES):
        rs = slice(r * SUBLANES, (r + 1) * SUBLANES)

        def body(t, carry, r=r):
            h_re, h_im = carry
            row = pl.multiple_of(t * nb + r * SUBLANES, SUBLANES)
            n_re = a_re * h_re - a_im * h_im + bu_scr[pl.ds(row, SUBLANES), 0:sw]
            n_im = a_re * h_im + a_im * h_re + bu_scr[pl.ds(row, SUBLANES), sw:2 * sw]
            bu_scr[pl.ds(row, SUBLANES), 0:sw] = n_re
            bu_scr[pl.ds(row, SUBLANES), sw:2 * sw] = n_im
            return n_re, n_im

        h_re, h_im = lax.fori_loop(0, tc, body, (hre_ref[rs, :], him_ref[rs, :]))
        hre_ref[rs, :] = h_re
        him_ref[rs, :] = h_im
    y = jnp.dot(bu_scr[...].astype(BF16), c_ref[...], preferred_element_type=F32) + d_ref[...] * u
    y_ref[...] = jax.nn.gelu(y)


def ssm_scan(u_tm, h0_re, h0_im, a_re, a_im, b_blk, c_blk, d, nb, tc):
    rows, ch = u_tm.shape
    n_gb = ch // SSM_BLOCK_CH
    sw = h0_re.shape[1] // n_gb
    n_chunks = rows // (tc * nb)
    blk_rows = tc * nb
    return pl.pallas_call(
        functools.partial(_ssm_kernel, nb=nb, tc=tc),
        grid=(n_gb, n_chunks),
        in_specs=[pl.BlockSpec((blk_rows, SSM_BLOCK_CH), lambda g, c: (c, g)),
                  pl.BlockSpec((nb, sw), lambda g, c: (0, g)),
                  pl.BlockSpec((nb, sw), lambda g, c: (0, g)),
                  pl.BlockSpec((SUBLANES, sw), lambda g, c: (0, g)),
                  pl.BlockSpec((SUBLANES, sw), lambda g, c: (0, g)),
                  pl.BlockSpec((None, SSM_BLOCK_CH, 2 * sw), lambda g, c: (g, 0, 0)),
                  pl.BlockSpec((None, 2 * sw, SSM_BLOCK_CH), lambda g, c: (g, 0, 0)),
                  pl.BlockSpec((1, SSM_BLOCK_CH), lambda g, c: (0, g))],
        out_specs=[pl.BlockSpec((blk_rows, SSM_BLOCK_CH), lambda g, c: (c, g)),
                   pl.BlockSpec((nb, sw), lambda g, c: (0, g)),
                   pl.BlockSpec((nb, sw), lambda g, c: (0, g))],
        out_shape=[jax.ShapeDtypeStruct((rows, ch), F32),
                   jax.ShapeDtypeStruct(h0_re.shape, F32),
                   jax.ShapeDtypeStruct(h0_im.shape, F32)],
        scratch_shapes=[pltpu.VMEM((blk_rows, 2 * sw), F32)],
        compiler_params=_params(2),
        name="ssm_scan",
    )(u_tm, h0_re, h0_im, a_re, a_im, b_blk, c_blk, d)


def _ssm_weights(a_re, a_im, b_re, b_im, c_re, c_im, log_dt):
    n_groups, state = a_re.shape
    gpb = SSM_BLOCK_CH // GROUP_CH
    n_gb = n_groups // gpb
    dt = jnp.exp(log_dt)[:, None]
    mag = jnp.exp(dt * a_re)
    ab_re = mag * jnp.cos(dt * a_im)
    ab_im = mag * jnp.sin(dt * a_im)
    den = a_re * a_re + a_im * a_im
    nr = ab_re - 1.0
    coef_re = (nr * a_re + ab_im * a_im) / den
    coef_im = (ab_im * a_re - nr * a_im) / den
    bb_re = coef_re[..., None] * b_re - coef_im[..., None] * b_im
    bb_im = coef_re[..., None] * b_im + coef_im[..., None] * b_re
    eye = jnp.eye(gpb, dtype=F32)

    def b_block(bb):
        bb = bb.reshape(n_gb, gpb, state, GROUP_CH)
        return jnp.einsum("bgpc,gh->bgchp", bb, eye).reshape(n_gb, gpb * GROUP_CH, gpb * state)

    def c_block(cc):
        cc = cc.reshape(n_gb, gpb, GROUP_CH, state)
        return jnp.einsum("bgcp,gh->bgphc", cc, eye).reshape(n_gb, gpb * state, gpb * GROUP_CH)

    b_blk = jnp.concatenate([b_block(bb_re), b_block(bb_im)], axis=2).astype(BF16)
    c_blk = jnp.concatenate([c_block(c_re), c_block(-c_im)], axis=1).astype(BF16)
    bc = lambda a: jnp.broadcast_to(a.reshape(1, n_groups * state), (SUBLANES, n_groups * state))
    return bc(ab_re), bc(ab_im), b_blk, c_blk


def _lambda_init(layer):
    return 0.8 - 0.6 * math.exp(-0.3 * layer)


def kernel(x_prompt, x_sample, cache_k, cache_v, state_ssm_re, state_ssm_im, page_table, norm_mix, w_in, q_norm, k_norm, lambda_q1, lambda_k1, lambda_q2, lambda_k2, subln, ssm_a_re, ssm_a_im, ssm_b_re, ssm_b_im, ssm_c_re, ssm_c_im, ssm_d, ssm_log_dt, w_glu, w_proj_attn, w_proj_ssm, w_out, norm_ffn, ffn_w_gate, ffn_w_up, ffn_w_down, router, moe_w_gate, moe_w_up, moe_w_down):
    n_b, seq, d = x_prompt.shape
    db, t_new, _ = x_sample.shape
    depth = w_in.shape[0]
    n_heads = cache_k.shape[3]
    qk_w = n_heads * HEAD_LANES
    ssm_w = w_glu.shape[1]
    n_groups, state = ssm_a_re.shape[1:]
    n_state = n_groups * state
    page = cache_k.shape[2]
    tp = n_b * seq
    ts = db * t_new
    t_all = tp + ts
    assert n_b == SUBLANES and db % SUBLANES == 0 and ssm_w % SSM_BLOCK_CH == 0
    assert w_in.shape[2] == 3 * qk_w + ssm_w + 2 * d and ssm_w == qk_w

    tm = _tile(math.gcd(seq, ts), 1024)
    n_i = t_all // tm
    cache_k = cache_k.reshape(cache_k.shape[0], cache_k.shape[1], page * n_heads, HEAD_LANES)
    cache_v = cache_v.reshape(cache_v.shape[0], cache_v.shape[1], page * n_heads, HEAD_LANES)
    x_parts = [x_prompt.reshape(tp, d), x_sample.reshape(ts, d)]
    zeros_h0 = jnp.zeros((n_b, n_state), F32)
    scale = HEAD_DIM ** -0.5

    n_ip = tp // tm
    prow = lambda i: jnp.minimum(i, n_ip - 1)
    srow = lambda i: jnp.maximum(i - n_ip, 0)
    which_part = lambda pids: jnp.where(pids[0] < n_ip, 0, 1)

    def row_lhs(a):
        return (a, (tm, a.shape[1]), lambda i, j: (i, 0))

    def split_lhs(a_p, a_s):
        return [(a_p, (tm, a_p.shape[1]), lambda i, j: (prow(i), 0)),
                (a_s, (tm, a_s.shape[1]), lambda i, j: (srow(i), 0))]

    k_out, v_out, hrp_out, hip_out, hrs_out, his_out = [], [], [], [], [], []
    for layer in range(depth):
        lam_0 = _lambda_init(layer)
        lam = (jnp.exp(jnp.sum(lambda_q1[layer] * lambda_k1[layer]))
               - jnp.exp(jnp.sum(lambda_q2[layer] * lambda_k2[layer])) + lam_0).astype(F32)
        xn = rmsnorm_cast(x_parts if layer == 0 else [x], norm_mix[layer], _tile(ts, 512))

        tn = _tile(qk_w, 512)
        nq_t, ns_t, ng_t = qk_w // tn, ssm_w // tn, 2 * d // tn
        lo_k, lo_v, lo_u, lo_g = nq_t, 2 * nq_t, 3 * nq_t, 3 * nq_t + ns_t
        gain_row = lambda g: (jnp.tile(g, tn // HEAD_DIM).reshape(1, tn), (1, tn), lambda i, j: (0, 0))

        def ranged_out(width, dtype, lo, n):
            return (jax.ShapeDtypeStruct((t_all, width), dtype), (tm, tn),
                    lambda i, j: (i, jnp.clip(j - lo, 0, n - 1)))

        def ep_proj(accs, ex, out, pids):
            j = pids[1]
            y = accs[0]

            def head_normed(gain_ref, mult):
                return y * lax.rsqrt(_segment_mean64(y * y) + EPS) * gain_ref[...] * mult

            def stores(lo, hi, value_fn, refs):
                @pl.when(jnp.logical_and(j >= lo, j < hi))
                def _():
                    value = value_fn()
                    for r in refs:
                        r[...] = value.astype(r.dtype)

            qn_r, kn_r, knb_r, v_r, vb_r, u_r, g_r = out
            stores(0, lo_k, lambda: head_normed(ex[0], scale), [qn_r])
            stores(lo_k, lo_v, lambda: head_normed(ex[1], 1.0), [kn_r, knb_r])
            stores(lo_v, lo_u, lambda: y, [v_r, vb_r])
            stores(lo_u, lo_g, lambda: y, [u_r])
            stores(lo_g, lo_g + ng_t, lambda: jax.nn.sigmoid(y), [g_r])

        qn, kn, kn_b, v_all, v_b, u_all, gates = fused_matmul(
            "proj_in", (n_i, lo_g + ng_t), [row_lhs(xn)],
            [(0, w_in, (None, d, tn), lambda i, j: (layer, 0, j))],
            [gain_row(q_norm[layer]), gain_row(k_norm[layer])],
            [ranged_out(qk_w, BF16, 0, nq_t),
             ranged_out(qk_w, F32, lo_k, nq_t), ranged_out(qk_w, BF16, lo_k, nq_t),
             ranged_out(qk_w, F32, lo_v, nq_t), ranged_out(qk_w, BF16, lo_v, nq_t),
             ranged_out(ssm_w, F32, lo_u, ns_t),
             ranged_out(2 * d, F32, lo_g, ng_t)], ep_proj)

        out_scale = 1.0 - lam_0
        o_p = attn_prompt(lam, qn, kn_b, v_b, subln[layer], out_scale, n_b, seq, n_heads)
        k_s = kn[tp:].reshape(db, t_new * n_heads, HEAD_LANES)
        v_s = v_all[tp:].reshape(db, t_new * n_heads, HEAD_LANES)
        q_s = qn[tp:].astype(F32).reshape(db, t_new, qk_w)
        o_s = attn_sample(lam, page_table, q_s, k_s, v_s, cache_k, cache_v, layer, subln[layer],
                          out_scale, n_heads).reshape(ts, qk_w)

        a_re, a_im, b_blk, c_blk = _ssm_weights(
            ssm_a_re[layer], ssm_a_im[layer], ssm_b_re[layer], ssm_b_im[layer],
            ssm_c_re[layer], ssm_c_im[layer], ssm_log_dt[layer])
        d_row = ssm_d[layer].reshape(1, ssm_w)
        u_p = u_all[:tp].reshape(n_b, seq, ssm_w).transpose(1, 0, 2).reshape(tp, ssm_w)
        u_s = u_all[tp:].reshape(db, t_new, ssm_w).transpose(1, 0, 2).reshape(ts, ssm_w)
        yg_p, hrp, hip = ssm_scan(u_p, zeros_h0, zeros_h0, a_re, a_im, b_blk, c_blk, d_row,
                                  n_b, _tile(seq, 256))
        yg_s, hrs, his = ssm_scan(u_s, state_ssm_re[layer].reshape(db, n_state),
                                  state_ssm_im[layer].reshape(db, n_state),
                                  a_re, a_im, b_blk, c_blk, d_row, db, t_new)
        yg_p = yg_p.reshape(seq, n_b, ssm_w).transpose(1, 0, 2).reshape(tp, ssm_w)
        yg_s = yg_s.reshape(t_new, db, ssm_w).transpose(1, 0, 2).reshape(ts, ssm_w)

        def ep_glu(accs, ex, out, pids):
            yg_tile = jnp.where(pids[0] < n_ip, ex[0][...], ex[1][...])
            out[0][...] = (yg_tile * jax.nn.sigmoid(accs[0])).astype(out[0].dtype)

        tn = _tile(ssm_w, 512)
        y_ssm, = fused_matmul(
            "ssm_glu", (n_i, ssm_w // tn), [split_lhs(yg_p, yg_s)],
            [(0, w_glu, (None, ssm_w, tn), lambda i, j: (layer, 0, j))],
            [(yg_p, (tm, tn), lambda i, j: (prow(i), j)),
             (yg_s, (tm, tn), lambda i, j: (srow(i), j))],
            [(jax.ShapeDtypeStruct((t_all, ssm_w), BF16), (tm, tn), lambda i, j: (i, j))], ep_glu,
            which=which_part)

        tn = _tile(d, 512)

        def ep_merge(accs, ex, out, pids):
            out[0][...] = (ex[0][...] * accs[0] + ex[1][...] * accs[1]).astype(out[0].dtype)

        merged, = fused_matmul(
            "merge", (n_i, d // tn), [split_lhs(o_p, o_s), row_lhs(y_ssm)],
            [(0, w_proj_attn, (None, qk_w, tn), lambda i, j: (layer, 0, j)),
             (1, w_proj_ssm, (None, ssm_w, tn), lambda i, j: (layer, 0, j))],
            [(gates, (tm, tn), lambda i, j: (i, j)),
             (gates, (tm, tn), lambda i, j: (i, j + d // tn))],
            [(jax.ShapeDtypeStruct((t_all, d), BF16), (tm, tn), lambda i, j: (i, j))], ep_merge,
            which=which_part)

        def ep_residual(accs, ex, out, pids):
            res = ex[0][...] if len(ex) == 1 else jnp.where(pids[0] < n_ip, ex[0][...], ex[1][...])
            out[0][...] = res + accs[0]

        if layer == 0:
            x_tiles = [(x_parts[0], (tm, tn), lambda i, j: (prow(i), j)),
                       (x_parts[1], (tm, tn), lambda i, j: (srow(i), j))]
        else:
            x_tiles = [(x, (tm, tn), lambda i, j: (i, j))]
        x, = fused_matmul(
            "out_proj", (n_i, d // tn), [row_lhs(merged)],
            [(0, w_out, (None, d, tn), lambda i, j: (layer, 0, j))], x_tiles,
            [(jax.ShapeDtypeStruct((t_all, d), F32), (tm, tn), lambda i, j: (i, j))], ep_residual)

        def ep_swiglu(accs, ex, out, pids):
            out[0][...] = (jax.nn.silu(accs[0]) * accs[1]).astype(out[0].dtype)

        jl = layer // 2
        if layer % 2 == 0:
            d_ff = ffn_w_gate.shape[2]
            tn = _tile(d_ff, 512)

            def norm_ffn_rows(block, ex):
                ms = jnp.mean(block * block, axis=-1, keepdims=True)
                return block * lax.rsqrt(ms + EPS) * ex[0][...]

            h, = fused_matmul(
                "ffn_gate_up", (n_i, d_ff // tn), [row_lhs(x)],
                [(0, ffn_w_gate, (None, d, tn), lambda i, j: (jl, 0, j)),
                 (0, ffn_w_up, (None, d, tn), lambda i, j: (jl, 0, j))],
                [(norm_ffn[layer].reshape(1, d), (1, d), lambda i, j: (0, 0))],
                [(jax.ShapeDtypeStruct((t_all, d_ff), BF16), (tm, tn), lambda i, j: (i, j))],
                ep_swiglu, prep={0: norm_ffn_rows})
            tn = _tile(d, 256)
            x, = fused_matmul(
                "ffn_down", (n_i, d // tn), [row_lhs(h)],
                [(0, ffn_w_down, (None, d_ff, tn), lambda i, j: (jl, 0, j))],
                [(x, (tm, tn), lambda i, j: (i, j))],
                [(jax.ShapeDtypeStruct((t_all, d), F32), (tm, tn), lambda i, j: (i, j))],
                ep_residual)
        else:
            xn2, sel = rmsnorm_router(x, norm_ffn[layer], router[jl], _tile(t_all, 512))
            last = layer == depth - 1
            x_out = moe_sparse(x, xn2, sel, moe_w_gate, moe_w_up, moe_w_down, jl,
                               split_rows=tp if last else None)
            x = x_out[0]

        k_out.append(kn)
        v_out.append(v_all)
        hrp_out.append(hrp)
        hip_out.append(hip)
        hrs_out.append(hrs)
        his_out.append(his)

    def split(parts, lo, hi, shape):
        return jnp.stack([p[lo:hi].reshape(shape) for p in parts], axis=0)

    kv_p = (n_b, seq, n_heads, HEAD_LANES)
    kv_s = (db, t_new, n_heads, HEAD_LANES)
    st = lambda parts, n: jnp.stack([p.reshape(n, n_groups, state) for p in parts], axis=0)
    if depth % 2 == 0:
        x_p, x_s = x_out
    else:
        x_p, x_s = x[:tp], x[tp:]
    return (x_p.reshape(n_b, seq, d), x_s.reshape(db, t_new, d),
            split(k_out, 0, tp, kv_p), split(v_out, 0, tp, kv_p),
            st(hrp_out, n_b), st(hip_out, n_b),
            split(k_out, tp, t_all, kv_s), split(v_out, tp, t_all, kv_s),
            st(hrs_out, db), st(his_out, db))
```

```python
import functools
import math

import jax
import jax.numpy as jnp
from jax import lax
from jax.experimental import pallas as pl
from jax.experimental.pallas import tpu as pltpu

F32 = jnp.float32
BF16 = jnp.bfloat16

EPS = 1e-6
NEG_INF = -1e30
HEAD_DIM = 64
HEAD_LANES = 2 * HEAD_DIM
LANES = 128
SUBLANES = 8
GROUP_CH = 16
SSM_BLOCK_CH = 256
TOP_K = 2
VMEM_LIMIT_BYTES = 56 * 1024 * 1024


def _tile(dim, pref):
    t = min(pref, dim)
    while dim % t:
        t //= 2
    return t


def _log2(n):
    assert n & (n - 1) == 0
    return n.bit_length() - 1


def _params(n_axes):
    return pltpu.CompilerParams(dimension_semantics=("arbitrary",) * n_axes,
                                vmem_limit_bytes=VMEM_LIMIT_BYTES)


def _rmsnorm_kernel(*refs, n_first):
    x_refs, g_ref, o_ref = refs[:-2], refs[-2], refs[-1]

    def emit(x_ref):
        x = x_ref[...]
        ms = jnp.mean(x * x, axis=-1, keepdims=True)
        o_ref[...] = (x * lax.rsqrt(ms + EPS) * g_ref[...]).astype(o_ref.dtype)

    if len(x_refs) == 1:
        emit(x_refs[0])
    else:
        i = pl.program_id(0)
        pl.when(i < n_first)(lambda: emit(x_refs[0]))
        pl.when(i >= n_first)(lambda: emit(x_refs[1]))


def rmsnorm_cast(parts, g, tm):
    d = parts[0].shape[1]
    t = sum(p.shape[0] for p in parts)
    n_first = parts[0].shape[0] // tm
    if len(parts) == 1:
        x_specs = [pl.BlockSpec((tm, d), lambda i: (i, 0))]
    else:
        assert len(parts) == 2 and all(p.shape[0] % tm == 0 for p in parts)
        x_specs = [pl.BlockSpec((tm, d), lambda i: (jnp.minimum(i, n_first - 1), 0)),
                   pl.BlockSpec((tm, d), lambda i: (jnp.maximum(i - n_first, 0), 0))]
    return pl.pallas_call(
        functools.partial(_rmsnorm_kernel, n_first=n_first),
        grid=(t // tm,),
        in_specs=x_specs + [pl.BlockSpec((1, d), lambda i: (0, 0))],
        out_specs=pl.BlockSpec((tm, d), lambda i: (i, 0)),
        out_shape=jax.ShapeDtypeStruct((t, d), BF16),
        compiler_params=_params(1),
        name="rmsnorm_cast",
    )(*parts, g.reshape(1, d))


def _split3(a):
    hi = a.astype(BF16)
    r1 = a - hi.astype(F32)
    mid = r1.astype(BF16)
    lo = (r1 - mid.astype(F32)).astype(BF16)
    return hi, mid, lo


def _rmsnorm_router_kernel(x_ref, g_ref, rw_ref, o_ref, gates_ref, *, n_experts):
    x = x_ref[...]
    ms = jnp.mean(x * x, axis=-1, keepdims=True)
    xn = x * lax.rsqrt(ms + EPS) * g_ref[...]
    o_ref[...] = xn.astype(o_ref.dtype)
    xh, xm, xl = _split3(xn)
    wh, wm, wl = _split3(rw_ref[...])
    dot = functools.partial(jnp.dot, preferred_element_type=F32)
    logits = (dot(xh, wh) + (dot(xh, wm) + dot(xm, wh))
              + (dot(xh, wl) + dot(xm, wm) + dot(xl, wh)))
    lane = lax.broadcasted_iota(jnp.int32, logits.shape, 1).astype(F32)
    logits = jnp.where(lane < n_experts, logits, -jnp.inf)
    m1 = jnp.max(logits, axis=-1, keepdims=True)
    i1 = jnp.min(jnp.where(logits == m1, lane, float(LANES)), axis=-1, keepdims=True)
    rest = jnp.where(lane == i1, -jnp.inf, logits)
    m2 = jnp.max(rest, axis=-1, keepdims=True)
    i2 = jnp.min(jnp.where(rest == m2, lane, float(LANES)), axis=-1, keepdims=True)
    e2 = jnp.exp(m2 - m1)
    g1 = 1.0 / (1.0 + e2)
    g2 = e2 / (1.0 + e2)
    gates_ref[...] = jnp.where(lane == 0.0, i1, jnp.where(lane == 1.0, i2, jnp.where(
        lane == 2.0, g1, jnp.where(lane == 3.0, g2, 0.0))))


def rmsnorm_router(x, g, router_w, tm):
    t, d = x.shape
    n_experts = router_w.shape[1]
    rw = jnp.zeros((d, LANES), F32).at[:, :n_experts].set(router_w)
    return pl.pallas_call(
        functools.partial(_rmsnorm_router_kernel, n_experts=n_experts),
        grid=(t // tm,),
        in_specs=[pl.BlockSpec((tm, d), lambda i: (i, 0)),
                  pl.BlockSpec((1, d), lambda i: (0, 0)),
                  pl.BlockSpec((d, LANES), lambda i: (0, 0))],
        out_specs=[pl.BlockSpec((tm, d), lambda i: (i, 0)),
                   pl.BlockSpec((tm, LANES), lambda i: (i, 0))],
        out_shape=[jax.ShapeDtypeStruct((t, d), F32),
                   jax.ShapeDtypeStruct((t, LANES), F32)],
        compiler_params=_params(1),
        name="rmsnorm_router",
    )(x, g.reshape(1, d), rw)


def _issue_row_gather(idx_ref, n_rows, src_hbm, dst_buf, slot, sem):
    def body(r, carry):
        row = idx_ref[0, r]
        pltpu.make_async_copy(src_hbm.at[pl.ds(row, 1)], dst_buf.at[slot, pl.ds(r, 1)],
                              sem.at[slot]).start()
        return carry
    lax.fori_loop(0, n_rows, body, 0, unroll=8)


def _wait_row_gather(n_rows, src_hbm, dst_buf, slot, sem):
    pltpu.make_async_copy(src_hbm.at[pl.ds(0, n_rows)], dst_buf.at[slot], sem.at[slot]).wait()


def _gathered_rows(step, n_steps, first, idx_ref, idx_next_ref, src_hbm, buf, sem, n_rows):
    slot = step % 2

    @pl.when(first)
    def _():
        @pl.when(step == 0)
        def _():
            _issue_row_gather(idx_ref, n_rows, src_hbm, buf, 0, sem)

        _wait_row_gather(n_rows, src_hbm, buf, slot, sem)

        @pl.when(step + 1 < n_steps)
        def _():
            _issue_row_gather(idx_next_ref, n_rows, src_hbm, buf, 1 - slot, sem)

    return slot


def _moe_gather_kernel(idx_ref, idx_next_ref, xn_hbm, o_ref, buf, sem, *, tmb):
    bi = pl.program_id(0)
    slot = _gathered_rows(bi, pl.num_programs(0), bi >= 0, idx_ref, idx_next_ref, xn_hbm, buf, sem, tmb)
    o_ref[...] = buf[slot].astype(o_ref.dtype)


def _moe_gate_up_kernel(blk_e_ref, blk_rows_ref, x_ref, wg_ref, wu_ref, h_ref, *, tmb):
    del blk_e_ref
    bi = pl.program_id(0)
    n_valid = blk_rows_ref[bi]
    half = tmb // 2
    wg = wg_ref[...].astype(BF16)
    wu = wu_ref[...].astype(BF16)
    for k in range(2):
        rs = slice(k * half, (k + 1) * half)

        @pl.when(n_valid > k * half)
        def _(rs=rs):
            a = x_ref[rs, :]
            gate = jnp.dot(a, wg, preferred_element_type=F32)
            up = jnp.dot(a, wu, preferred_element_type=F32)
            h_ref[rs, :] = (jax.nn.silu(gate) * up).astype(h_ref.dtype)

        @pl.when(n_valid <= k * half)
        def _(rs=rs):
            h_ref[rs, :] = jnp.zeros((half, h_ref.shape[1]), h_ref.dtype)


def _moe_down_kernel(blk_e_ref, blk_rows_ref, h_ref, w_ref, y_ref, *, tmb):
    del blk_e_ref
    bi = pl.program_id(0)
    n_valid = blk_rows_ref[bi]
    half = tmb // 2
    w = w_ref[...].astype(BF16)
    for k in range(2):
        rs = slice(k * half, (k + 1) * half)

        @pl.when(n_valid > k * half)
        def _(rs=rs):
            y_ref[rs, :] = jnp.dot(h_ref[rs, :], w, preferred_element_type=F32)

        @pl.when(n_valid <= k * half)
        def _(rs=rs):
            y_ref[rs, :] = jnp.zeros((half, y_ref.shape[1]), F32)


def _moe_combine_kernel(idx_ref, idx_next_ref, x_ref, sel_ref, ys_hbm, *rest, tc, n_first):
    o_refs, (buf, sem) = rest[:-2], rest[-2:]
    i = pl.program_id(0)
    slot = _gathered_rows(i, pl.num_programs(0), i >= 0, idx_ref, idx_next_ref, ys_hbm, buf, sem, 2 * tc)
    g1 = sel_ref[:, TOP_K:TOP_K + 1]
    g2 = sel_ref[:, TOP_K + 1:TOP_K + 2]
    y = x_ref[...] + (g1 * buf[slot, 0:tc, :] + g2 * buf[slot, tc:2 * tc, :])
    if len(o_refs) == 1:
        o_refs[0][...] = y
    else:
        @pl.when(i < n_first)
        def _():
            o_refs[0][...] = y

        @pl.when(i >= n_first)
        def _():
            o_refs[1][...] = y


def moe_sparse(x, xn, sel, w_gate, w_up, w_down, jl, split_rows=None):
    t, d = x.shape
    n_e, _, d_fe = w_gate.shape[1:]
    n_assign = TOP_K * t
    tmb = 1024 if n_assign >= 8 * 1024 else 256
    n_blk = n_assign // tmb + n_e
    n_rows = n_blk * tmb

    e_flat = sel[:, 0:TOP_K].astype(jnp.int32).reshape(n_assign)
    onehot = (e_flat[:, None] == jnp.arange(n_e, dtype=jnp.int32)[None, :]).astype(jnp.int32)
    counts = jnp.sum(onehot, axis=0)
    rank = jnp.sum((jnp.cumsum(onehot, axis=0) - onehot) * onehot, axis=1)
    blks_e = (counts + tmb - 1) // tmb
    blk_end = jnp.cumsum(blks_e)
    blk_start = blk_end - blks_e
    pos = blk_start[e_flat] * tmb + rank
    src_tok = (jnp.arange(n_rows, dtype=jnp.int32) % t).at[pos].set(
        jnp.arange(n_assign, dtype=jnp.int32) // TOP_K)
    blk_ids = jnp.arange(n_blk, dtype=jnp.int32)
    n_active = blk_end[-1]
    blk_e = jnp.sum((blk_ids[:, None] >= blk_end[None, :]).astype(jnp.int32), axis=1)
    blk_e = jnp.minimum(blk_e, blk_e[jnp.maximum(n_active - 1, 0)])
    blk_rows = jnp.clip(counts[blk_e] - (blk_ids - blk_start[blk_e]) * tmb, 0, tmb)
    blk_rows = jnp.where(blk_ids < n_active, blk_rows, 0).astype(jnp.int32)

    idx2d = src_tok.reshape(n_blk, 1, tmb)
    xs = pl.pallas_call(
        functools.partial(_moe_gather_kernel, tmb=tmb),
        grid=(n_blk,),
        in_specs=[pl.BlockSpec((None, 1, tmb), lambda bi: (bi, 0, 0), memory_space=pltpu.SMEM),
                  pl.BlockSpec((None, 1, tmb), lambda bi: (jnp.minimum(bi + 1, n_blk - 1), 0, 0),
                               memory_space=pltpu.SMEM),
                  pl.BlockSpec(memory_space=pl.ANY)],
        out_specs=pl.BlockSpec((tmb, d), lambda bi: (bi, 0)),
        out_shape=jax.ShapeDtypeStruct((n_rows, d), BF16),
        scratch_shapes=[pltpu.VMEM((2, tmb, d), F32),
                        pltpu.SemaphoreType.DMA((2,))],
        compiler_params=_params(1),
        name="moe_gather",
    )(idx2d, idx2d, xn)

    tn = _tile(d_fe, 256)
    nj = d_fe // tn
    col = lambda bi, j, be, br: jnp.where(br[bi] > 0, j, nj - 1)
    h = pl.pallas_call(
        functools.partial(_moe_gate_up_kernel, tmb=tmb),
        grid_spec=pltpu.PrefetchScalarGridSpec(
            num_scalar_prefetch=2,
            grid=(n_blk, nj),
            in_specs=[pl.BlockSpec((tmb, d), lambda bi, j, be, br: (bi, 0)),
                      pl.BlockSpec((None, None, d, tn), lambda bi, j, be, br: (jl, be[bi], 0, col(bi, j, be, br))),
                      pl.BlockSpec((None, None, d, tn), lambda bi, j, be, br: (jl, be[bi], 0, col(bi, j, be, br)))],
            out_specs=pl.BlockSpec((tmb, tn), lambda bi, j, be, br: (bi, j))),
        out_shape=jax.ShapeDtypeStruct((n_rows, d_fe), BF16),
        compiler_params=_params(2),
        name="moe_gate_up",
    )(blk_e, blk_rows, xs, w_gate, w_up)

    tn = _tile(d, 512)
    nj = d // tn
    ys = pl.pallas_call(
        functools.partial(_moe_down_kernel, tmb=tmb),
        grid_spec=pltpu.PrefetchScalarGridSpec(
            num_scalar_prefetch=2,
            grid=(n_blk, nj),
            in_specs=[pl.BlockSpec((tmb, d_fe), lambda bi, j, be, br: (bi, 0)),
                      pl.BlockSpec((None, None, d_fe, tn), lambda bi, j, be, br: (jl, be[bi], 0, col(bi, j, be, br)))],
            out_specs=pl.BlockSpec((tmb, tn), lambda bi, j, be, br: (bi, j))),
        out_shape=jax.ShapeDtypeStruct((n_rows, d), F32),
        compiler_params=_params(2),
        name="moe_down",
    )(blk_e, blk_rows, h, w_down)

    tc = _tile(t, 512)
    n_i = t // tc
    pos_blk = pos.reshape(n_i, tc, TOP_K).transpose(0, 2, 1).reshape(n_i, 1, TOP_K * tc)
    if split_rows is None:
        n_first = n_i
        out_specs = [pl.BlockSpec((tc, d), lambda i: (i, 0))]
        out_shape = [jax.ShapeDtypeStruct((t, d), F32)]
    else:
        n_first = split_rows // tc
        out_specs = [pl.BlockSpec((tc, d), lambda i: (jnp.minimum(i, n_first - 1), 0)),
                     pl.BlockSpec((tc, d), lambda i: (jnp.maximum(i - n_first, 0), 0))]
        out_shape = [jax.ShapeDtypeStruct((split_rows, d), F32),
                     jax.ShapeDtypeStruct((t - split_rows, d), F32)]
    return pl.pallas_call(
        functools.partial(_moe_combine_kernel, tc=tc, n_first=n_first),
        grid=(n_i,),
        in_specs=[pl.BlockSpec((None, 1, TOP_K * tc), lambda i: (i, 0, 0), memory_space=pltpu.SMEM),
                  pl.BlockSpec((None, 1, TOP_K * tc), lambda i: (jnp.minimum(i + 1, n_i - 1), 0, 0),
                               memory_space=pltpu.SMEM),
                  pl.BlockSpec((tc, d), lambda i: (i, 0)),
                  pl.BlockSpec((tc, LANES), lambda i: (i, 0)),
                  pl.BlockSpec(memory_space=pl.ANY)],
        out_specs=out_specs,
        out_shape=out_shape,
        scratch_shapes=[pltpu.VMEM((2, TOP_K * tc, d), F32),
                        pltpu.SemaphoreType.DMA((2,))],
        compiler_params=_params(1),
        name="moe_combine",
    )(pos_blk, pos_blk, x, sel, ys)


def fused_matmul(name, grid, lhs, terms, extras, outs, epilogue, which=None, prep=None):
    lhs = [alts if isinstance(alts, list) else [alts] for alts in lhs]
    flat_lhs = [alt for alts in lhs for alt in alts]
    first_ref = [sum(len(a) for a in lhs[:k]) for k in range(len(lhs))]
    n_lhs, n_terms, n_ex, n_out = len(flat_lhs), len(terms), len(extras), len(outs)
    staged = [k for k, alts in enumerate(lhs) if len(alts) > 1 or alts[0][0].dtype != BF16]

    def kernel(*refs):
        lhs_refs = refs[:n_lhs]
        w_refs = refs[n_lhs:n_lhs + n_terms]
        ex_refs = refs[n_lhs + n_terms:n_lhs + n_terms + n_ex]
        out_refs = refs[n_lhs + n_terms + n_ex:n_lhs + n_terms + n_ex + n_out]
        scr_refs = refs[n_lhs + n_terms + n_ex + n_out:]
        pids = [pl.program_id(a) for a in range(len(grid))]
        if staged:
            first = pids[1] == 0
            for p in pids[2:]:
                first = jnp.logical_and(first, p == 0)
            for s, k in enumerate(staged):
                for a in range(len(lhs[k])):
                    use = first if len(lhs[k]) == 1 else jnp.logical_and(first, which(pids) == a)

                    @pl.when(use)
                    def _(s=s, k=k, r=first_ref[k] + a):
                        block = lhs_refs[r][...]
                        if prep and k in prep:
                            block = prep[k](block.astype(F32), ex_refs)
                        scr_refs[s][...] = block.astype(BF16)

        accs = []
        for (li, _, _, _), w_ref in zip(terms, w_refs):
            a = scr_refs[staged.index(li)][...] if li in staged else lhs_refs[first_ref[li]][...]
            accs.append(jnp.dot(a, w_ref[...].astype(BF16), preferred_element_type=F32))
        epilogue(accs, ex_refs, out_refs, pids)

    in_specs = ([pl.BlockSpec(bs, im) for _, bs, im in flat_lhs]
                + [pl.BlockSpec(bs, im) for _, _, bs, im in terms]
                + [pl.BlockSpec(bs, im) for _, bs, im in extras])
    args = [a for a, _, _ in flat_lhs] + [w for _, w, _, _ in terms] + [a for a, _, _ in extras]
    scratch = [pltpu.VMEM(tuple(b for b in lhs[k][0][1] if b is not None), BF16) for k in staged]
    res = pl.pallas_call(
        kernel,
        grid=grid,
        in_specs=in_specs,
        out_specs=[pl.BlockSpec(bs, im) for _, bs, im in outs],
        out_shape=[sd for sd, _, _ in outs],
        scratch_shapes=scratch,
        compiler_params=_params(len(grid)),
        name=name,
    )(*args)
    return res


def _segment_mean64(sq):
    r = lax.broadcasted_iota(jnp.int32, (LANES, LANES), 0) >> _log2(HEAD_DIM)
    c = lax.broadcasted_iota(jnp.int32, (LANES, LANES), 1) >> _log2(HEAD_DIM)
    ones = (r == c).astype(BF16)
    hi, mid, lo = _split3(sq)
    dot = functools.partial(jnp.dot, preferred_element_type=F32)
    cols = []
    for j in range(sq.shape[1] // LANES):
        sl = slice(j * LANES, (j + 1) * LANES)
        cols.append(dot(hi[:, sl], ones) + dot(mid[:, sl], ones) + dot(lo[:, sl], ones))
    return jnp.concatenate(cols, axis=1) * (1.0 / HEAD_DIM)


ATTN_HEADS_PER_STEP = 4


def _attn_prompt_kernel(lam_ref, q_ref, k_ref, v_ref, g_ref, o_ref, vt_scr, *, tq, out_scale):
    qi = pl.program_id(2)
    lam = lam_ref[0]
    cols = 2 * tq
    n_h = q_ref.shape[1] // HEAD_LANES
    head = lambda h: slice(h * HEAD_LANES, (h + 1) * HEAD_LANES)

    @pl.when(qi == 0)
    def _():
        vt_scr[...] = v_ref[...].astype(F32).T.astype(BF16)

    lane = lax.broadcasted_iota(jnp.int32, (tq, HEAD_LANES), 1)
    qs = []
    for h in range(n_h):
        q = q_ref[:, head(h)].astype(F32)
        qs.append(jnp.concatenate([jnp.where(lane < HEAD_DIM, q, 0.0),
                                   jnp.where(lane >= HEAD_DIM, q, 0.0)], axis=0).astype(BF16))

    def block(kb, carry, diagonal):
        start = pl.multiple_of(kb * tq, tq)
        out = []
        for h in range(n_h):
            m, l, acc = carry[h]
            k = k_ref[pl.ds(start, tq), head(h)]
            s = lax.dot_general(k, qs[h], (((1,), (1,)), ((), ())), preferred_element_type=F32)
            if diagonal:
                key = lax.broadcasted_iota(jnp.int32, s.shape, 0)
                qry = lax.broadcasted_iota(jnp.int32, s.shape, 1)
                qry = jnp.where(qry >= tq, qry - tq, qry)
                s = jnp.where(key <= qry, s, NEG_INF)
            m_new = jnp.maximum(m, jnp.max(s, axis=0, keepdims=True))
            alpha = jnp.exp(m - m_new)
            p = jnp.exp(s - m_new)
            l = alpha * l + jnp.sum(p, axis=0, keepdims=True)
            pv = jnp.dot(vt_scr[head(h), pl.ds(start, tq)], p.astype(BF16),
                         preferred_element_type=F32)
            out.append((m_new, l, alpha * acc + pv))
        return tuple(out)

    carry = tuple((jnp.full((1, cols), NEG_INF, F32), jnp.zeros((1, cols), F32),
                   jnp.zeros((HEAD_LANES, cols), F32)) for _ in range(n_h))
    carry = lax.fori_loop(0, qi, lambda kb, c: block(kb, c, False), carry)
    carry = block(qi, carry, True)
    for h in range(n_h):
        _, l, acc = carry[h]
        ot = acc / l
        o = (ot[:, :tq] - lam * ot[:, tq:]).T
        o = o * lax.rsqrt(jnp.mean(o * o, axis=-1, keepdims=True) + EPS) * g_ref[...] * out_scale
        o_ref[:, head(h)] = o.astype(o_ref.dtype)


def attn_prompt(lam, qn, kb, vb, subln, out_scale, n_batch, seq, n_heads):
    tq = _tile(seq, 256)
    nq = seq // tq
    hw = ATTN_HEADS_PER_STEP * HEAD_LANES
    assert n_heads % ATTN_HEADS_PER_STEP == 0
    return pl.pallas_call(
        functools.partial(_attn_prompt_kernel, tq=tq, out_scale=out_scale),
        grid=(n_batch, n_heads // ATTN_HEADS_PER_STEP, nq),
        in_specs=[pl.BlockSpec(memory_space=pltpu.SMEM),
                  pl.BlockSpec((tq, hw), lambda b, h, i: (b * nq + i, h)),
                  pl.BlockSpec((seq, hw), lambda b, h, i: (b, h)),
                  pl.BlockSpec((seq, hw), lambda b, h, i: (b, h)),
                  pl.BlockSpec((1, HEAD_LANES), lambda b, h, i: (0, 0))],
        out_specs=pl.BlockSpec((tq, hw), lambda b, h, i: (b * nq + i, h)),
        out_shape=jax.ShapeDtypeStruct((n_batch * seq, n_heads * HEAD_LANES), BF16),
        scratch_shapes=[pltpu.VMEM((hw, seq), BF16)],
        compiler_params=_params(3),
        name="attn_prompt",
    )(lam.reshape(1), qn, kb, vb, subln.reshape(1, HEAD_LANES))


def _attn_sample_kernel(pt_ref, lam_ref, q_ref, kn_ref, vn_ref, *rest, n_heads, t_new, out_scale,
                        pages_per_step):
    del pt_ref
    kc_refs = rest[:pages_per_step]
    vc_refs = rest[pages_per_step:2 * pages_per_step]
    g_ref, o_ref, q_scr, bias_scr, s_scr, m_scr, l_scr, acc_scr = rest[2 * pages_per_step:]
    p = pl.program_id(1)
    n_steps = pl.num_programs(1)
    rows = 2 * n_heads * t_new
    page_rows = kc_refs[0].shape[0]
    nt = (((1,), (1,)), ((), ()))
    log_t, head_mask = _log2(t_new), n_heads - 1

    @pl.when(p == 0)
    def _():
        q = q_ref[0]
        lane = lax.broadcasted_iota(jnp.int32, (t_new, HEAD_LANES), 1)
        pieces = []
        for half in range(2):
            for h in range(n_heads):
                qh = q[:, h * HEAD_LANES:(h + 1) * HEAD_LANES]
                pieces.append(jnp.where((lane >= HEAD_DIM) == bool(half), qh, 0.0))
        q_scr[...] = jnp.concatenate(pieces, axis=0).astype(BF16)
        r = lax.broadcasted_iota(jnp.int32, (rows, page_rows), 0)
        c = lax.broadcasted_iota(jnp.int32, (rows, page_rows), 1)
        bias_scr[...] = jnp.where(((r >> log_t) & head_mask) == (c & head_mask), 0.0, NEG_INF)
        m_scr[...] = jnp.full(m_scr.shape, NEG_INF, F32)
        l_scr[...] = jnp.zeros(l_scr.shape, F32)
        acc_scr[...] = jnp.zeros(acc_scr.shape, F32)

    def update(keys, values, bias):
        q = q_scr[...]
        m_old = m_scr[...]
        m_new = m_old
        for i, k in enumerate(keys):
            s = lax.dot_general(q, k(), nt, preferred_element_type=F32) + bias
            s_scr[i, :, 0:s.shape[1]] = s
            m_new = jnp.maximum(m_new, jnp.max(s, axis=-1, keepdims=True))
        alpha = jnp.exp(m_old - m_new)
        l = alpha * l_scr[...]
        acc = alpha * acc_scr[...]
        for i, v in enumerate(values):
            pe = jnp.exp(s_scr[i, :, 0:bias.shape[1]] - m_new)
            l = l + jnp.sum(pe, axis=-1, keepdims=True)
            acc = acc + jnp.dot(pe.astype(BF16), v(), preferred_element_type=F32)
        l_scr[...] = l
        acc_scr[...] = acc
        m_scr[...] = m_new

    update([lambda r=r: r[...].astype(BF16) for r in kc_refs],
           [lambda r=r: r[...].astype(BF16) for r in vc_refs], bias_scr[...])

    @pl.when(p == n_steps - 1)
    def _():
        new_rows = t_new * n_heads
        pad = jnp.zeros((LANES - new_rows, HEAD_LANES), F32)
        k_new = jnp.concatenate([kn_ref[0], pad], axis=0).astype(BF16)
        v_new = jnp.concatenate([vn_ref[0], pad], axis=0).astype(BF16)
        r = lax.broadcasted_iota(jnp.int32, (rows, LANES), 0)
        c = lax.broadcasted_iota(jnp.int32, (rows, LANES), 1)
        ok = jnp.logical_and(((r >> log_t) & head_mask) == (c & head_mask),
                             (c >> _log2(n_heads)) <= (r & (t_new - 1)))
        update([lambda: k_new], [lambda: v_new], jnp.where(ok, 0.0, NEG_INF))
        lam = lam_ref[0]
        inv_l = 1.0 / l_scr[...]
        outs = []
        for h in range(n_heads):
            r1 = slice(h * t_new, (h + 1) * t_new)
            r2 = slice((n_heads + h) * t_new, (n_heads + h + 1) * t_new)
            o = acc_scr[r1, :] * inv_l[r1] - lam * (acc_scr[r2, :] * inv_l[r2])
            o = o * lax.rsqrt(jnp.mean(o * o, axis=-1, keepdims=True) + EPS) * g_ref[...] * out_scale
            outs.append(o)
        o_ref[0] = jnp.concatenate(outs, axis=1).astype(o_ref.dtype)


def attn_sample(lam, page_table, q_s, k_s, v_s, cache_k, cache_v, layer, subln, out_scale, n_heads):
    db, t_new, width = q_s.shape
    page_rows = cache_k.shape[2]
    new_rows = t_new * n_heads
    n_pages = page_table.shape[1]
    rows = 2 * n_heads * t_new
    assert new_rows <= LANES
    pps = _tile(n_pages, 8)

    def page_spec(g):
        return pl.BlockSpec((None, None, page_rows, HEAD_LANES),
                            lambda b, p, pt: (layer, pt[b, p * pps + g], 0, 0))

    grid_spec = pltpu.PrefetchScalarGridSpec(
        num_scalar_prefetch=1,
        grid=(db, n_pages // pps),
        in_specs=([pl.BlockSpec(memory_space=pltpu.SMEM),
                   pl.BlockSpec((1, t_new, width), lambda b, p, pt: (b, 0, 0)),
                   pl.BlockSpec((1, new_rows, HEAD_LANES), lambda b, p, pt: (b, 0, 0)),
                   pl.BlockSpec((1, new_rows, HEAD_LANES), lambda b, p, pt: (b, 0, 0))]
                  + [page_spec(g) for g in range(pps)] + [page_spec(g) for g in range(pps)]
                  + [pl.BlockSpec((1, HEAD_LANES), lambda b, p, pt: (0, 0))]),
        out_specs=pl.BlockSpec((1, t_new, width), lambda b, p, pt: (b, 0, 0)),
        scratch_shapes=[pltpu.VMEM((rows, HEAD_LANES), BF16),
                        pltpu.VMEM((rows, page_rows), F32),
                        pltpu.VMEM((pps, rows, page_rows), F32),
                        pltpu.VMEM((rows, 1), F32),
                        pltpu.VMEM((rows, 1), F32),
                        pltpu.VMEM((rows, HEAD_LANES), F32)],
    )
    return pl.pallas_call(
        functools.partial(_attn_sample_kernel, n_heads=n_heads, t_new=t_new, out_scale=out_scale,
                          pages_per_step=pps),
        grid_spec=grid_spec,
        out_shape=jax.ShapeDtypeStruct((db, t_new, width), F32),
        compiler_params=_params(2),
        name="attn_sample",
    )(page_table, lam.reshape(1), q_s, k_s, v_s, *([cache_k] * pps), *([cache_v] * pps),
      subln.reshape(1, HEAD_LANES))


def _ssm_kernel(u_ref, h0re_ref, h0im_ref, are_ref, aim_ref, b_ref, c_ref, d_ref,
                y_ref, hre_ref, him_ref, bu_scr, *, nb, tc):
    c_idx = pl.program_id(1)
    sw = are_ref.shape[1]

    @pl.when(c_idx == 0)
    def _():
        hre_ref[...] = h0re_ref[...]
        him_ref[...] = h0im_ref[...]

    u = u_ref[...]
    bu_scr[...] = jnp.dot(u.astype(BF16), b_ref[...], preferred_element_type=F32)
    a_re = are_ref[...]
    a_im = aim_ref[...]
    for r in range(nb // SUBLANES):
        rs = slice(r * SUBLANES, (r + 1) * SUBLANES)

        def body(t, carry, r=r):
            h_re, h_im = carry
            row = pl.multiple_of(t * nb + r * SUBLANES, SUBLANES)
            n_re = a_re * h_re - a_im * h_im + bu_scr[pl.ds(row, SUBLANES), 0:sw]
            n_im = a_re * h_im + a_im * h_re + bu_scr[pl.ds(row, SUBLANES), sw:2 * sw]
            bu_scr[pl.ds(row, SUBLANES), 0:sw] = n_re
            bu_scr[pl.ds(row, SUBLANES), sw:2 * sw] = n_im
            return n_re, n_im

        h_re, h_im = lax.fori_loop(0, tc, body, (hre_ref[rs, :], him_ref[rs, :]))
        hre_ref[rs, :] = h_re
        him_ref[rs, :] = h_im
    y = jnp.dot(bu_scr[...].astype(BF16), c_ref[...], preferred_element_type=F32) + d_ref[...] * u
    y_ref[...] = jax.nn.gelu(y)


def ssm_scan(u_tm, h0_re, h0_im, a_re, a_im, b_blk, c_blk, d, nb, tc):
    rows, ch = u_tm.shape
    n_gb = ch // SSM_BLOCK_CH
    sw = h0_re.shape[1] // n_gb
    n_chunks = rows // (tc * nb)
    blk_rows = tc * nb
    return pl.pallas_call(
        functools.partial(_ssm_kernel, nb=nb, tc=tc),
        grid=(n_gb, n_chunks),
        in_specs=[pl.BlockSpec((blk_rows, SSM_BLOCK_CH), lambda g, c: (c, g)),
                  pl.BlockSpec((nb, sw), lambda g, c: (0, g)),
                  pl.BlockSpec((nb, sw), lambda g, c: (0, g)),
                  pl.BlockSpec((SUBLANES, sw), lambda g, c: (0, g)),
                  pl.BlockSpec((SUBLANES, sw), lambda g, c: (0, g)),
                  pl.BlockSpec((None, SSM_BLOCK_CH, 2 * sw), lambda g, c: (g, 0, 0)),
                  pl.BlockSpec((None, 2 * sw, SSM_BLOCK_CH), lambda g, c: (g, 0, 0)),
                  pl.BlockSpec((1, SSM_BLOCK_CH), lambda g, c: (0, g))],
        out_specs=[pl.BlockSpec((blk_rows, SSM_BLOCK_CH), lambda g, c: (c, g)),
                   pl.BlockSpec((nb, sw), lambda g, c: (0, g)),
                   pl.BlockSpec((nb, sw), lambda g, c: (0, g))],
        out_shape=[jax.ShapeDtypeStruct((rows, ch), F32),
                   jax.ShapeDtypeStruct(h0_re.shape, F32),
                   jax.ShapeDtypeStruct(h0_im.shape, F32)],
        scratch_shapes=[pltpu.VMEM((blk_rows, 2 * sw), F32)],
        compiler_params=_params(2),
        name="ssm_scan",
    )(u_tm, h0_re, h0_im, a_re, a_im, b_blk, c_blk, d)


def _ssm_weights(a_re, a_im, b_re, b_im, c_re, c_im, log_dt):
    n_groups, state = a_re.shape
    gpb = SSM_BLOCK_CH // GROUP_CH
    n_gb = n_groups // gpb
    dt = jnp.exp(log_dt)[:, None]
    mag = jnp.exp(dt * a_re)
    ab_re = mag * jnp.cos(dt * a_im)
    ab_im = mag * jnp.sin(dt * a_im)
    den = a_re * a_re + a_im * a_im
    nr = ab_re - 1.0
    coef_re = (nr * a_re + ab_im * a_im) / den
    coef_im = (ab_im * a_re - nr * a_im) / den
    bb_re = coef_re[..., None] * b_re - coef_im[..., None] * b_im
    bb_im = coef_re[..., None] * b_im + coef_im[..., None] * b_re
    eye = jnp.eye(gpb, dtype=F32)

    def b_block(bb):
        bb = bb.reshape(n_gb, gpb, state, GROUP_CH)
        return jnp.einsum("bgpc,gh->bgchp", bb, eye).reshape(n_gb, gpb * GROUP_CH, gpb * state)

    def c_block(cc):
        cc = cc.reshape(n_gb, gpb, GROUP_CH, state)
        return jnp.einsum("bgcp,gh->bgphc", cc, eye).reshape(n_gb, gpb * state, gpb * GROUP_CH)

    b_blk = jnp.concatenate([b_block(bb_re), b_block(bb_im)], axis=2).astype(BF16)
    c_blk = jnp.concatenate([c_block(c_re), c_block(-c_im)], axis=1).astype(BF16)
    bc = lambda a: jnp.broadcast_to(a.reshape(1, n_groups * state), (SUBLANES, n_groups * state))
    return bc(ab_re), bc(ab_im), b_blk, c_blk


def _lambda_init(layer):
    return 0.8 - 0.6 * math.exp(-0.3 * layer)


def kernel(x_prompt, x_sample, cache_k, cache_v, state_ssm_re, state_ssm_im, page_table, norm_mix, w_in, q_norm, k_norm, lambda_q1, lambda_k1, lambda_q2, lambda_k2, subln, ssm_a_re, ssm_a_im, ssm_b_re, ssm_b_im, ssm_c_re, ssm_c_im, ssm_d, ssm_log_dt, w_glu, w_proj_attn, w_proj_ssm, w_out, norm_ffn, ffn_w_gate, ffn_w_up, ffn_w_down, router, moe_w_gate, moe_w_up, moe_w_down):
    n_b, seq, d = x_prompt.shape
    db, t_new, _ = x_sample.shape
    depth = w_in.shape[0]
    n_heads = cache_k.shape[3]
    qk_w = n_heads * HEAD_LANES
    ssm_w = w_glu.shape[1]
    n_groups, state = ssm_a_re.shape[1:]
    n_state = n_groups * state
    page = cache_k.shape[2]
    tp = n_b * seq
    ts = db * t_new
    t_all = tp + ts
    assert n_b == SUBLANES and db % SUBLANES == 0 and ssm_w % SSM_BLOCK_CH == 0
    assert w_in.shape[2] == 3 * qk_w + ssm_w + 2 * d and ssm_w == qk_w

    tm = _tile(math.gcd(seq, ts), 1024)
    n_i = t_all // tm
    cache_k = cache_k.reshape(cache_k.shape[0], cache_k.shape[1], page * n_heads, HEAD_LANES)
    cache_v = cache_v.reshape(cache_v.shape[0], cache_v.shape[1], page * n_heads, HEAD_LANES)
    x_parts = [x_prompt.reshape(tp, d), x_sample.reshape(ts, d)]
    zeros_h0 = jnp.zeros((n_b, n_state), F32)
    scale = HEAD_DIM ** -0.5

    n_ip = tp // tm
    prow = lambda i: jnp.minimum(i, n_ip - 1)
    srow = lambda i: jnp.maximum(i - n_ip, 0)
    which_part = lambda pids: jnp.where(pids[0] < n_ip, 0, 1)

    def row_lhs(a):
        return (a, (tm, a.shape[1]), lambda i, j: (i, 0))

    def split_lhs(a_p, a_s):
        return [(a_p, (tm, a_p.shape[1]), lambda i, j: (prow(i), 0)),
                (a_s, (tm, a_s.shape[1]), lambda i, j: (srow(i), 0))]

    k_out, v_out, hrp_out, hip_out, hrs_out, his_out = [], [], [], [], [], []
    for layer in range(depth):
        lam_0 = _lambda_init(layer)
        lam = (jnp.exp(jnp.sum(lambda_q1[layer] * lambda_k1[layer]))
               - jnp.exp(jnp.sum(lambda_q2[layer] * lambda_k2[layer])) + lam_0).astype(F32)
        xn = rmsnorm_cast(x_parts if layer == 0 else [x], norm_mix[layer], _tile(ts, 512))

        tn = _tile(qk_w, 512)
        hpt = tn // HEAD_LANES
        nq_t, ns_t, ng_t = qk_w // tn, ssm_w // tn, 2 * d // tn
        gain_row = lambda g: (jnp.tile(g, tn // HEAD_DIM).reshape(1, tn), (1, tn), lambda i, j: (0, 0))

        def ranged_out(width, dtype, lo, n):
            return (jax.ShapeDtypeStruct((t_all, width), dtype), (tm, tn),
                    lambda i, j: (i, jnp.clip(j - lo, 0, n - 1)))

        head_rows_out = (jax.ShapeDtypeStruct((t_all * n_heads, HEAD_LANES), F32),
                         (tm * n_heads, HEAD_LANES), lambda i, j: (i, 0))

        def stores(j, lo, n, value_fn, tile_refs, head_rows_ref=None):
            @pl.when(jnp.logical_and(j >= lo, j < lo + n))
            def _():
                value = value_fn()
                for r in tile_refs:
                    r[...] = value.astype(r.dtype)
                if head_rows_ref is not None:
                    for hl in range(hpt):
                        rows = pl.ds((j - lo) * hpt + hl, tm, stride=n_heads)
                        head_rows_ref[rows, :] = value[:, hl * HEAD_LANES:(hl + 1) * HEAD_LANES]

        def ep_qkv(accs, ex, out, pids):
            j = pids[1]
            y = accs[0]

            def head_normed(gain_ref, mult):
                return y * lax.rsqrt(_segment_mean64(y * y) + EPS) * gain_ref[...] * mult

            qn_r, k32_r, knb_r, v32_r, vb_r = out
            stores(j, 0, nq_t, lambda: head_normed(ex[0], scale), [qn_r])
            stores(j, nq_t, nq_t, lambda: head_normed(ex[1], 1.0), [knb_r], k32_r)
            stores(j, 2 * nq_t, nq_t, lambda: y, [vb_r], v32_r)

        qn, k_rows, kn_b, v_rows, v_b = fused_matmul(
            "proj_qkv", (n_i, 3 * nq_t), [row_lhs(xn)],
            [(0, w_in, (None, d, tn), lambda i, j: (layer, 0, j))],
            [gain_row(q_norm[layer]), gain_row(k_norm[layer])],
            [ranged_out(qk_w, BF16, 0, nq_t),
             head_rows_out, ranged_out(qk_w, BF16, nq_t, nq_t),
             head_rows_out, ranged_out(qk_w, BF16, 2 * nq_t, nq_t)], ep_qkv)

        def ep_ug(accs, ex, out, pids):
            stores(pids[1], 0, ns_t, lambda: accs[0], [out[0]])
            stores(pids[1], ns_t, ng_t, lambda: jax.nn.sigmoid(accs[0]), [out[1]])

        u_all, gates = fused_matmul(
            "proj_ug", (n_i, ns_t + ng_t), [row_lhs(xn)],
            [(0, w_in, (None, d, tn), lambda i, j: (layer, 0, j + 3 * nq_t))], [],
            [ranged_out(ssm_w, F32, 0, ns_t), ranged_out(2 * d, F32, ns_t, ng_t)], ep_ug)

        out_scale = 1.0 - lam_0
        o_p = attn_prompt(lam, qn, kn_b, v_b, subln[layer], out_scale, n_b, seq, n_heads)
        k_s = k_rows[tp * n_heads:].reshape(db, t_new * n_heads, HEAD_LANES)
        v_s = v_rows[tp * n_heads:].reshape(db, t_new * n_heads, HEAD_LANES)
        q_s = qn[tp:].astype(F32).reshape(db, t_new, qk_w)
        o_s = attn_sample(lam, page_table, q_s, k_s, v_s, cache_k, cache_v, layer, subln[layer],
                          out_scale, n_heads).reshape(ts, qk_w)

        a_re, a_im, b_blk, c_blk = _ssm_weights(
            ssm_a_re[layer], ssm_a_im[layer], ssm_b_re[layer], ssm_b_im[layer],
            ssm_c_re[layer], ssm_c_im[layer], ssm_log_dt[layer])
        d_row = ssm_d[layer].reshape(1, ssm_w)
        u_p = u_all[:tp].reshape(n_b, seq, ssm_w).transpose(1, 0, 2).reshape(tp, ssm_w)
        u_s = u_all[tp:].reshape(db, t_new, ssm_w).transpose(1, 0, 2).reshape(ts, ssm_w)
        yg_p, hrp, hip = ssm_scan(u_p, zeros_h0, zeros_h0, a_re, a_im, b_blk, c_blk, d_row,
                                  n_b, _tile(seq, 256))
        yg_s, hrs, his = ssm_scan(u_s, state_ssm_re[layer].reshape(db, n_state),
                                  state_ssm_im[layer].reshape(db, n_state),
                                  a_re, a_im, b_blk, c_blk, d_row, db, t_new)
        yg_p = yg_p.reshape(seq, n_b, ssm_w).transpose(1, 0, 2).reshape(tp, ssm_w)
        yg_s = yg_s.reshape(t_new, db, ssm_w).transpose(1, 0, 2).reshape(ts, ssm_w)

        def ep_glu(accs, ex, out, pids):
            yg_tile = jnp.where(pids[0] < n_ip, ex[0][...], ex[1][...])
            out[0][...] = (yg_tile * jax.nn.sigmoid(accs[0])).astype(out[0].dtype)

        tn = _tile(ssm_w, 512)
        y_ssm, = fused_matmul(
            "ssm_glu", (n_i, ssm_w // tn), [split_lhs(yg_p, yg_s)],
            [(0, w_glu, (None, ssm_w, tn), lambda i, j: (layer, 0, j))],
            [(yg_p, (tm, tn), lambda i, j: (prow(i), j)),
             (yg_s, (tm, tn), lambda i, j: (srow(i), j))],
            [(jax.ShapeDtypeStruct((t_all, ssm_w), BF16), (tm, tn), lambda i, j: (i, j))], ep_glu,
            which=which_part)

        tn = _tile(d, 512)

        def ep_merge(accs, ex, out, pids):
            out[0][...] = (ex[0][...] * accs[0] + ex[1][...] * accs[1]).astype(out[0].dtype)

        merged, = fused_matmul(
            "merge", (n_i, d // tn), [split_lhs(o_p, o_s), row_lhs(y_ssm)],
            [(0, w_proj_attn, (None, qk_w, tn), lambda i, j: (layer, 0, j)),
             (1, w_proj_ssm, (None, ssm_w, tn), lambda i, j: (layer, 0, j))],
            [(gates, (tm, tn), lambda i, j: (i, j)),
             (gates, (tm, tn), lambda i, j: (i, j + d // tn))],
            [(jax.ShapeDtypeStruct((t_all, d), BF16), (tm, tn), lambda i, j: (i, j))], ep_merge,
            which=which_part)

        def ep_residual(accs, ex, out, pids):
            res = ex[0][...] if len(ex) == 1 else jnp.where(pids[0] < n_ip, ex[0][...], ex[1][...])
            out[0][...] = res + accs[0]

        if layer == 0:
            x_tiles = [(x_parts[0], (tm, tn), lambda i, j: (prow(i), j)),
                       (x_parts[1], (tm, tn), lambda i, j: (srow(i), j))]
        else:
            x_tiles = [(x, (tm, tn), lambda i, j: (i, j))]
        x, = fused_matmul(
            "out_proj", (n_i, d // tn), [row_lhs(merged)],
            [(0, w_out, (None, d, tn), lambda i, j: (layer, 0, j))], x_tiles,
            [(jax.ShapeDtypeStruct((t_all, d), F32), (tm, tn), lambda i, j: (i, j))], ep_residual)

        def ep_swiglu(accs, ex, out, pids):
            out[0][...] = (jax.nn.silu(accs[0]) * accs[1]).astype(out[0].dtype)

        jl = layer // 2
        if layer % 2 == 0:
            d_ff = ffn_w_gate.shape[2]
            tn = _tile(d_ff, 512)

            def norm_ffn_rows(block, ex):
                ms = jnp.mean(block * block, axis=-1, keepdims=True)
                return block * lax.rsqrt(ms + EPS) * ex[0][...]

            h, = fused_matmul(
                "ffn_gate_up", (n_i, d_ff // tn), [row_lhs(x)],
                [(0, ffn_w_gate, (None, d, tn), lambda i, j: (jl, 0, j)),
                 (0, ffn_w_up, (None, d, tn), lambda i, j: (jl, 0, j))],
                [(norm_ffn[layer].reshape(1, d), (1, d), lambda i, j: (0, 0))],
                [(jax.ShapeDtypeStruct((t_all, d_ff), BF16), (tm, tn), lambda i, j: (i, j))],
                ep_swiglu, prep={0: norm_ffn_rows})
            tn = _tile(d, 256)
            x, = fused_matmul(
                "ffn_down", (n_i, d // tn), [row_lhs(h)],
                [(0, ffn_w_down, (None, d_ff, tn), lambda i, j: (jl, 0, j))],
                [(x, (tm, tn), lambda i, j: (i, j))],
                [(jax.ShapeDtypeStruct((t_all, d), F32), (tm, tn), lambda i, j: (i, j))],
                ep_residual)
        else:
            xn2, sel = rmsnorm_router(x, norm_ffn[layer], router[jl], _tile(t_all, 512))
            last = layer == depth - 1
            x_out = moe_sparse(x, xn2, sel, moe_w_gate, moe_w_up, moe_w_down, jl,
                               split_rows=tp if last else None)
            x = x_out[0]

        k_out.append(k_rows)
        v_out.append(v_rows)
        hrp_out.append(hrp)
        hip_out.append(hip)
        hrs_out.append(hrs)
        his_out.append(his)

    def split(parts, lo, hi, shape):
        return jnp.stack([p[lo * n_heads:hi * n_heads].reshape(shape) for p in parts], axis=0)

    kv_p = (n_b, seq, n_heads, HEAD_LANES)
    kv_s = (db, t_new, n_heads, HEAD_LANES)
    st = lambda parts, n: jnp.stack([p.reshape(n, n_groups, state) for p in parts], axis=0)
    if depth % 2 == 0:
        x_p, x_s = x_out
    else:
        x_p, x_s = x[:tp], x[tp:]
    return (x_p.reshape(n_b, seq, d), x_s.reshape(db, t_new, d),
            split(k_out, 0, tp, kv_p), split(v_out, 0, tp, kv_p),
            st(hrp_out, n_b), st(hip_out, n_b),
            split(k_out, tp, t_all, kv_s), split(v_out, tp, t_all, kv_s),
            st(hrs_out, db), st(his_out, db))
```

```python
import functools
import math

import jax
import jax.numpy as jnp
from jax import lax
from jax.experimental import pallas as pl
from jax.experimental.pallas import tpu as pltpu

F32 = jnp.float32
BF16 = jnp.bfloat16

EPS = 1e-6
NEG_INF = -1e30
HEAD_DIM = 64
HEAD_LANES = 2 * HEAD_DIM
LANES = 128
MXU_DIM = 256
SUBLANES = 8
GROUP_CH = 16
SSM_BLOCK_CH = 256
TOP_K = 2
VMEM_LIMIT_BYTES = 56 * 1024 * 1024


def _tile(dim, pref):
    t = min(pref, dim)
    while dim % t:
        t //= 2
    return t


def _log2(n):
    assert n & (n - 1) == 0
    return n.bit_length() - 1


def _params(n_axes):
    return pltpu.CompilerParams(dimension_semantics=("arbitrary",) * n_axes,
                                vmem_limit_bytes=VMEM_LIMIT_BYTES)


def _rmsnorm_kernel(*refs, n_first):
    x_refs, g_ref, o_ref = refs[:-2], refs[-2], refs[-1]

    def emit(x_ref):
        x = x_ref[...]
        ms = jnp.mean(x * x, axis=-1, keepdims=True)
        o_ref[...] = (x * lax.rsqrt(ms + EPS) * g_ref[...]).astype(o_ref.dtype)

    if len(x_refs) == 1:
        emit(x_refs[0])
    else:
        i = pl.program_id(0)
        pl.when(i < n_first)(lambda: emit(x_refs[0]))
        pl.when(i >= n_first)(lambda: emit(x_refs[1]))


def rmsnorm_cast(parts, g, tm):
    d = parts[0].shape[1]
    t = sum(p.shape[0] for p in parts)
    n_first = parts[0].shape[0] // tm
    if len(parts) == 1:
        x_specs = [pl.BlockSpec((tm, d), lambda i: (i, 0))]
    else:
        assert len(parts) == 2 and all(p.shape[0] % tm == 0 for p in parts)
        x_specs = [pl.BlockSpec((tm, d), lambda i: (jnp.minimum(i, n_first - 1), 0)),
                   pl.BlockSpec((tm, d), lambda i: (jnp.maximum(i - n_first, 0), 0))]
    return pl.pallas_call(
        functools.partial(_rmsnorm_kernel, n_first=n_first),
        grid=(t // tm,),
        in_specs=x_specs + [pl.BlockSpec((1, d), lambda i: (0, 0))],
        out_specs=pl.BlockSpec((tm, d), lambda i: (i, 0)),
        out_shape=jax.ShapeDtypeStruct((t, d), BF16),
        compiler_params=_params(1),
        name="rmsnorm_cast",
    )(*parts, g.reshape(1, d))


def _split3(a):
    hi = a.astype(BF16)
    r1 = a - hi.astype(F32)
    mid = r1.astype(BF16)
    lo = (r1 - mid.astype(F32)).astype(BF16)
    return hi, mid, lo


def _rmsnorm_router_kernel(x_ref, g_ref, rw_ref, o_ref, gates_ref, *, n_experts):
    x = x_ref[...]
    ms = jnp.mean(x * x, axis=-1, keepdims=True)
    xn = x * lax.rsqrt(ms + EPS) * g_ref[...]
    o_ref[...] = xn.astype(o_ref.dtype)
    xh, xm, xl = _split3(xn)
    wh, wm, wl = _split3(rw_ref[...])
    dot = functools.partial(jnp.dot, preferred_element_type=F32)
    logits = (dot(xh, wh) + (dot(xh, wm) + dot(xm, wh))
              + (dot(xh, wl) + dot(xm, wm) + dot(xl, wh)))
    lane = lax.broadcasted_iota(jnp.int32, logits.shape, 1).astype(F32)
    logits = jnp.where(lane < n_experts, logits, -jnp.inf)
    m1 = jnp.max(logits, axis=-1, keepdims=True)
    i1 = jnp.min(jnp.where(logits == m1, lane, float(LANES)), axis=-1, keepdims=True)
    rest = jnp.where(lane == i1, -jnp.inf, logits)
    m2 = jnp.max(rest, axis=-1, keepdims=True)
    i2 = jnp.min(jnp.where(rest == m2, lane, float(LANES)), axis=-1, keepdims=True)
    e2 = jnp.exp(m2 - m1)
    g1 = 1.0 / (1.0 + e2)
    g2 = e2 / (1.0 + e2)
    gates_ref[...] = jnp.where(lane == 0.0, i1, jnp.where(lane == 1.0, i2, jnp.where(
        lane == 2.0, g1, jnp.where(lane == 3.0, g2, 0.0))))


def rmsnorm_router(x, g, router_w, tm):
    t, d = x.shape
    n_experts = router_w.shape[1]
    rw = jnp.zeros((d, LANES), F32).at[:, :n_experts].set(router_w)
    return pl.pallas_call(
        functools.partial(_rmsnorm_router_kernel, n_experts=n_experts),
        grid=(t // tm,),
        in_specs=[pl.BlockSpec((tm, d), lambda i: (i, 0)),
                  pl.BlockSpec((1, d), lambda i: (0, 0)),
                  pl.BlockSpec((d, LANES), lambda i: (0, 0))],
        out_specs=[pl.BlockSpec((tm, d), lambda i: (i, 0)),
                   pl.BlockSpec((tm, LANES), lambda i: (i, 0))],
        out_shape=[jax.ShapeDtypeStruct((t, d), F32),
                   jax.ShapeDtypeStruct((t, LANES), F32)],
        compiler_params=_params(1),
        name="rmsnorm_router",
    )(x, g.reshape(1, d), rw)


def _issue_row_gather(idx_ref, n_rows, src_hbm, dst_buf, slot, sem):
    def body(r, carry):
        row = idx_ref[0, r]
        pltpu.make_async_copy(src_hbm.at[pl.ds(row, 1)], dst_buf.at[slot, pl.ds(r, 1)],
                              sem.at[slot]).start()
        return carry
    lax.fori_loop(0, n_rows, body, 0, unroll=8)


def _wait_row_gather(n_rows, src_hbm, dst_buf, slot, sem):
    pltpu.make_async_copy(src_hbm.at[pl.ds(0, n_rows)], dst_buf.at[slot], sem.at[slot]).wait()


def _gathered_rows(step, n_steps, first, idx_ref, idx_next_ref, src_hbm, buf, sem, n_rows):
    slot = step % 2

    @pl.when(first)
    def _():
        @pl.when(step == 0)
        def _():
            _issue_row_gather(idx_ref, n_rows, src_hbm, buf, 0, sem)

        _wait_row_gather(n_rows, src_hbm, buf, slot, sem)

        @pl.when(step + 1 < n_steps)
        def _():
            _issue_row_gather(idx_next_ref, n_rows, src_hbm, buf, 1 - slot, sem)

    return slot


def _moe_gather_kernel(idx_ref, idx_next_ref, xn_hbm, o_ref, buf, sem, *, tmb):
    bi = pl.program_id(0)
    slot = _gathered_rows(bi, pl.num_programs(0), bi >= 0, idx_ref, idx_next_ref, xn_hbm, buf, sem, tmb)
    o_ref[...] = buf[slot].astype(o_ref.dtype)


def _moe_gate_up_kernel(blk_e_ref, blk_rows_ref, x_ref, wg_ref, wu_ref, h_ref, *, tmb):
    del blk_e_ref
    bi = pl.program_id(0)
    wg = wg_ref[...].astype(BF16)
    wu = wu_ref[...].astype(BF16)

    def rows(rs):
        a = x_ref[rs, :]
        gate = jnp.dot(a, wg, preferred_element_type=F32)
        up = jnp.dot(a, wu, preferred_element_type=F32)
        h_ref[rs, :] = (jax.nn.silu(gate) * up).astype(h_ref.dtype)

    _by_valid_rows(blk_rows_ref[bi], tmb, rows, h_ref)


def _by_valid_rows(n_valid, tmb, compute, out_ref):
    half = tmb // 2
    zeros = lambda n: jnp.zeros((n, out_ref.shape[1]), out_ref.dtype)

    @pl.when(n_valid > half)
    def _():
        compute(slice(0, tmb))

    @pl.when(jnp.logical_and(n_valid > 0, n_valid <= half))
    def _():
        compute(slice(0, half))
        out_ref[half:tmb, :] = zeros(tmb - half)

    @pl.when(n_valid <= 0)
    def _():
        out_ref[...] = zeros(tmb)


def _moe_down_kernel(blk_e_ref, blk_rows_ref, h_ref, w_ref, y_ref, *, tmb):
    del blk_e_ref
    bi = pl.program_id(0)
    w = w_ref[...].astype(BF16)

    def rows(rs):
        y_ref[rs, :] = jnp.dot(h_ref[rs, :], w, preferred_element_type=F32)

    _by_valid_rows(blk_rows_ref[bi], tmb, rows, y_ref)


def _moe_combine_kernel(idx_ref, idx_next_ref, x_ref, sel_ref, ys_hbm, *rest, tc, n_first):
    o_refs, (buf, sem) = rest[:-2], rest[-2:]
    i = pl.program_id(0)
    slot = _gathered_rows(i, pl.num_programs(0), i >= 0, idx_ref, idx_next_ref, ys_hbm, buf, sem, 2 * tc)
    g1 = sel_ref[:, TOP_K:TOP_K + 1]
    g2 = sel_ref[:, TOP_K + 1:TOP_K + 2]
    y = x_ref[...] + (g1 * buf[slot, 0:tc, :] + g2 * buf[slot, tc:2 * tc, :])
    if len(o_refs) == 1:
        o_refs[0][...] = y
    else:
        @pl.when(i < n_first)
        def _():
            o_refs[0][...] = y

        @pl.when(i >= n_first)
        def _():
            o_refs[1][...] = y


def moe_sparse(x, xn, sel, w_gate, w_up, w_down, jl, split_rows=None):
    t, d = x.shape
    n_e, _, d_fe = w_gate.shape[1:]
    n_assign = TOP_K * t
    tmb = 1024 if n_assign >= 8 * 1024 else 256
    n_blk = n_assign // tmb + n_e
    n_rows = n_blk * tmb

    e_flat = sel[:, 0:TOP_K].astype(jnp.int32).reshape(n_assign)
    onehot = (e_flat[:, None] == jnp.arange(n_e, dtype=jnp.int32)[None, :]).astype(jnp.int32)
    counts = jnp.sum(onehot, axis=0)
    rank = jnp.sum((jnp.cumsum(onehot, axis=0) - onehot) * onehot, axis=1)
    blks_e = (counts + tmb - 1) // tmb
    blk_end = jnp.cumsum(blks_e)
    blk_start = blk_end - blks_e
    pos = blk_start[e_flat] * tmb + rank
    src_tok = (jnp.arange(n_rows, dtype=jnp.int32) % t).at[pos].set(
        jnp.arange(n_assign, dtype=jnp.int32) // TOP_K)
    blk_ids = jnp.arange(n_blk, dtype=jnp.int32)
    n_active = blk_end[-1]
    blk_e = jnp.sum((blk_ids[:, None] >= blk_end[None, :]).astype(jnp.int32), axis=1)
    blk_e = jnp.minimum(blk_e, blk_e[jnp.maximum(n_active - 1, 0)])
    blk_rows = jnp.clip(counts[blk_e] - (blk_ids - blk_start[blk_e]) * tmb, 0, tmb)
    blk_rows = jnp.where(blk_ids < n_active, blk_rows, 0).astype(jnp.int32)

    idx2d = src_tok.reshape(n_blk, 1, tmb)
    xs = pl.pallas_call(
        functools.partial(_moe_gather_kernel, tmb=tmb),
        grid=(n_blk,),
        in_specs=[pl.BlockSpec((None, 1, tmb), lambda bi: (bi, 0, 0), memory_space=pltpu.SMEM),
                  pl.BlockSpec((None, 1, tmb), lambda bi: (jnp.minimum(bi + 1, n_blk - 1), 0, 0),
                               memory_space=pltpu.SMEM),
                  pl.BlockSpec(memory_space=pl.ANY)],
        out_specs=pl.BlockSpec((tmb, d), lambda bi: (bi, 0)),
        out_shape=jax.ShapeDtypeStruct((n_rows, d), BF16),
        scratch_shapes=[pltpu.VMEM((2, tmb, d), F32),
                        pltpu.SemaphoreType.DMA((2,))],
        compiler_params=_params(1),
        name="moe_gather",
    )(idx2d, idx2d, xn)

    tn = _tile(d_fe, 256)
    nj = d_fe // tn
    col = lambda bi, j, be, br: jnp.where(br[bi] > 0, j, nj - 1)
    h = pl.pallas_call(
        functools.partial(_moe_gate_up_kernel, tmb=tmb),
        grid_spec=pltpu.PrefetchScalarGridSpec(
            num_scalar_prefetch=2,
            grid=(n_blk, nj),
            in_specs=[pl.BlockSpec((tmb, d), lambda bi, j, be, br: (bi, 0)),
                      pl.BlockSpec((None, None, d, tn), lambda bi, j, be, br: (jl, be[bi], 0, col(bi, j, be, br))),
                      pl.BlockSpec((None, None, d, tn), lambda bi, j, be, br: (jl, be[bi], 0, col(bi, j, be, br)))],
            out_specs=pl.BlockSpec((tmb, tn), lambda bi, j, be, br: (bi, j))),
        out_shape=jax.ShapeDtypeStruct((n_rows, d_fe), BF16),
        compiler_params=_params(2),
        name="moe_gate_up",
    )(blk_e, blk_rows, xs, w_gate, w_up)

    tn = _tile(d, 512)
    nj = d // tn
    ys = pl.pallas_call(
        functools.partial(_moe_down_kernel, tmb=tmb),
        grid_spec=pltpu.PrefetchScalarGridSpec(
            num_scalar_prefetch=2,
            grid=(n_blk, nj),
            in_specs=[pl.BlockSpec((tmb, d_fe), lambda bi, j, be, br: (bi, 0)),
                      pl.BlockSpec((None, None, d_fe, tn), lambda bi, j, be, br: (jl, be[bi], 0, col(bi, j, be, br)))],
            out_specs=pl.BlockSpec((tmb, tn), lambda bi, j, be, br: (bi, j))),
        out_shape=jax.ShapeDtypeStruct((n_rows, d), F32),
        compiler_params=_params(2),
        name="moe_down",
    )(blk_e, blk_rows, h, w_down)

    tc = _tile(t, 512)
    n_i = t // tc
    pos_blk = pos.reshape(n_i, tc, TOP_K).transpose(0, 2, 1).reshape(n_i, 1, TOP_K * tc)
    if split_rows is None:
        n_first = n_i
        out_specs = [pl.BlockSpec((tc, d), lambda i: (i, 0))]
        out_shape = [jax.ShapeDtypeStruct((t, d), F32)]
    else:
        n_first = split_rows // tc
        out_specs = [pl.BlockSpec((tc, d), lambda i: (jnp.minimum(i, n_first - 1), 0)),
                     pl.BlockSpec((tc, d), lambda i: (jnp.maximum(i - n_first, 0), 0))]
        out_shape = [jax.ShapeDtypeStruct((split_rows, d), F32),
                     jax.ShapeDtypeStruct((t - split_rows, d), F32)]
    return pl.pallas_call(
        functools.partial(_moe_combine_kernel, tc=tc, n_first=n_first),
        grid=(n_i,),
        in_specs=[pl.BlockSpec((None, 1, TOP_K * tc), lambda i: (i, 0, 0), memory_space=pltpu.SMEM),
                  pl.BlockSpec((None, 1, TOP_K * tc), lambda i: (jnp.minimum(i + 1, n_i - 1), 0, 0),
                               memory_space=pltpu.SMEM),
                  pl.BlockSpec((tc, d), lambda i: (i, 0)),
                  pl.BlockSpec((tc, LANES), lambda i: (i, 0)),
                  pl.BlockSpec(memory_space=pl.ANY)],
        out_specs=out_specs,
        out_shape=out_shape,
        scratch_shapes=[pltpu.VMEM((2, TOP_K * tc, d), F32),
                        pltpu.SemaphoreType.DMA((2,))],
        compiler_params=_params(1),
        name="moe_combine",
    )(pos_blk, pos_blk, x, sel, ys)


def fused_matmul(name, grid, lhs, terms, extras, outs, epilogue, which=None, prep=None):
    lhs = [alts if isinstance(alts, list) else [alts] for alts in lhs]
    flat_lhs = [alt for alts in lhs for alt in alts]
    first_ref = [sum(len(a) for a in lhs[:k]) for k in range(len(lhs))]
    n_lhs, n_terms, n_ex, n_out = len(flat_lhs), len(terms), len(extras), len(outs)
    staged = [k for k, alts in enumerate(lhs) if len(alts) > 1 or alts[0][0].dtype != BF16]

    def kernel(*refs):
        lhs_refs = refs[:n_lhs]
        w_refs = refs[n_lhs:n_lhs + n_terms]
        ex_refs = refs[n_lhs + n_terms:n_lhs + n_terms + n_ex]
        out_refs = refs[n_lhs + n_terms + n_ex:n_lhs + n_terms + n_ex + n_out]
        scr_refs = refs[n_lhs + n_terms + n_ex + n_out:]
        pids = [pl.program_id(a) for a in range(len(grid))]
        if staged:
            first = pids[1] == 0
            for p in pids[2:]:
                first = jnp.logical_and(first, p == 0)
            for s, k in enumerate(staged):
                for a in range(len(lhs[k])):
                    use = first if len(lhs[k]) == 1 else jnp.logical_and(first, which(pids) == a)

                    @pl.when(use)
                    def _(s=s, k=k, r=first_ref[k] + a):
                        block = lhs_refs[r][...]
                        if prep and k in prep:
                            block = prep[k](block.astype(F32), ex_refs)
                        scr_refs[s][...] = block.astype(BF16)

        accs = []
        for (li, _, _, _), w_ref in zip(terms, w_refs):
            a = scr_refs[staged.index(li)][...] if li in staged else lhs_refs[first_ref[li]][...]
            accs.append(jnp.dot(a, w_ref[...].astype(BF16), preferred_element_type=F32))
        epilogue(accs, ex_refs, out_refs, pids)

    in_specs = ([pl.BlockSpec(bs, im) for _, bs, im in flat_lhs]
                + [pl.BlockSpec(bs, im) for _, _, bs, im in terms]
                + [pl.BlockSpec(bs, im) for _, bs, im in extras])
    args = [a for a, _, _ in flat_lhs] + [w for _, w, _, _ in terms] + [a for a, _, _ in extras]
    scratch = [pltpu.VMEM(tuple(b for b in lhs[k][0][1] if b is not None), BF16) for k in staged]
    res = pl.pallas_call(
        kernel,
        grid=grid,
        in_specs=in_specs,
        out_specs=[pl.BlockSpec(bs, im) for _, bs, im in outs],
        out_shape=[sd for sd, _, _ in outs],
        scratch_shapes=scratch,
        compiler_params=_params(len(grid)),
        name=name,
    )(*args)
    return res


def _segment_mean64(sq):
    n = sq.shape[1]
    w = MXU_DIM if n % MXU_DIM == 0 else LANES
    r = lax.broadcasted_iota(jnp.int32, (w, w), 0) >> _log2(HEAD_DIM)
    c = lax.broadcasted_iota(jnp.int32, (w, w), 1) >> _log2(HEAD_DIM)
    ones = (r == c).astype(BF16)
    hi = sq.astype(BF16)
    lo = (sq - hi.astype(F32)).astype(BF16)
    dot = functools.partial(jnp.dot, preferred_element_type=F32)
    cols = []
    for j in range(n // w):
        sl = slice(j * w, (j + 1) * w)
        cols.append(dot(hi[:, sl], ones) + dot(lo[:, sl], ones))
    return jnp.concatenate(cols, axis=1) * (1.0 / HEAD_DIM)


ATTN_HEADS_PER_STEP = 4


def _attn_prompt_kernel(lam_ref, q_ref, k_ref, v_ref, g_ref, o_ref, vt_scr, *, tq, out_scale):
    qi = pl.program_id(2)
    lam = lam_ref[0]
    cols = 2 * tq
    n_h = q_ref.shape[1] // HEAD_LANES
    head = lambda h: slice(h * HEAD_LANES, (h + 1) * HEAD_LANES)

    @pl.when(qi == 0)
    def _():
        vt_scr[...] = v_ref[...].astype(F32).T.astype(BF16)

    lane = lax.broadcasted_iota(jnp.int32, (tq, HEAD_LANES), 1)
    qs = []
    for h in range(n_h):
        q = q_ref[:, head(h)].astype(F32)
        qs.append(jnp.concatenate([jnp.where(lane < HEAD_DIM, q, 0.0),
                                   jnp.where(lane >= HEAD_DIM, q, 0.0)], axis=0).astype(BF16))

    def block(kb, carry, diagonal):
        start = pl.multiple_of(kb * tq, tq)
        out = []
        for h in range(n_h):
            m, l, acc = carry[h]
            k = k_ref[pl.ds(start, tq), head(h)]
            s = lax.dot_general(k, qs[h], (((1,), (1,)), ((), ())), preferred_element_type=F32)
            if diagonal:
                key = lax.broadcasted_iota(jnp.int32, s.shape, 0)
                qry = lax.broadcasted_iota(jnp.int32, s.shape, 1)
                qry = jnp.where(qry >= tq, qry - tq, qry)
                s = jnp.where(key <= qry, s, NEG_INF)
            m_new = jnp.maximum(m, jnp.max(s, axis=0, keepdims=True))
            alpha = jnp.exp(m - m_new)
            p = jnp.exp(s - m_new)
            l = alpha * l + jnp.sum(p, axis=0, keepdims=True)
            pv = jnp.dot(vt_scr[head(h), pl.ds(start, tq)], p.astype(BF16),
                         preferred_element_type=F32)
            out.append((m_new, l, alpha * acc + pv))
        return tuple(out)

    carry = tuple((jnp.full((1, cols), NEG_INF, F32), jnp.zeros((1, cols), F32),
                   jnp.zeros((HEAD_LANES, cols), F32)) for _ in range(n_h))
    carry = lax.fori_loop(0, qi, lambda kb, c: block(kb, c, False), carry)
    carry = block(qi, carry, True)
    for h in range(n_h):
        _, l, acc = carry[h]
        ot = acc / l
        o = (ot[:, :tq] - lam * ot[:, tq:]).T
        o = o * lax.rsqrt(jnp.mean(o * o, axis=-1, keepdims=True) + EPS) * g_ref[...] * out_scale
        o_ref[:, head(h)] = o.astype(o_ref.dtype)


def attn_prompt(lam, qn, kb, vb, subln, out_scale, n_batch, seq, n_heads):
    tq = _tile(seq, 256)
    nq = seq // tq
    hw = ATTN_HEADS_PER_STEP * HEAD_LANES
    assert n_heads % ATTN_HEADS_PER_STEP == 0
    return pl.pallas_call(
        functools.partial(_attn_prompt_kernel, tq=tq, out_scale=out_scale),
        grid=(n_batch, n_heads // ATTN_HEADS_PER_STEP, nq),
        in_specs=[pl.BlockSpec(memory_space=pltpu.SMEM),
                  pl.BlockSpec((tq, hw), lambda b, h, i: (b * nq + i, h)),
                  pl.BlockSpec((seq, hw), lambda b, h, i: (b, h)),
                  pl.BlockSpec((seq, hw), lambda b, h, i: (b, h)),
                  pl.BlockSpec((1, HEAD_LANES), lambda b, h, i: (0, 0))],
        out_specs=pl.BlockSpec((tq, hw), lambda b, h, i: (b * nq + i, h)),
        out_shape=jax.ShapeDtypeStruct((n_batch * seq, n_heads * HEAD_LANES), BF16),
        scratch_shapes=[pltpu.VMEM((hw, seq), BF16)],
        compiler_params=_params(3),
        name="attn_prompt",
    )(lam.reshape(1), qn, kb, vb, subln.reshape(1, HEAD_LANES))


def _attn_sample_kernel(pt_ref, lam_ref, q_ref, kn_ref, vn_ref, *rest, n_heads, t_new, out_scale,
                        pages_per_step):
    del pt_ref
    kc_refs = rest[:pages_per_step]
    vc_refs = rest[pages_per_step:2 * pages_per_step]
    g_ref, o_ref, q_scr, bias_scr, s_scr, m_scr, l_scr, acc_scr = rest[2 * pages_per_step:]
    p = pl.program_id(1)
    n_steps = pl.num_programs(1)
    rows = 2 * n_heads * t_new
    page_rows = kc_refs[0].shape[0]
    nt = (((1,), (1,)), ((), ()))
    log_t, head_mask = _log2(t_new), n_heads - 1

    @pl.when(p == 0)
    def _():
        q = q_ref[0]
        lane = lax.broadcasted_iota(jnp.int32, (t_new, HEAD_LANES), 1)
        pieces = []
        for half in range(2):
            for h in range(n_heads):
                qh = q[:, h * HEAD_LANES:(h + 1) * HEAD_LANES]
                pieces.append(jnp.where((lane >= HEAD_DIM) == bool(half), qh, 0.0))
        q_scr[...] = jnp.concatenate(pieces, axis=0).astype(BF16)
        r = lax.broadcasted_iota(jnp.int32, (rows, page_rows), 0)
        c = lax.broadcasted_iota(jnp.int32, (rows, page_rows), 1)
        bias_scr[...] = jnp.where(((r >> log_t) & head_mask) == (c & head_mask), 0.0, NEG_INF)
        m_scr[...] = jnp.full(m_scr.shape, NEG_INF, F32)
        l_scr[...] = jnp.zeros(l_scr.shape, F32)
        acc_scr[...] = jnp.zeros(acc_scr.shape, F32)

    def update(keys, values, bias):
        q = q_scr[...]
        m_old = m_scr[...]
        m_new = m_old
        for i, k in enumerate(keys):
            s = lax.dot_general(q, k(), nt, preferred_element_type=F32) + bias
            s_scr[i, :, 0:s.shape[1]] = s
            m_new = jnp.maximum(m_new, jnp.max(s, axis=-1, keepdims=True))
        alpha = jnp.exp(m_old - m_new)
        l = alpha * l_scr[...]
        acc = alpha * acc_scr[...]
        for i, v in enumerate(values):
            pe = jnp.exp(s_scr[i, :, 0:bias.shape[1]] - m_new)
            l = l + jnp.sum(pe, axis=-1, keepdims=True)
            acc = acc + jnp.dot(pe.astype(BF16), v(), preferred_element_type=F32)
        l_scr[...] = l
        acc_scr[...] = acc
        m_scr[...] = m_new

    update([lambda r=r: r[...].astype(BF16) for r in kc_refs],
           [lambda r=r: r[...].astype(BF16) for r in vc_refs], bias_scr[...])

    @pl.when(p == n_steps - 1)
    def _():
        new_rows = t_new * n_heads
        pad = jnp.zeros((LANES - new_rows, HEAD_LANES), F32)
        k_new = jnp.concatenate([kn_ref[0], pad], axis=0).astype(BF16)
        v_new = jnp.concatenate([vn_ref[0], pad], axis=0).astype(BF16)
        r = lax.broadcasted_iota(jnp.int32, (rows, LANES), 0)
        c = lax.broadcasted_iota(jnp.int32, (rows, LANES), 1)
        ok = jnp.logical_and(((r >> log_t) & head_mask) == (c & head_mask),
                             (c >> _log2(n_heads)) <= (r & (t_new - 1)))
        update([lambda: k_new], [lambda: v_new], jnp.where(ok, 0.0, NEG_INF))
        lam = lam_ref[0]
        inv_l = 1.0 / l_scr[...]
        outs = []
        for h in range(n_heads):
            r1 = slice(h * t_new, (h + 1) * t_new)
            r2 = slice((n_heads + h) * t_new, (n_heads + h + 1) * t_new)
            o = acc_scr[r1, :] * inv_l[r1] - lam * (acc_scr[r2, :] * inv_l[r2])
            o = o * lax.rsqrt(jnp.mean(o * o, axis=-1, keepdims=True) + EPS) * g_ref[...] * out_scale
            outs.append(o)
        o_ref[0] = jnp.concatenate(outs, axis=1).astype(o_ref.dtype)


def attn_sample(lam, page_table, q_s, k_s, v_s, cache_k, cache_v, layer, subln, out_scale, n_heads):
    db, t_new, width = q_s.shape
    page_rows = cache_k.shape[2]
    new_rows = t_new * n_heads
    n_pages = page_table.shape[1]
    rows = 2 * n_heads * t_new
    assert new_rows <= LANES
    pps = _tile(n_pages, 8)

    def page_spec(g):
        return pl.BlockSpec((None, None, page_rows, HEAD_LANES),
                            lambda b, p, pt: (layer, pt[b, p * pps + g], 0, 0))

    grid_spec = pltpu.PrefetchScalarGridSpec(
        num_scalar_prefetch=1,
        grid=(db, n_pages // pps),
        in_specs=([pl.BlockSpec(memory_space=pltpu.SMEM),
                   pl.BlockSpec((1, t_new, width), lambda b, p, pt: (b, 0, 0)),
                   pl.BlockSpec((1, new_rows, HEAD_LANES), lambda b, p, pt: (b, 0, 0)),
                   pl.BlockSpec((1, new_rows, HEAD_LANES), lambda b, p, pt: (b, 0, 0))]
                  + [page_spec(g) for g in range(pps)] + [page_spec(g) for g in range(pps)]
                  + [pl.BlockSpec((1, HEAD_LANES), lambda b, p, pt: (0, 0))]),
        out_specs=pl.BlockSpec((1, t_new, width), lambda b, p, pt: (b, 0, 0)),
        scratch_shapes=[pltpu.VMEM((rows, HEAD_LANES), BF16),
                        pltpu.VMEM((rows, page_rows), F32),
                        pltpu.VMEM((pps, rows, page_rows), F32),
                        pltpu.VMEM((rows, 1), F32),
                        pltpu.VMEM((rows, 1), F32),
                        pltpu.VMEM((rows, HEAD_LANES), F32)],
    )
    return pl.pallas_call(
        functools.partial(_attn_sample_kernel, n_heads=n_heads, t_new=t_new, out_scale=out_scale,
                          pages_per_step=pps),
        grid_spec=grid_spec,
        out_shape=jax.ShapeDtypeStruct((db, t_new, width), F32),
        compiler_params=_params(2),
        name="attn_sample",
    )(page_table, lam.reshape(1), q_s, k_s, v_s, *([cache_k] * pps), *([cache_v] * pps),
      subln.reshape(1, HEAD_LANES))


def _ssm_kernel(u_ref, h0re_ref, h0im_ref, are_ref, aim_ref, b_ref, c_ref, d_ref,
                y_ref, hre_ref, him_ref, bu_scr, *, nb, tc):
    c_idx = pl.program_id(1)
    sw = are_ref.shape[1]

    @pl.when(c_idx == 0)
    def _():
        hre_ref[...] = h0re_ref[...]
        him_ref[...] = h0im_ref[...]

    u = u_ref[...]
    bu_scr[...] = jnp.dot(u.astype(BF16), b_ref[...], preferred_element_type=F32)
    a_re = are_ref[...]
    a_im = aim_ref[...]
    for r in range(nb // SUBLANES):
        rs = slice(r * SUBLANES, (r + 1) * SUBLANES)

        def body(t, carry, r=r):
            h_re, h_im = carry
            row = pl.multiple_of(t * nb + r * SUBLANES, SUBLANES)
            n_re = a_re * h_re - a_im * h_im + bu_scr[pl.ds(row, SUBLANES), 0:sw]
            n_im = a_re * h_im + a_im * h_re + bu_scr[pl.ds(row, SUBLANES), sw:2 * sw]
            bu_scr[pl.ds(row, SUBLANES), 0:sw] = n_re
            bu_scr[pl.ds(row, SUBLANES), sw:2 * sw] = n_im
            return n_re, n_im

        h_re, h_im = lax.fori_loop(0, tc, body, (hre_ref[rs, :], him_ref[rs, :]))
        hre_ref[rs, :] = h_re
        him_ref[rs, :] = h_im
    y = jnp.dot(bu_scr[...].astype(BF16), c_ref[...], preferred_element_type=F32) + d_ref[...] * u
    y_ref[...] = jax.nn.gelu(y)


def ssm_scan(u_tm, h0_re, h0_im, a_re, a_im, b_blk, c_blk, d, nb, tc):
    rows, ch = u_tm.shape
    n_gb = ch // SSM_BLOCK_CH
    sw = h0_re.shape[1] // n_gb
    n_chunks = rows // (tc * nb)
    blk_rows = tc * nb
    return pl.pallas_call(
        functools.partial(_ssm_kernel, nb=nb, tc=tc),
        grid=(n_gb, n_chunks),
        in_specs=[pl.BlockSpec((blk_rows, SSM_BLOCK_CH), lambda g, c: (c, g)),
                  pl.BlockSpec((nb, sw), lambda g, c: (0, g)),
                  pl.BlockSpec((nb, sw), lambda g, c: (0, g)),
                  pl.BlockSpec((SUBLANES, sw), lambda g, c: (0, g)),
                  pl.BlockSpec((SUBLANES, sw), lambda g, c: (0, g)),
                  pl.BlockSpec((None, SSM_BLOCK_CH, 2 * sw), lambda g, c: (g, 0, 0)),
                  pl.BlockSpec((None, 2 * sw, SSM_BLOCK_CH), lambda g, c: (g, 0, 0)),
                  pl.BlockSpec((1, SSM_BLOCK_CH), lambda g, c: (0, g))],
        out_specs=[pl.BlockSpec((blk_rows, SSM_BLOCK_CH), lambda g, c: (c, g)),
                   pl.BlockSpec((nb, sw), lambda g, c: (0, g)),
                   pl.BlockSpec((nb, sw), lambda g, c: (0, g))],
        out_shape=[jax.ShapeDtypeStruct((rows, ch), F32),
                   jax.ShapeDtypeStruct(h0_re.shape, F32),
                   jax.ShapeDtypeStruct(h0_im.shape, F32)],
        scratch_shapes=[pltpu.VMEM((blk_rows, 2 * sw), F32)],
        compiler_params=_params(2),
        name="ssm_scan",
    )(u_tm, h0_re, h0_im, a_re, a_im, b_blk, c_blk, d)


def _ssm_weights(a_re, a_im, b_re, b_im, c_re, c_im, log_dt):
    n_groups, state = a_re.shape
    gpb = SSM_BLOCK_CH // GROUP_CH
    n_gb = n_groups // gpb
    dt = jnp.exp(log_dt)[:, None]
    mag = jnp.exp(dt * a_re)
    ab_re = mag * jnp.cos(dt * a_im)
    ab_im = mag * jnp.sin(dt * a_im)
    den = a_re * a_re + a_im * a_im
    nr = ab_re - 1.0
    coef_re = (nr * a_re + ab_im * a_im) / den
    coef_im = (ab_im * a_re - nr * a_im) / den
    bb_re = coef_re[..., None] * b_re - coef_im[..., None] * b_im
    bb_im = coef_re[..., None] * b_im + coef_im[..., None] * b_re
    eye = jnp.eye(gpb, dtype=F32)

    def b_block(bb):
        bb = bb.reshape(n_gb, gpb, state, GROUP_CH)
        return jnp.einsum("bgpc,gh->bgchp", bb, eye).reshape(n_gb, gpb * GROUP_CH, gpb * state)

    def c_block(cc):
        cc = cc.reshape(n_gb, gpb, GROUP_CH, state)
        return jnp.einsum("bgcp,gh->bgphc", cc, eye).reshape(n_gb, gpb * state, gpb * GROUP_CH)

    b_blk = jnp.concatenate([b_block(bb_re), b_block(bb_im)], axis=2).astype(BF16)
    c_blk = jnp.concatenate([c_block(c_re), c_block(-c_im)], axis=1).astype(BF16)
    bc = lambda a: jnp.broadcast_to(a.reshape(1, n_groups * state), (SUBLANES, n_groups * state))
    return bc(ab_re), bc(ab_im), b_blk, c_blk


def _lambda_init(layer):
    return 0.8 - 0.6 * math.exp(-0.3 * layer)


def kernel(x_prompt, x_sample, cache_k, cache_v, state_ssm_re, state_ssm_im, page_table, norm_mix, w_in, q_norm, k_norm, lambda_q1, lambda_k1, lambda_q2, lambda_k2, subln, ssm_a_re, ssm_a_im, ssm_b_re, ssm_b_im, ssm_c_re, ssm_c_im, ssm_d, ssm_log_dt, w_glu, w_proj_attn, w_proj_ssm, w_out, norm_ffn, ffn_w_gate, ffn_w_up, ffn_w_down, router, moe_w_gate, moe_w_up, moe_w_down):
    n_b, seq, d = x_prompt.shape
    db, t_new, _ = x_sample.shape
    depth = w_in.shape[0]
    n_heads = cache_k.shape[3]
    qk_w = n_heads * HEAD_LANES
    ssm_w = w_glu.shape[1]
    n_groups, state = ssm_a_re.shape[1:]
    n_state = n_groups * state
    page = cache_k.shape[2]
    tp = n_b * seq
    ts = db * t_new
    t_all = tp + ts
    assert n_b == SUBLANES and db % SUBLANES == 0 and ssm_w % SSM_BLOCK_CH == 0
    assert w_in.shape[2] == 3 * qk_w + ssm_w + 2 * d and ssm_w == qk_w

    tm = _tile(math.gcd(seq, ts), 1024)
    n_i = t_all // tm
    cache_k = cache_k.reshape(cache_k.shape[0], cache_k.shape[1], page * n_heads, HEAD_LANES)
    cache_v = cache_v.reshape(cache_v.shape[0], cache_v.shape[1], page * n_heads, HEAD_LANES)
    x_parts = [x_prompt.reshape(tp, d), x_sample.reshape(ts, d)]
    zeros_h0 = jnp.zeros((n_b, n_state), F32)
    scale = HEAD_DIM ** -0.5

    n_ip = tp // tm
    prow = lambda i: jnp.minimum(i, n_ip - 1)
    srow = lambda i: jnp.maximum(i - n_ip, 0)
    which_part = lambda pids: jnp.where(pids[0] < n_ip, 0, 1)

    def row_lhs(a):
        return (a, (tm, a.shape[1]), lambda i, j: (i, 0))

    def split_lhs(a_p, a_s):
        return [(a_p, (tm, a_p.shape[1]), lambda i, j: (prow(i), 0)),
                (a_s, (tm, a_s.shape[1]), lambda i, j: (srow(i), 0))]

    k_out, v_out, hrp_out, hip_out, hrs_out, his_out = [], [], [], [], [], []
    for layer in range(depth):
        lam_0 = _lambda_init(layer)
        lam = (jnp.exp(jnp.sum(lambda_q1[layer] * lambda_k1[layer]))
               - jnp.exp(jnp.sum(lambda_q2[layer] * lambda_k2[layer])) + lam_0).astype(F32)
        xn = rmsnorm_cast(x_parts if layer == 0 else [x], norm_mix[layer], _tile(ts, 512))

        tn = _tile(qk_w, 512)
        hpt = tn // HEAD_LANES
        nq_t, ns_t, ng_t = qk_w // tn, ssm_w // tn, 2 * d // tn
        gain_row = lambda g: (jnp.tile(g, tn // HEAD_DIM).reshape(1, tn), (1, tn), lambda i, j: (0, 0))

        def ranged_out(width, dtype, lo, n):
            return (jax.ShapeDtypeStruct((t_all, width), dtype), (tm, tn),
                    lambda i, j: (i, jnp.clip(j - lo, 0, n - 1)))

        head_rows_out = (jax.ShapeDtypeStruct((t_all * n_heads, HEAD_LANES), F32),
                         (tm * n_heads, HEAD_LANES), lambda i, j: (i, 0))

        def stores(j, lo, n, value_fn, tile_refs, head_rows_ref=None):
            @pl.when(jnp.logical_and(j >= lo, j < lo + n))
            def _():
                value = value_fn()
                for r in tile_refs:
                    r[...] = value.astype(r.dtype)
                if head_rows_ref is not None:
                    for hl in range(hpt):
                        rows = pl.ds((j - lo) * hpt + hl, tm, stride=n_heads)
                        head_rows_ref[rows, :] = value[:, hl * HEAD_LANES:(hl + 1) * HEAD_LANES]

        def ep_qkv(accs, ex, out, pids):
            j = pids[1]
            y = accs[0]

            def head_normed(gain_ref, mult):
                return y * lax.rsqrt(_segment_mean64(y * y) + EPS) * gain_ref[...] * mult

            qn_r, k32_r, knb_r, v32_r, vb_r = out
            stores(j, 0, nq_t, lambda: head_normed(ex[0], scale), [qn_r])
            stores(j, nq_t, nq_t, lambda: head_normed(ex[1], 1.0), [knb_r], k32_r)
            stores(j, 2 * nq_t, nq_t, lambda: y, [vb_r], v32_r)

        qn, k_rows, kn_b, v_rows, v_b = fused_matmul(
            "proj_qkv", (n_i, 3 * nq_t), [row_lhs(xn)],
            [(0, w_in, (None, d, tn), lambda i, j: (layer, 0, j))],
            [gain_row(q_norm[layer]), gain_row(k_norm[layer])],
            [ranged_out(qk_w, BF16, 0, nq_t),
             head_rows_out, ranged_out(qk_w, BF16, nq_t, nq_t),
             head_rows_out, ranged_out(qk_w, BF16, 2 * nq_t, nq_t)], ep_qkv)

        def ep_ug(accs, ex, out, pids):
            stores(pids[1], 0, ns_t, lambda: accs[0], [out[0]])
            stores(pids[1], ns_t, ng_t, lambda: jax.nn.sigmoid(accs[0]), [out[1]])

        u_all, gates = fused_matmul(
            "proj_ug", (n_i, ns_t + ng_t), [row_lhs(xn)],
            [(0, w_in, (None, d, tn), lambda i, j: (layer, 0, j + 3 * nq_t))], [],
            [ranged_out(ssm_w, F32, 0, ns_t), ranged_out(2 * d, F32, ns_t, ng_t)], ep_ug)

        out_scale = 1.0 - lam_0
        o_p = attn_prompt(lam, qn, kn_b, v_b, subln[layer], out_scale, n_b, seq, n_heads)
        k_s = k_rows[tp * n_heads:].reshape(db, t_new * n_heads, HEAD_LANES)
        v_s = v_rows[tp * n_heads:].reshape(db, t_new * n_heads, HEAD_LANES)
        q_s = qn[tp:].astype(F32).reshape(db, t_new, qk_w)
        o_s = attn_sample(lam, page_table, q_s, k_s, v_s, cache_k, cache_v, layer, subln[layer],
                          out_scale, n_heads).reshape(ts, qk_w)

        a_re, a_im, b_blk, c_blk = _ssm_weights(
            ssm_a_re[layer], ssm_a_im[layer], ssm_b_re[layer], ssm_b_im[layer],
            ssm_c_re[layer], ssm_c_im[layer], ssm_log_dt[layer])
        d_row = ssm_d[layer].reshape(1, ssm_w)
        u_p = u_all[:tp].reshape(n_b, seq, ssm_w).transpose(1, 0, 2).reshape(tp, ssm_w)
        u_s = u_all[tp:].reshape(db, t_new, ssm_w).transpose(1, 0, 2).reshape(ts, ssm_w)
        yg_p, hrp, hip = ssm_scan(u_p, zeros_h0, zeros_h0, a_re, a_im, b_blk, c_blk, d_row,
                                  n_b, _tile(seq, 256))
        yg_s, hrs, his = ssm_scan(u_s, state_ssm_re[layer].reshape(db, n_state),
                                  state_ssm_im[layer].reshape(db, n_state),
                                  a_re, a_im, b_blk, c_blk, d_row, db, t_new)
        yg_p = yg_p.reshape(seq, n_b, ssm_w).transpose(1, 0, 2).reshape(tp, ssm_w)
        yg_s = yg_s.reshape(t_new, db, ssm_w).transpose(1, 0, 2).reshape(ts, ssm_w)

        def ep_glu(accs, ex, out, pids):
            yg_tile = jnp.where(pids[0] < n_ip, ex[0][...], ex[1][...])
            out[0][...] = (yg_tile * jax.nn.sigmoid(accs[0])).astype(out[0].dtype)

        tn = _tile(ssm_w, 512)
        y_ssm, = fused_matmul(
            "ssm_glu", (n_i, ssm_w // tn), [split_lhs(yg_p, yg_s)],
            [(0, w_glu, (None, ssm_w, tn), lambda i, j: (layer, 0, j))],
            [(yg_p, (tm, tn), lambda i, j: (prow(i), j)),
             (yg_s, (tm, tn), lambda i, j: (srow(i), j))],
            [(jax.ShapeDtypeStruct((t_all, ssm_w), BF16), (tm, tn), lambda i, j: (i, j))], ep_glu,
            which=which_part)

        tn = _tile(d, 512)

        def ep_merge(accs, ex, out, pids):
            out[0][...] = (ex[0][...] * accs[0] + ex[1][...] * accs[1]).astype(out[0].dtype)

        merged, = fused_matmul(
            "merge", (n_i, d // tn), [split_lhs(o_p, o_s), row_lhs(y_ssm)],
            [(0, w_proj_attn, (None, qk_w, tn), lambda i, j: (layer, 0, j)),
             (1, w_proj_ssm, (None, ssm_w, tn), lambda i, j: (layer, 0, j))],
            [(gates, (tm, tn), lambda i, j: (i, j)),
             (gates, (tm, tn), lambda i, j: (i, j + d // tn))],
            [(jax.ShapeDtypeStruct((t_all, d), BF16), (tm, tn), lambda i, j: (i, j))], ep_merge,
            which=which_part)

        def ep_residual(accs, ex, out, pids):
            res = ex[0][...] if len(ex) == 1 else jnp.where(pids[0] < n_ip, ex[0][...], ex[1][...])
            out[0][...] = res + accs[0]

        if layer == 0:
            x_tiles = [(x_parts[0], (tm, tn), lambda i, j: (prow(i), j)),
                       (x_parts[1], (tm, tn), lambda i, j: (srow(i), j))]
        else:
            x_tiles = [(x, (tm, tn), lambda i, j: (i, j))]
        x, = fused_matmul(
            "out_proj", (n_i, d // tn), [row_lhs(merged)],
            [(0, w_out, (None, d, tn), lambda i, j: (layer, 0, j))], x_tiles,
            [(jax.ShapeDtypeStruct((t_all, d), F32), (tm, tn), lambda i, j: (i, j))], ep_residual)

        def ep_swiglu(accs, ex, out, pids):
            out[0][...] = (jax.nn.silu(accs[0]) * accs[1]).astype(out[0].dtype)

        jl = layer // 2
        if layer % 2 == 0:
            d_ff = ffn_w_gate.shape[2]
            tn = _tile(d_ff, 512)

            def norm_ffn_rows(block, ex):
                ms = jnp.mean(block * block, axis=-1, keepdims=True)
                return block * lax.rsqrt(ms + EPS) * ex[0][...]

            h, = fused_matmul(
                "ffn_gate_up", (n_i, d_ff // tn), [row_lhs(x)],
                [(0, ffn_w_gate, (None, d, tn), lambda i, j: (jl, 0, j)),
                 (0, ffn_w_up, (None, d, tn), lambda i, j: (jl, 0, j))],
                [(norm_ffn[layer].reshape(1, d), (1, d), lambda i, j: (0, 0))],
                [(jax.ShapeDtypeStruct((t_all, d_ff), BF16), (tm, tn), lambda i, j: (i, j))],
                ep_swiglu, prep={0: norm_ffn_rows})
            tn = _tile(d, 256)
            x, = fused_matmul(
                "ffn_down", (n_i, d // tn), [row_lhs(h)],
                [(0, ffn_w_down, (None, d_ff, tn), lambda i, j: (jl, 0, j))],
                [(x, (tm, tn), lambda i, j: (i, j))],
                [(jax.ShapeDtypeStruct((t_all, d), F32), (tm, tn), lambda i, j: (i, j))],
                ep_residual)
        else:
            xn2, sel = rmsnorm_router(x, norm_ffn[layer], router[jl], _tile(t_all, 512))
            last = layer == depth - 1
            x_out = moe_sparse(x, xn2, sel, moe_w_gate, moe_w_up, moe_w_down, jl,
                               split_rows=tp if last else None)
            x = x_out[0]

        k_out.append(k_rows)
        v_out.append(v_rows)
        hrp_out.append(hrp)
        hip_out.append(hip)
        hrs_out.append(hrs)
        his_out.append(his)

    def split(parts, lo, hi, shape):
        return jnp.stack([p[lo * n_heads:hi * n_heads].reshape(shape) for p in parts], axis=0)

    kv_p = (n_b, seq, n_heads, HEAD_LANES)
    kv_s = (db, t_new, n_heads, HEAD_LANES)
    st = lambda parts, n: jnp.stack([p.reshape(n, n_groups, state) for p in parts], axis=0)
    if depth % 2 == 0:
        x_p, x_s = x_out
    else:
        x_p, x_s = x[:tp], x[tp:]
    return (x_p.reshape(n_b, seq, d), x_s.reshape(db, t_new, d),
            split(k_out, 0, tp, kv_p), split(v_out, 0, tp, kv_p),
            st(hrp_out, n_b), st(hip_out, n_b),
            split(k_out, tp, t_all, kv_s), split(v_out, tp, t_all, kv_s),
            st(hrs_out, db), st(his_out, db))
```

```python
import functools
import math

import jax
import jax.numpy as jnp
from jax import lax
from jax.experimental import pallas as pl
from jax.experimental.pallas import tpu as pltpu

F32 = jnp.float32
BF16 = jnp.bfloat16

EPS = 1e-6
NEG_INF = -1e30
HEAD_DIM = 64
HEAD_LANES = 2 * HEAD_DIM
LANES = 128
MXU_DIM = 256
SUBLANES = 8
GROUP_CH = 16
SSM_BLOCK_CH = 256
TOP_K = 2
VMEM_LIMIT_BYTES = 56 * 1024 * 1024


def _tile(dim, pref):
    t = min(pref, dim)
    while dim % t:
        t //= 2
    return t


def _log2(n):
    assert n & (n - 1) == 0
    return n.bit_length() - 1


def _params(n_axes):
    return pltpu.CompilerParams(dimension_semantics=("arbitrary",) * n_axes,
                                vmem_limit_bytes=VMEM_LIMIT_BYTES)


def _rmsnorm_kernel(*refs, n_first):
    x_refs, g_ref, o_ref = refs[:-2], refs[-2], refs[-1]

    def emit(x_ref):
        x = x_ref[...]
        ms = jnp.mean(x * x, axis=-1, keepdims=True)
        o_ref[...] = (x * lax.rsqrt(ms + EPS) * g_ref[...]).astype(o_ref.dtype)

    if len(x_refs) == 1:
        emit(x_refs[0])
    else:
        i = pl.program_id(0)
        pl.when(i < n_first)(lambda: emit(x_refs[0]))
        pl.when(i >= n_first)(lambda: emit(x_refs[1]))


def rmsnorm_cast(parts, g, tm):
    d = parts[0].shape[1]
    t = sum(p.shape[0] for p in parts)
    n_first = parts[0].shape[0] // tm
    if len(parts) == 1:
        x_specs = [pl.BlockSpec((tm, d), lambda i: (i, 0))]
    else:
        assert len(parts) == 2 and all(p.shape[0] % tm == 0 for p in parts)
        x_specs = [pl.BlockSpec((tm, d), lambda i: (jnp.minimum(i, n_first - 1), 0)),
                   pl.BlockSpec((tm, d), lambda i: (jnp.maximum(i - n_first, 0), 0))]
    return pl.pallas_call(
        functools.partial(_rmsnorm_kernel, n_first=n_first),
        grid=(t // tm,),
        in_specs=x_specs + [pl.BlockSpec((1, d), lambda i: (0, 0))],
        out_specs=pl.BlockSpec((tm, d), lambda i: (i, 0)),
        out_shape=jax.ShapeDtypeStruct((t, d), BF16),
        compiler_params=_params(1),
        name="rmsnorm_cast",
    )(*parts, g.reshape(1, d))


def _split3(a):
    hi = a.astype(BF16)
    r1 = a - hi.astype(F32)
    mid = r1.astype(BF16)
    lo = (r1 - mid.astype(F32)).astype(BF16)
    return hi, mid, lo


def _rmsnorm_router_kernel(x_ref, g_ref, rw_ref, o_ref, gates_ref, *, n_experts):
    x = x_ref[...]
    ms = jnp.mean(x * x, axis=-1, keepdims=True)
    xn = x * lax.rsqrt(ms + EPS) * g_ref[...]
    o_ref[...] = xn.astype(o_ref.dtype)
    xh, xm, xl = _split3(xn)
    wh, wm, wl = _split3(rw_ref[...])
    dot = functools.partial(jnp.dot, preferred_element_type=F32)
    logits = (dot(xh, wh) + (dot(xh, wm) + dot(xm, wh))
              + (dot(xh, wl) + dot(xm, wm) + dot(xl, wh)))
    lane = lax.broadcasted_iota(jnp.int32, logits.shape, 1).astype(F32)
    logits = jnp.where(lane < n_experts, logits, -jnp.inf)
    m1 = jnp.max(logits, axis=-1, keepdims=True)
    i1 = jnp.min(jnp.where(logits == m1, lane, float(LANES)), axis=-1, keepdims=True)
    rest = jnp.where(lane == i1, -jnp.inf, logits)
    m2 = jnp.max(rest, axis=-1, keepdims=True)
    i2 = jnp.min(jnp.where(rest == m2, lane, float(LANES)), axis=-1, keepdims=True)
    e2 = jnp.exp(m2 - m1)
    g1 = 1.0 / (1.0 + e2)
    g2 = e2 / (1.0 + e2)
    gates_ref[...] = jnp.where(lane == 0.0, i1, jnp.where(lane == 1.0, i2, jnp.where(
        lane == 2.0, g1, jnp.where(lane == 3.0, g2, 0.0))))


def rmsnorm_router(x, g, router_w, tm):
    t, d = x.shape
    n_experts = router_w.shape[1]
    rw = jnp.zeros((d, LANES), F32).at[:, :n_experts].set(router_w)
    return pl.pallas_call(
        functools.partial(_rmsnorm_router_kernel, n_experts=n_experts),
        grid=(t // tm,),
        in_specs=[pl.BlockSpec((tm, d), lambda i: (i, 0)),
                  pl.BlockSpec((1, d), lambda i: (0, 0)),
                  pl.BlockSpec((d, LANES), lambda i: (0, 0))],
        out_specs=[pl.BlockSpec((tm, d), lambda i: (i, 0)),
                   pl.BlockSpec((tm, LANES), lambda i: (i, 0))],
        out_shape=[jax.ShapeDtypeStruct((t, d), F32),
                   jax.ShapeDtypeStruct((t, LANES), F32)],
        compiler_params=_params(1),
        name="rmsnorm_router",
    )(x, g.reshape(1, d), rw)


def _issue_row_gather(idx_ref, n_rows, src_hbm, dst_buf, slot, sem):
    def body(r, carry):
        row = idx_ref[0, r]
        pltpu.make_async_copy(src_hbm.at[pl.ds(row, 1)], dst_buf.at[slot, pl.ds(r, 1)],
                              sem.at[slot]).start()
        return carry
    lax.fori_loop(0, n_rows, body, 0, unroll=8)


def _wait_row_gather(n_rows, src_hbm, dst_buf, slot, sem):
    pltpu.make_async_copy(src_hbm.at[pl.ds(0, n_rows)], dst_buf.at[slot], sem.at[slot]).wait()


def _gathered_rows(step, n_steps, first, idx_ref, idx_next_ref, src_hbm, buf, sem, n_rows):
    slot = step % 2

    @pl.when(first)
    def _():
        @pl.when(step == 0)
        def _():
            _issue_row_gather(idx_ref, n_rows, src_hbm, buf, 0, sem)

        _wait_row_gather(n_rows, src_hbm, buf, slot, sem)

        @pl.when(step + 1 < n_steps)
        def _():
            _issue_row_gather(idx_next_ref, n_rows, src_hbm, buf, 1 - slot, sem)

    return slot


def _moe_gather_kernel(idx_ref, idx_next_ref, xn_hbm, o_ref, buf, sem, *, tmb):
    bi = pl.program_id(0)
    slot = _gathered_rows(bi, pl.num_programs(0), bi >= 0, idx_ref, idx_next_ref, xn_hbm, buf, sem, tmb)
    o_ref[...] = buf[slot].astype(o_ref.dtype)


def _moe_gate_up_kernel(blk_e_ref, blk_rows_ref, x_ref, wg_ref, wu_ref, h_ref, *, tmb):
    del blk_e_ref
    bi = pl.program_id(0)
    wg = wg_ref[...].astype(BF16)
    wu = wu_ref[...].astype(BF16)

    def rows(rs):
        a = x_ref[rs, :]
        gate = jnp.dot(a, wg, preferred_element_type=F32)
        up = jnp.dot(a, wu, preferred_element_type=F32)
        h_ref[rs, :] = (jax.nn.silu(gate) * up).astype(h_ref.dtype)

    _by_valid_rows(blk_rows_ref[bi], tmb, rows, h_ref)


def _by_valid_rows(n_valid, tmb, compute, out_ref):
    half = tmb // 2
    zeros = lambda n: jnp.zeros((n, out_ref.shape[1]), out_ref.dtype)

    @pl.when(n_valid > half)
    def _():
        compute(slice(0, tmb))

    @pl.when(jnp.logical_and(n_valid > 0, n_valid <= half))
    def _():
        compute(slice(0, half))
        out_ref[half:tmb, :] = zeros(tmb - half)

    @pl.when(n_valid <= 0)
    def _():
        out_ref[...] = zeros(tmb)


def _moe_down_kernel(blk_e_ref, blk_rows_ref, h_ref, w_ref, y_ref, *, tmb):
    del blk_e_ref
    bi = pl.program_id(0)
    w = w_ref[...].astype(BF16)

    def rows(rs):
        y_ref[rs, :] = jnp.dot(h_ref[rs, :], w, preferred_element_type=F32)

    _by_valid_rows(blk_rows_ref[bi], tmb, rows, y_ref)


def _moe_combine_kernel(idx_ref, idx_next_ref, x_ref, sel_ref, ys_hbm, *rest, tc, n_first):
    o_refs, (buf, sem) = rest[:-2], rest[-2:]
    i = pl.program_id(0)
    slot = _gathered_rows(i, pl.num_programs(0), i >= 0, idx_ref, idx_next_ref, ys_hbm, buf, sem, 2 * tc)
    g1 = sel_ref[:, TOP_K:TOP_K + 1]
    g2 = sel_ref[:, TOP_K + 1:TOP_K + 2]
    y = x_ref[...] + (g1 * buf[slot, 0:tc, :] + g2 * buf[slot, tc:2 * tc, :])
    if len(o_refs) == 1:
        o_refs[0][...] = y
    else:
        @pl.when(i < n_first)
        def _():
            o_refs[0][...] = y

        @pl.when(i >= n_first)
        def _():
            o_refs[1][...] = y


def moe_sparse(x, xn, sel, w_gate, w_up, w_down, jl, split_rows=None):
    t, d = x.shape
    n_e, _, d_fe = w_gate.shape[1:]
    n_assign = TOP_K * t
    tmb = 1024 if n_assign >= 8 * 1024 else 256
    n_blk = n_assign // tmb + n_e
    n_rows = n_blk * tmb

    e_flat = sel[:, 0:TOP_K].astype(jnp.int32).reshape(n_assign)
    onehot = (e_flat[:, None] == jnp.arange(n_e, dtype=jnp.int32)[None, :]).astype(jnp.int32)
    counts = jnp.sum(onehot, axis=0)
    rank = jnp.sum((jnp.cumsum(onehot, axis=0) - onehot) * onehot, axis=1)
    blks_e = (counts + tmb - 1) // tmb
    blk_end = jnp.cumsum(blks_e)
    blk_start = blk_end - blks_e
    pos = blk_start[e_flat] * tmb + rank
    src_tok = (jnp.arange(n_rows, dtype=jnp.int32) % t).at[pos].set(
        jnp.arange(n_assign, dtype=jnp.int32) // TOP_K)
    blk_ids = jnp.arange(n_blk, dtype=jnp.int32)
    n_active = blk_end[-1]
    blk_e = jnp.sum((blk_ids[:, None] >= blk_end[None, :]).astype(jnp.int32), axis=1)
    blk_e = jnp.minimum(blk_e, blk_e[jnp.maximum(n_active - 1, 0)])
    blk_rows = jnp.clip(counts[blk_e] - (blk_ids - blk_start[blk_e]) * tmb, 0, tmb)
    blk_rows = jnp.where(blk_ids < n_active, blk_rows, 0).astype(jnp.int32)

    idx2d = src_tok.reshape(n_blk, 1, tmb)
    xs = pl.pallas_call(
        functools.partial(_moe_gather_kernel, tmb=tmb),
        grid=(n_blk,),
        in_specs=[pl.BlockSpec((None, 1, tmb), lambda bi: (bi, 0, 0), memory_space=pltpu.SMEM),
                  pl.BlockSpec((None, 1, tmb), lambda bi: (jnp.minimum(bi + 1, n_blk - 1), 0, 0),
                               memory_space=pltpu.SMEM),
                  pl.BlockSpec(memory_space=pl.ANY)],
        out_specs=pl.BlockSpec((tmb, d), lambda bi: (bi, 0)),
        out_shape=jax.ShapeDtypeStruct((n_rows, d), BF16),
        scratch_shapes=[pltpu.VMEM((2, tmb, d), F32),
                        pltpu.SemaphoreType.DMA((2,))],
        compiler_params=_params(1),
        name="moe_gather",
    )(idx2d, idx2d, xn)

    tn = _tile(d_fe, 256)
    nj = d_fe // tn
    col = lambda bi, j, be, br: jnp.where(br[bi] > 0, j, nj - 1)
    h = pl.pallas_call(
        functools.partial(_moe_gate_up_kernel, tmb=tmb),
        grid_spec=pltpu.PrefetchScalarGridSpec(
            num_scalar_prefetch=2,
            grid=(n_blk, nj),
            in_specs=[pl.BlockSpec((tmb, d), lambda bi, j, be, br: (bi, 0)),
                      pl.BlockSpec((None, None, d, tn), lambda bi, j, be, br: (jl, be[bi], 0, col(bi, j, be, br))),
                      pl.BlockSpec((None, None, d, tn), lambda bi, j, be, br: (jl, be[bi], 0, col(bi, j, be, br)))],
            out_specs=pl.BlockSpec((tmb, tn), lambda bi, j, be, br: (bi, j))),
        out_shape=jax.ShapeDtypeStruct((n_rows, d_fe), BF16),
        compiler_params=_params(2),
        name="moe_gate_up",
    )(blk_e, blk_rows, xs, w_gate, w_up)

    tn = _tile(d, 512)
    nj = d // tn
    ys = pl.pallas_call(
        functools.partial(_moe_down_kernel, tmb=tmb),
        grid_spec=pltpu.PrefetchScalarGridSpec(
            num_scalar_prefetch=2,
            grid=(n_blk, nj),
            in_specs=[pl.BlockSpec((tmb, d_fe), lambda bi, j, be, br: (bi, 0)),
                      pl.BlockSpec((None, None, d_fe, tn), lambda bi, j, be, br: (jl, be[bi], 0, col(bi, j, be, br)))],
            out_specs=pl.BlockSpec((tmb, tn), lambda bi, j, be, br: (bi, j))),
        out_shape=jax.ShapeDtypeStruct((n_rows, d), F32),
        compiler_params=_params(2),
        name="moe_down",
    )(blk_e, blk_rows, h, w_down)

    tc = _tile(t, 512)
    n_i = t // tc
    pos_blk = pos.reshape(n_i, tc, TOP_K).transpose(0, 2, 1).reshape(n_i, 1, TOP_K * tc)
    if split_rows is None:
        n_first = n_i
        out_specs = [pl.BlockSpec((tc, d), lambda i: (i, 0))]
        out_shape = [jax.ShapeDtypeStruct((t, d), F32)]
    else:
        n_first = split_rows // tc
        out_specs = [pl.BlockSpec((tc, d), lambda i: (jnp.minimum(i, n_first - 1), 0)),
                     pl.BlockSpec((tc, d), lambda i: (jnp.maximum(i - n_first, 0), 0))]
        out_shape = [jax.ShapeDtypeStruct((split_rows, d), F32),
                     jax.ShapeDtypeStruct((t - split_rows, d), F32)]
    return pl.pallas_call(
        functools.partial(_moe_combine_kernel, tc=tc, n_first=n_first),
        grid=(n_i,),
        in_specs=[pl.BlockSpec((None, 1, TOP_K * tc), lambda i: (i, 0, 0), memory_space=pltpu.SMEM),
                  pl.BlockSpec((None, 1, TOP_K * tc), lambda i: (jnp.minimum(i + 1, n_i - 1), 0, 0),
                               memory_space=pltpu.SMEM),
                  pl.BlockSpec((tc, d), lambda i: (i, 0)),
                  pl.BlockSpec((tc, LANES), lambda i: (i, 0)),
                  pl.BlockSpec(memory_space=pl.ANY)],
        out_specs=out_specs,
        out_shape=out_shape,
        scratch_shapes=[pltpu.VMEM((2, TOP_K * tc, d), F32),
                        pltpu.SemaphoreType.DMA((2,))],
        compiler_params=_params(1),
        name="moe_combine",
    )(pos_blk, pos_blk, x, sel, ys)


def fused_matmul(name, grid, lhs, terms, extras, outs, epilogue, which=None, prep=None):
    lhs = [alts if isinstance(alts, list) else [alts] for alts in lhs]
    flat_lhs = [alt for alts in lhs for alt in alts]
    first_ref = [sum(len(a) for a in lhs[:k]) for k in range(len(lhs))]
    n_lhs, n_terms, n_ex, n_out = len(flat_lhs), len(terms), len(extras), len(outs)
    staged = [k for k, alts in enumerate(lhs) if len(alts) > 1 or alts[0][0].dtype != BF16]

    def kernel(*refs):
        lhs_refs = refs[:n_lhs]
        w_refs = refs[n_lhs:n_lhs + n_terms]
        ex_refs = refs[n_lhs + n_terms:n_lhs + n_terms + n_ex]
        out_refs = refs[n_lhs + n_terms + n_ex:n_lhs + n_terms + n_ex + n_out]
        scr_refs = refs[n_lhs + n_terms + n_ex + n_out:]
        pids = [pl.program_id(a) for a in range(len(grid))]
        if staged:
            first = pids[1] == 0
            for p in pids[2:]:
                first = jnp.logical_and(first, p == 0)
            for s, k in enumerate(staged):
                for a in range(len(lhs[k])):
                    use = first if len(lhs[k]) == 1 else jnp.logical_and(first, which(pids) == a)

                    @pl.when(use)
                    def _(s=s, k=k, r=first_ref[k] + a):
                        block = lhs_refs[r][...]
                        if prep and k in prep:
                            block = prep[k](block.astype(F32), ex_refs)
                        scr_refs[s][...] = block.astype(BF16)

        accs = []
        for (li, _, _, _), w_ref in zip(terms, w_refs):
            a = scr_refs[staged.index(li)][...] if li in staged else lhs_refs[first_ref[li]][...]
            accs.append(jnp.dot(a, w_ref[...].astype(BF16), preferred_element_type=F32))
        epilogue(accs, ex_refs, out_refs, pids)

    in_specs = ([pl.BlockSpec(bs, im) for _, bs, im in flat_lhs]
                + [pl.BlockSpec(bs, im) for _, _, bs, im in terms]
                + [pl.BlockSpec(bs, im) for _, bs, im in extras])
    args = [a for a, _, _ in flat_lhs] + [w for _, w, _, _ in terms] + [a for a, _, _ in extras]
    scratch = [pltpu.VMEM(tuple(b for b in lhs[k][0][1] if b is not None), BF16) for k in staged]
    res = pl.pallas_call(
        kernel,
        grid=grid,
        in_specs=in_specs,
        out_specs=[pl.BlockSpec(bs, im) for _, bs, im in outs],
        out_shape=[sd for sd, _, _ in outs],
        scratch_shapes=scratch,
        compiler_params=_params(len(grid)),
        name=name,
    )(*args)
    return res


def _segment_mean64(sq):
    n = sq.shape[1]
    w = MXU_DIM if n % MXU_DIM == 0 else LANES
    r = lax.broadcasted_iota(jnp.int32, (w, w), 0) >> _log2(HEAD_DIM)
    c = lax.broadcasted_iota(jnp.int32, (w, w), 1) >> _log2(HEAD_DIM)
    ones = (r == c).astype(BF16)
    hi = sq.astype(BF16)
    lo = (sq - hi.astype(F32)).astype(BF16)
    dot = functools.partial(jnp.dot, preferred_element_type=F32)
    cols = []
    for j in range(n // w):
        sl = slice(j * w, (j + 1) * w)
        cols.append(dot(hi[:, sl], ones) + dot(lo[:, sl], ones))
    return jnp.concatenate(cols, axis=1) * (1.0 / HEAD_DIM)


ATTN_HEADS_PER_STEP = 4


def _attn_prompt_kernel(lam_ref, q_ref, k_ref, v_ref, g_ref, o_ref, vt_scr, *, tq, out_scale):
    qi = pl.program_id(2)
    lam = lam_ref[0]
    cols = 2 * tq
    n_h = q_ref.shape[1] // HEAD_LANES
    head = lambda h: slice(h * HEAD_LANES, (h + 1) * HEAD_LANES)

    @pl.when(qi == 0)
    def _():
        vt_scr[...] = v_ref[...].astype(F32).T.astype(BF16)

    lane = lax.broadcasted_iota(jnp.int32, (tq, HEAD_LANES), 1)
    qs = []
    for h in range(n_h):
        q = q_ref[:, head(h)].astype(F32)
        qs.append(jnp.concatenate([jnp.where(lane < HEAD_DIM, q, 0.0),
                                   jnp.where(lane >= HEAD_DIM, q, 0.0)], axis=0).astype(BF16))

    def block(kb, carry, diagonal):
        start = pl.multiple_of(kb * tq, tq)
        out = []
        for h in range(n_h):
            m, l, acc = carry[h]
            k = k_ref[pl.ds(start, tq), head(h)]
            s = lax.dot_general(k, qs[h], (((1,), (1,)), ((), ())), preferred_element_type=F32)
            if diagonal:
                key = lax.broadcasted_iota(jnp.int32, s.shape, 0)
                qry = lax.broadcasted_iota(jnp.int32, s.shape, 1)
                qry = jnp.where(qry >= tq, qry - tq, qry)
                s = jnp.where(key <= qry, s, NEG_INF)
            m_new = jnp.maximum(m, jnp.max(s, axis=0, keepdims=True))
            alpha = jnp.exp(m - m_new)
            p = jnp.exp(s - m_new)
            l = alpha * l + jnp.sum(p, axis=0, keepdims=True)
            pv = jnp.dot(vt_scr[head(h), pl.ds(start, tq)], p.astype(BF16),
                         preferred_element_type=F32)
            out.append((m_new, l, alpha * acc + pv))
        return tuple(out)

    carry = tuple((jnp.full((1, cols), NEG_INF, F32), jnp.zeros((1, cols), F32),
                   jnp.zeros((HEAD_LANES, cols), F32)) for _ in range(n_h))
    carry = lax.fori_loop(0, qi, lambda kb, c: block(kb, c, False), carry)
    carry = block(qi, carry, True)
    for h in range(n_h):
        _, l, acc = carry[h]
        ot = acc / l
        o = (ot[:, :tq] - lam * ot[:, tq:]).T
        o = o * lax.rsqrt(jnp.mean(o * o, axis=-1, keepdims=True) + EPS) * g_ref[...] * out_scale
        o_ref[:, head(h)] = o.astype(o_ref.dtype)


def attn_prompt(lam, qn, kb, vb, subln, out_scale, n_batch, seq, n_heads):
    tq = _tile(seq, 256)
    nq = seq // tq
    hw = ATTN_HEADS_PER_STEP * HEAD_LANES
    assert n_heads % ATTN_HEADS_PER_STEP == 0
    return pl.pallas_call(
        functools.partial(_attn_prompt_kernel, tq=tq, out_scale=out_scale),
        grid=(n_batch, n_heads // ATTN_HEADS_PER_STEP, nq),
        in_specs=[pl.BlockSpec(memory_space=pltpu.SMEM),
                  pl.BlockSpec((tq, hw), lambda b, h, i: (b * nq + i, h)),
                  pl.BlockSpec((seq, hw), lambda b, h, i: (b, h)),
                  pl.BlockSpec((seq, hw), lambda b, h, i: (b, h)),
                  pl.BlockSpec((1, HEAD_LANES), lambda b, h, i: (0, 0))],
        out_specs=pl.BlockSpec((tq, hw), lambda b, h, i: (b * nq + i, h)),
        out_shape=jax.ShapeDtypeStruct((n_batch * seq, n_heads * HEAD_LANES), BF16),
        scratch_shapes=[pltpu.VMEM((hw, seq), BF16)],
        compiler_params=_params(3),
        name="attn_prompt",
    )(lam.reshape(1), qn, kb, vb, subln.reshape(1, HEAD_LANES))


def _attn_sample_kernel(pt_ref, lam_ref, q_ref, kn_ref, vn_ref, *rest, n_heads, t_new, out_scale,
                        pages_per_step):
    del pt_ref
    kc_refs = rest[:pages_per_step]
    vc_refs = rest[pages_per_step:2 * pages_per_step]
    g_ref, o_ref, q_scr, bias_scr, s_scr, m_scr, l_scr, acc_scr = rest[2 * pages_per_step:]
    p = pl.program_id(1)
    n_steps = pl.num_programs(1)
    rows = 2 * n_heads * t_new
    page_rows = kc_refs[0].shape[0]
    nt = (((1,), (1,)), ((), ()))
    log_t, head_mask = _log2(t_new), n_heads - 1

    @pl.when(p == 0)
    def _():
        q = q_ref[0]
        lane = lax.broadcasted_iota(jnp.int32, (t_new, HEAD_LANES), 1)
        pieces = []
        for half in range(2):
            for h in range(n_heads):
                qh = q[:, h * HEAD_LANES:(h + 1) * HEAD_LANES]
                pieces.append(jnp.where((lane >= HEAD_DIM) == bool(half), qh, 0.0))
        q_scr[...] = jnp.concatenate(pieces, axis=0).astype(BF16)
        r = lax.broadcasted_iota(jnp.int32, (rows, page_rows), 0)
        c = lax.broadcasted_iota(jnp.int32, (rows, page_rows), 1)
        bias_scr[...] = jnp.where(((r >> log_t) & head_mask) == (c & head_mask), 0.0, NEG_INF)
        m_scr[...] = jnp.full(m_scr.shape, NEG_INF, F32)
        l_scr[...] = jnp.zeros(l_scr.shape, F32)
        acc_scr[...] = jnp.zeros(acc_scr.shape, F32)

    def update(keys, values, bias):
        q = q_scr[...]
        m_old = m_scr[...]
        m_new = m_old
        for i, k in enumerate(keys):
            s = lax.dot_general(q, k(), nt, preferred_element_type=F32) + bias
            s_scr[i, :, 0:s.shape[1]] = s
            m_new = jnp.maximum(m_new, jnp.max(s, axis=-1, keepdims=True))
        alpha = jnp.exp(m_old - m_new)
        l = alpha * l_scr[...]
        acc = alpha * acc_scr[...]
        for i, v in enumerate(values):
            pe = jnp.exp(s_scr[i, :, 0:bias.shape[1]] - m_new)
            l = l + jnp.sum(pe, axis=-1, keepdims=True)
            acc = acc + jnp.dot(pe.astype(BF16), v(), preferred_element_type=F32)
        l_scr[...] = l
        acc_scr[...] = acc
        m_scr[...] = m_new

    update([lambda r=r: r[...].astype(BF16) for r in kc_refs],
           [lambda r=r: r[...].astype(BF16) for r in vc_refs], bias_scr[...])

    @pl.when(p == n_steps - 1)
    def _():
        new_rows = t_new * n_heads
        pad = jnp.zeros((LANES - new_rows, HEAD_LANES), F32)
        k_new = jnp.concatenate([kn_ref[0], pad], axis=0).astype(BF16)
        v_new = jnp.concatenate([vn_ref[0], pad], axis=0).astype(BF16)
        r = lax.broadcasted_iota(jnp.int32, (rows, LANES), 0)
        c = lax.broadcasted_iota(jnp.int32, (rows, LANES), 1)
        ok = jnp.logical_and(((r >> log_t) & head_mask) == (c & head_mask),
                             (c >> _log2(n_heads)) <= (r & (t_new - 1)))
        update([lambda: k_new], [lambda: v_new], jnp.where(ok, 0.0, NEG_INF))
        lam = lam_ref[0]
        inv_l = 1.0 / l_scr[...]
        outs = []
        for h in range(n_heads):
            r1 = slice(h * t_new, (h + 1) * t_new)
            r2 = slice((n_heads + h) * t_new, (n_heads + h + 1) * t_new)
            o = acc_scr[r1, :] * inv_l[r1] - lam * (acc_scr[r2, :] * inv_l[r2])
            o = o * lax.rsqrt(jnp.mean(o * o, axis=-1, keepdims=True) + EPS) * g_ref[...] * out_scale
            outs.append(o)
        o_ref[0] = jnp.concatenate(outs, axis=1).astype(o_ref.dtype)


def attn_sample(lam, page_table, q_s, k_s, v_s, cache_k, cache_v, layer, subln, out_scale, n_heads):
    db, t_new, width = q_s.shape
    page_rows = cache_k.shape[2]
    new_rows = t_new * n_heads
    n_pages = page_table.shape[1]
    rows = 2 * n_heads * t_new
    assert new_rows <= LANES
    pps = _tile(n_pages, 8)

    def page_spec(g):
        return pl.BlockSpec((None, None, page_rows, HEAD_LANES),
                            lambda b, p, pt: (layer, pt[b, p * pps + g], 0, 0))

    grid_spec = pltpu.PrefetchScalarGridSpec(
        num_scalar_prefetch=1,
        grid=(db, n_pages // pps),
        in_specs=([pl.BlockSpec(memory_space=pltpu.SMEM),
                   pl.BlockSpec((1, t_new, width), lambda b, p, pt: (b, 0, 0)),
                   pl.BlockSpec((1, new_rows, HEAD_LANES), lambda b, p, pt: (b, 0, 0)),
                   pl.BlockSpec((1, new_rows, HEAD_LANES), lambda b, p, pt: (b, 0, 0))]
                  + [page_spec(g) for g in range(pps)] + [page_spec(g) for g in range(pps)]
                  + [pl.BlockSpec((1, HEAD_LANES), lambda b, p, pt: (0, 0))]),
        out_specs=pl.BlockSpec((1, t_new, width), lambda b, p, pt: (b, 0, 0)),
        scratch_shapes=[pltpu.VMEM((rows, HEAD_LANES), BF16),
                        pltpu.VMEM((rows, page_rows), F32),
                        pltpu.VMEM((pps, rows, page_rows), F32),
                        pltpu.VMEM((rows, 1), F32),
                        pltpu.VMEM((rows, 1), F32),
                        pltpu.VMEM((rows, HEAD_LANES), F32)],
    )
    return pl.pallas_call(
        functools.partial(_attn_sample_kernel, n_heads=n_heads, t_new=t_new, out_scale=out_scale,
                          pages_per_step=pps),
        grid_spec=grid_spec,
        out_shape=jax.ShapeDtypeStruct((db, t_new, width), F32),
        compiler_params=_params(2),
        name="attn_sample",
    )(page_table, lam.reshape(1), q_s, k_s, v_s, *([cache_k] * pps), *([cache_v] * pps),
      subln.reshape(1, HEAD_LANES))


def _ssm_kernel(u_ref, h0re_ref, h0im_ref, are_ref, aim_ref, b_ref, c_ref, d_ref,
                y_ref, hre_ref, him_ref, bu_scr, *, nb, tc):
    c_idx = pl.program_id(1)
    sw = are_ref.shape[1]

    @pl.when(c_idx == 0)
    def _():
        hre_ref[...] = h0re_ref[...]
        him_ref[...] = h0im_ref[...]

    u = u_ref[...]
    bu_scr[...] = jnp.dot(u.astype(BF16), b_ref[...], preferred_element_type=F32)
    a_re = are_ref[...]
    a_im = aim_ref[...]
    for r in range(nb // SUBLANES):
        rs = slice(r * SUBLANES, (r + 1) * SUBLANES)

        def body(t, carry, r=r):
            h_re, h_im = carry
            row = pl.multiple_of(t * nb + r * SUBLANES, SUBLANES)
            n_re = a_re * h_re - a_im * h_im + bu_scr[pl.ds(row, SUBLANES), 0:sw]
            n_im = a_re * h_im + a_im * h_re + bu_scr[pl.ds(row, SUBLANES), sw:2 * sw]
            bu_scr[pl.ds(row, SUBLANES), 0:sw] = n_re
            bu_scr[pl.ds(row, SUBLANES), sw:2 * sw] = n_im
            return n_re, n_im

        h_re, h_im = lax.fori_loop(0, tc, body, (hre_ref[rs, :], him_ref[rs, :]))
        hre_ref[rs, :] = h_re
        him_ref[rs, :] = h_im
    y = jnp.dot(bu_scr[...].astype(BF16), c_ref[...], preferred_element_type=F32) + d_ref[...] * u
    y_ref[...] = jax.nn.gelu(y)


def ssm_scan(u_tm, h0_re, h0_im, a_re, a_im, b_blk, c_blk, d, nb, tc):
    rows, ch = u_tm.shape
    n_gb = ch // SSM_BLOCK_CH
    sw = h0_re.shape[1] // n_gb
    n_chunks = rows // (tc * nb)
    blk_rows = tc * nb
    return pl.pallas_call(
        functools.partial(_ssm_kernel, nb=nb, tc=tc),
        grid=(n_gb, n_chunks),
        in_specs=[pl.BlockSpec((blk_rows, SSM_BLOCK_CH), lambda g, c: (c, g)),
                  pl.BlockSpec((nb, sw), lambda g, c: (0, g)),
                  pl.BlockSpec((nb, sw), lambda g, c: (0, g)),
                  pl.BlockSpec((SUBLANES, sw), lambda g, c: (0, g)),
                  pl.BlockSpec((SUBLANES, sw), lambda g, c: (0, g)),
                  pl.BlockSpec((None, SSM_BLOCK_CH, 2 * sw), lambda g, c: (g, 0, 0)),
                  pl.BlockSpec((None, 2 * sw, SSM_BLOCK_CH), lambda g, c: (g, 0, 0)),
                  pl.BlockSpec((1, SSM_BLOCK_CH), lambda g, c: (0, g))],
        out_specs=[pl.BlockSpec((blk_rows, SSM_BLOCK_CH), lambda g, c: (c, g)),
                   pl.BlockSpec((nb, sw), lambda g, c: (0, g)),
                   pl.BlockSpec((nb, sw), lambda g, c: (0, g))],
        out_shape=[jax.ShapeDtypeStruct((rows, ch), F32),
                   jax.ShapeDtypeStruct(h0_re.shape, F32),
                   jax.ShapeDtypeStruct(h0_im.shape, F32)],
        scratch_shapes=[pltpu.VMEM((blk_rows, 2 * sw), F32)],
        compiler_params=_params(2),
        name="ssm_scan",
    )(u_tm, h0_re, h0_im, a_re, a_im, b_blk, c_blk, d)


def _ssm_weights(a_re, a_im, b_re, b_im, c_re, c_im, log_dt):
    n_groups, state = a_re.shape
    gpb = SSM_BLOCK_CH // GROUP_CH
    n_gb = n_groups // gpb
    dt = jnp.exp(log_dt)[:, None]
    mag = jnp.exp(dt * a_re)
    ab_re = mag * jnp.cos(dt * a_im)
    ab_im = mag * jnp.sin(dt * a_im)
    den = a_re * a_re + a_im * a_im
    nr = ab_re - 1.0
    coef_re = (nr * a_re + ab_im * a_im) / den
    coef_im = (ab_im * a_re - nr * a_im) / den
    bb_re = coef_re[..., None] * b_re - coef_im[..., None] * b_im
    bb_im = coef_re[..., None] * b_im + coef_im[..., None] * b_re
    eye = jnp.eye(gpb, dtype=F32)

    def b_block(bb):
        bb = bb.reshape(n_gb, gpb, state, GROUP_CH)
        return jnp.einsum("bgpc,gh->bgchp", bb, eye).reshape(n_gb, gpb * GROUP_CH, gpb * state)

    def c_block(cc):
        cc = cc.reshape(n_gb, gpb, GROUP_CH, state)
        return jnp.einsum("bgcp,gh->bgphc", cc, eye).reshape(n_gb, gpb * state, gpb * GROUP_CH)

    b_blk = jnp.concatenate([b_block(bb_re), b_block(bb_im)], axis=2).astype(BF16)
    c_blk = jnp.concatenate([c_block(c_re), c_block(-c_im)], axis=1).astype(BF16)
    bc = lambda a: jnp.broadcast_to(a.reshape(1, n_groups * state), (SUBLANES, n_groups * state))
    return bc(ab_re), bc(ab_im), b_blk, c_blk


def _lambda_init(layer):
    return 0.8 - 0.6 * math.exp(-0.3 * layer)


def kernel(x_prompt, x_sample, cache_k, cache_v, state_ssm_re, state_ssm_im, page_table, norm_mix, w_in, q_norm, k_norm, lambda_q1, lambda_k1, lambda_q2, lambda_k2, subln, ssm_a_re, ssm_a_im, ssm_b_re, ssm_b_im, ssm_c_re, ssm_c_im, ssm_d, ssm_log_dt, w_glu, w_proj_attn, w_proj_ssm, w_out, norm_ffn, ffn_w_gate, ffn_w_up, ffn_w_down, router, moe_w_gate, moe_w_up, moe_w_down):
    n_b, seq, d = x_prompt.shape
    db, t_new, _ = x_sample.shape
    depth = w_in.shape[0]
    n_heads = cache_k.shape[3]
    qk_w = n_heads * HEAD_LANES
    ssm_w = w_glu.shape[1]
    n_groups, state = ssm_a_re.shape[1:]
    n_state = n_groups * state
    page = cache_k.shape[2]
    tp = n_b * seq
    ts = db * t_new
    t_all = tp + ts
    assert n_b == SUBLANES and db % SUBLANES == 0 and ssm_w % SSM_BLOCK_CH == 0
    assert w_in.shape[2] == 3 * qk_w + ssm_w + 2 * d and ssm_w == qk_w

    tm = _tile(math.gcd(seq, ts), 1024)
    n_i = t_all // tm
    cache_k = cache_k.reshape(cache_k.shape[0], cache_k.shape[1], page * n_heads, HEAD_LANES)
    cache_v = cache_v.reshape(cache_v.shape[0], cache_v.shape[1], page * n_heads, HEAD_LANES)
    x_parts = [x_prompt.reshape(tp, d), x_sample.reshape(ts, d)]
    w_in, w_glu, w_proj_attn, w_proj_ssm, w_out, ffn_w_gate, ffn_w_up, ffn_w_down = (
        w.astype(BF16) for w in
        (w_in, w_glu, w_proj_attn, w_proj_ssm, w_out, ffn_w_gate, ffn_w_up, ffn_w_down))
    zeros_h0 = jnp.zeros((n_b, n_state), F32)
    scale = HEAD_DIM ** -0.5

    n_ip = tp // tm
    prow = lambda i: jnp.minimum(i, n_ip - 1)
    srow = lambda i: jnp.maximum(i - n_ip, 0)
    which_part = lambda pids: jnp.where(pids[0] < n_ip, 0, 1)

    def row_lhs(a):
        return (a, (tm, a.shape[1]), lambda i, j: (i, 0))

    def split_lhs(a_p, a_s):
        return [(a_p, (tm, a_p.shape[1]), lambda i, j: (prow(i), 0)),
                (a_s, (tm, a_s.shape[1]), lambda i, j: (srow(i), 0))]

    k_out, v_out, hrp_out, hip_out, hrs_out, his_out = [], [], [], [], [], []
    for layer in range(depth):
        lam_0 = _lambda_init(layer)
        lam = (jnp.exp(jnp.sum(lambda_q1[layer] * lambda_k1[layer]))
               - jnp.exp(jnp.sum(lambda_q2[layer] * lambda_k2[layer])) + lam_0).astype(F32)
        xn = rmsnorm_cast(x_parts if layer == 0 else [x], norm_mix[layer], _tile(ts, 512))

        tn = _tile(qk_w, 512)
        hpt = tn // HEAD_LANES
        nq_t, ns_t, ng_t = qk_w // tn, ssm_w // tn, 2 * d // tn
        gain_row = lambda g: (jnp.tile(g, tn // HEAD_DIM).reshape(1, tn), (1, tn), lambda i, j: (0, 0))

        def ranged_out(width, dtype, lo, n):
            return (jax.ShapeDtypeStruct((t_all, width), dtype), (tm, tn),
                    lambda i, j: (i, jnp.clip(j - lo, 0, n - 1)))

        head_rows_out = (jax.ShapeDtypeStruct((t_all * n_heads, HEAD_LANES), F32),
                         (tm * n_heads, HEAD_LANES), lambda i, j: (i, 0))

        def stores(j, lo, n, value_fn, tile_refs, head_rows_ref=None):
            @pl.when(jnp.logical_and(j >= lo, j < lo + n))
            def _():
                value = value_fn()
                for r in tile_refs:
                    r[...] = value.astype(r.dtype)
                if head_rows_ref is not None:
                    for hl in range(hpt):
                        rows = pl.ds((j - lo) * hpt + hl, tm, stride=n_heads)
                        head_rows_ref[rows, :] = value[:, hl * HEAD_LANES:(hl + 1) * HEAD_LANES]

        def ep_qkv(accs, ex, out, pids):
            j = pids[1]
            y = accs[0]

            def head_normed(gain_ref, mult):
                return y * lax.rsqrt(_segment_mean64(y * y) + EPS) * gain_ref[...] * mult

            qn_r, k32_r, knb_r, v32_r, vb_r = out
            stores(j, 0, nq_t, lambda: head_normed(ex[0], scale), [qn_r])
            stores(j, nq_t, nq_t, lambda: head_normed(ex[1], 1.0), [knb_r], k32_r)
            stores(j, 2 * nq_t, nq_t, lambda: y, [vb_r], v32_r)

        qn, k_rows, kn_b, v_rows, v_b = fused_matmul(
            "proj_qkv", (n_i, 3 * nq_t), [row_lhs(xn)],
            [(0, w_in, (None, d, tn), lambda i, j: (layer, 0, j))],
            [gain_row(q_norm[layer]), gain_row(k_norm[layer])],
            [ranged_out(qk_w, BF16, 0, nq_t),
             head_rows_out, ranged_out(qk_w, BF16, nq_t, nq_t),
             head_rows_out, ranged_out(qk_w, BF16, 2 * nq_t, nq_t)], ep_qkv)

        def ep_ug(accs, ex, out, pids):
            stores(pids[1], 0, ns_t, lambda: accs[0], [out[0]])
            stores(pids[1], ns_t, ng_t, lambda: jax.nn.sigmoid(accs[0]), [out[1]])

        u_all, gates = fused_matmul(
            "proj_ug", (n_i, ns_t + ng_t), [row_lhs(xn)],
            [(0, w_in, (None, d, tn), lambda i, j: (layer, 0, j + 3 * nq_t))], [],
            [ranged_out(ssm_w, F32, 0, ns_t), ranged_out(2 * d, F32, ns_t, ng_t)], ep_ug)

        out_scale = 1.0 - lam_0
        o_p = attn_prompt(lam, qn, kn_b, v_b, subln[layer], out_scale, n_b, seq, n_heads)
        k_s = k_rows[tp * n_heads:].reshape(db, t_new * n_heads, HEAD_LANES)
        v_s = v_rows[tp * n_heads:].reshape(db, t_new * n_heads, HEAD_LANES)
        q_s = qn[tp:].astype(F32).reshape(db, t_new, qk_w)
        o_s = attn_sample(lam, page_table, q_s, k_s, v_s, cache_k, cache_v, layer, subln[layer],
                          out_scale, n_heads).reshape(ts, qk_w)

        a_re, a_im, b_blk, c_blk = _ssm_weights(
            ssm_a_re[layer], ssm_a_im[layer], ssm_b_re[layer], ssm_b_im[layer],
            ssm_c_re[layer], ssm_c_im[layer], ssm_log_dt[layer])
        d_row = ssm_d[layer].reshape(1, ssm_w)
        u_p = u_all[:tp].reshape(n_b, seq, ssm_w).transpose(1, 0, 2).reshape(tp, ssm_w)
        u_s = u_all[tp:].reshape(db, t_new, ssm_w).transpose(1, 0, 2).reshape(ts, ssm_w)
        yg_p, hrp, hip = ssm_scan(u_p, zeros_h0, zeros_h0, a_re, a_im, b_blk, c_blk, d_row,
                                  n_b, _tile(seq, 256))
        yg_s, hrs, his = ssm_scan(u_s, state_ssm_re[layer].reshape(db, n_state),
                                  state_ssm_im[layer].reshape(db, n_state),
                                  a_re, a_im, b_blk, c_blk, d_row, db, t_new)
        yg_p = yg_p.reshape(seq, n_b, ssm_w).transpose(1, 0, 2).reshape(tp, ssm_w)
        yg_s = yg_s.reshape(t_new, db, ssm_w).transpose(1, 0, 2).reshape(ts, ssm_w)

        def ep_glu(accs, ex, out, pids):
            yg_tile = jnp.where(pids[0] < n_ip, ex[0][...], ex[1][...])
            out[0][...] = (yg_tile * jax.nn.sigmoid(accs[0])).astype(out[0].dtype)

        tn = _tile(ssm_w, 512)
        y_ssm, = fused_matmul(
            "ssm_glu", (n_i, ssm_w // tn), [split_lhs(yg_p, yg_s)],
            [(0, w_glu, (None, ssm_w, tn), lambda i, j: (layer, 0, j))],
            [(yg_p, (tm, tn), lambda i, j: (prow(i), j)),
             (yg_s, (tm, tn), lambda i, j: (srow(i), j))],
            [(jax.ShapeDtypeStruct((t_all, ssm_w), BF16), (tm, tn), lambda i, j: (i, j))], ep_glu,
            which=which_part)

        tn = _tile(d, 512)

        def ep_merge(accs, ex, out, pids):
            out[0][...] = (ex[0][...] * accs[0] + ex[1][...] * accs[1]).astype(out[0].dtype)

        merged, = fused_matmul(
            "merge", (n_i, d // tn), [split_lhs(o_p, o_s), row_lhs(y_ssm)],
            [(0, w_proj_attn, (None, qk_w, tn), lambda i, j: (layer, 0, j)),
             (1, w_proj_ssm, (None, ssm_w, tn), lambda i, j: (layer, 0, j))],
            [(gates, (tm, tn), lambda i, j: (i, j)),
             (gates, (tm, tn), lambda i, j: (i, j + d // tn))],
            [(jax.ShapeDtypeStruct((t_all, d), BF16), (tm, tn), lambda i, j: (i, j))], ep_merge,
            which=which_part)

        def ep_residual(accs, ex, out, pids):
            res = ex[0][...] if len(ex) == 1 else jnp.where(pids[0] < n_ip, ex[0][...], ex[1][...])
            out[0][...] = res + accs[0]

        if layer == 0:
            x_tiles = [(x_parts[0], (tm, tn), lambda i, j: (prow(i), j)),
                       (x_parts[1], (tm, tn), lambda i, j: (srow(i), j))]
        else:
            x_tiles = [(x, (tm, tn), lambda i, j: (i, j))]
        x, = fused_matmul(
            "out_proj", (n_i, d // tn), [row_lhs(merged)],
            [(0, w_out, (None, d, tn), lambda i, j: (layer, 0, j))], x_tiles,
            [(jax.ShapeDtypeStruct((t_all, d), F32), (tm, tn), lambda i, j: (i, j))], ep_residual)

        def ep_swiglu(accs, ex, out, pids):
            out[0][...] = (jax.nn.silu(accs[0]) * accs[1]).astype(out[0].dtype)

        jl = layer // 2
        if layer % 2 == 0:
            d_ff = ffn_w_gate.shape[2]
            tn = _tile(d_ff, 512)

            def norm_ffn_rows(block, ex):
                ms = jnp.mean(block * block, axis=-1, keepdims=True)
                return block * lax.rsqrt(ms + EPS) * ex[0][...]

            h, = fused_matmul(
                "ffn_gate_up", (n_i, d_ff // tn), [row_lhs(x)],
                [(0, ffn_w_gate, (None, d, tn), lambda i, j: (jl, 0, j)),
                 (0, ffn_w_up, (None, d, tn), lambda i, j: (jl, 0, j))],
                [(norm_ffn[layer].reshape(1, d), (1, d), lambda i, j: (0, 0))],
                [(jax.ShapeDtypeStruct((t_all, d_ff), BF16), (tm, tn), lambda i, j: (i, j))],
                ep_swiglu, prep={0: norm_ffn_rows})
            tn = _tile(d, 512)
            x, = fused_matmul(
                "ffn_down", (n_i, d // tn), [row_lhs(h)],
                [(0, ffn_w_down, (None, d_ff, tn), lambda i, j: (jl, 0, j))],
                [(x, (tm, tn), lambda i, j: (i, j))],
                [(jax.ShapeDtypeStruct((t_all, d), F32), (tm, tn), lambda i, j: (i, j))],
                ep_residual)
        else:
            xn2, sel = rmsnorm_router(x, norm_ffn[layer], router[jl], _tile(t_all, 512))
            last = layer == depth - 1
            x_out = moe_sparse(x, xn2, sel, moe_w_gate, moe_w_up, moe_w_down, jl,
                               split_rows=tp if last else None)
            x = x_out[0]

        k_out.append(k_rows)
        v_out.append(v_rows)
        hrp_out.append(hrp)
        hip_out.append(hip)
        hrs_out.append(hrs)
        his_out.append(his)

    def split(parts, lo, hi, shape):
        return jnp.stack([p[lo * n_heads:hi * n_heads].reshape(shape) for p in parts], axis=0)

    kv_p = (n_b, seq, n_heads, HEAD_LANES)
    kv_s = (db, t_new, n_heads, HEAD_LANES)
    st = lambda parts, n: jnp.stack([p.reshape(n, n_groups, state) for p in parts], axis=0)
    if depth % 2 == 0:
        x_p, x_s = x_out
    else:
        x_p, x_s = x[:tp], x[tp:]
    return (x_p.reshape(n_b, seq, d), x_s.reshape(db, t_new, d),
            split(k_out, 0, tp, kv_p), split(v_out, 0, tp, kv_p),
            st(hrp_out, n_b), st(hip_out, n_b),
            split(k_out, tp, t_all, kv_s), split(v_out, tp, t_all, kv_s),
            st(hrs_out, db), st(his_out, db))
```

```python
import functools
import math

import jax
import jax.numpy as jnp
from jax import lax
from jax.experimental import pallas as pl
from jax.experimental.pallas import tpu as pltpu

F32 = jnp.float32
BF16 = jnp.bfloat16

EPS = 1e-6
NEG_INF = -1e30
HEAD_DIM = 64
HEAD_LANES = 2 * HEAD_DIM
LANES = 128
MXU_DIM = 256
SUBLANES = 8
GROUP_CH = 16
SSM_BLOCK_CH = 256
TOP_K = 2
VMEM_LIMIT_BYTES = 56 * 1024 * 1024


def _tile(dim, pref):
    t = min(pref, dim)
    while dim % t:
        t //= 2
    return t


def _log2(n):
    assert n & (n - 1) == 0
    return n.bit_length() - 1


def _params(n_axes):
    return pltpu.CompilerParams(dimension_semantics=("arbitrary",) * n_axes,
                                vmem_limit_bytes=VMEM_LIMIT_BYTES)


def _rmsnorm_kernel(*refs, n_first):
    x_refs, g_ref, o_ref = refs[:-2], refs[-2], refs[-1]

    def emit(x_ref):
        x = x_ref[...]
        ms = jnp.mean(x * x, axis=-1, keepdims=True)
        o_ref[...] = (x * lax.rsqrt(ms + EPS) * g_ref[...]).astype(o_ref.dtype)

    if len(x_refs) == 1:
        emit(x_refs[0])
    else:
        i = pl.program_id(0)
        pl.when(i < n_first)(lambda: emit(x_refs[0]))
        pl.when(i >= n_first)(lambda: emit(x_refs[1]))


def rmsnorm_cast(parts, g, tm):
    d = parts[0].shape[1]
    t = sum(p.shape[0] for p in parts)
    n_first = parts[0].shape[0] // tm
    if len(parts) == 1:
        x_specs = [pl.BlockSpec((tm, d), lambda i: (i, 0))]
    else:
        assert len(parts) == 2 and all(p.shape[0] % tm == 0 for p in parts)
        x_specs = [pl.BlockSpec((tm, d), lambda i: (jnp.minimum(i, n_first - 1), 0)),
                   pl.BlockSpec((tm, d), lambda i: (jnp.maximum(i - n_first, 0), 0))]
    return pl.pallas_call(
        functools.partial(_rmsnorm_kernel, n_first=n_first),
        grid=(t // tm,),
        in_specs=x_specs + [pl.BlockSpec((1, d), lambda i: (0, 0))],
        out_specs=pl.BlockSpec((tm, d), lambda i: (i, 0)),
        out_shape=jax.ShapeDtypeStruct((t, d), BF16),
        compiler_params=_params(1),
        name="rmsnorm_cast",
    )(*parts, g.reshape(1, d))


def _split3(a):
    hi = a.astype(BF16)
    r1 = a - hi.astype(F32)
    mid = r1.astype(BF16)
    lo = (r1 - mid.astype(F32)).astype(BF16)
    return hi, mid, lo


def _rmsnorm_router_kernel(x_ref, g_ref, rw_ref, o_ref, gates_ref, *, n_experts):
    x = x_ref[...]
    ms = jnp.mean(x * x, axis=-1, keepdims=True)
    xn = x * lax.rsqrt(ms + EPS) * g_ref[...]
    o_ref[...] = xn.astype(o_ref.dtype)
    xh, xm, xl = _split3(xn)
    wh, wm, wl = _split3(rw_ref[...])
    dot = functools.partial(jnp.dot, preferred_element_type=F32)
    logits = (dot(xh, wh) + (dot(xh, wm) + dot(xm, wh))
              + (dot(xh, wl) + dot(xm, wm) + dot(xl, wh)))
    lane = lax.broadcasted_iota(jnp.int32, logits.shape, 1).astype(F32)
    logits = jnp.where(lane < n_experts, logits, -jnp.inf)
    m1 = jnp.max(logits, axis=-1, keepdims=True)
    i1 = jnp.min(jnp.where(logits == m1, lane, float(LANES)), axis=-1, keepdims=True)
    rest = jnp.where(lane == i1, -jnp.inf, logits)
    m2 = jnp.max(rest, axis=-1, keepdims=True)
    i2 = jnp.min(jnp.where(rest == m2, lane, float(LANES)), axis=-1, keepdims=True)
    e2 = jnp.exp(m2 - m1)
    g1 = 1.0 / (1.0 + e2)
    g2 = e2 / (1.0 + e2)
    gates_ref[...] = jnp.where(lane == 0.0, i1, jnp.where(lane == 1.0, i2, jnp.where(
        lane == 2.0, g1, jnp.where(lane == 3.0, g2, 0.0))))


def rmsnorm_router(x, g, router_w, tm):
    t, d = x.shape
    n_experts = router_w.shape[1]
    rw = jnp.zeros((d, LANES), F32).at[:, :n_experts].set(router_w)
    return pl.pallas_call(
        functools.partial(_rmsnorm_router_kernel, n_experts=n_experts),
        grid=(t // tm,),
        in_specs=[pl.BlockSpec((tm, d), lambda i: (i, 0)),
                  pl.BlockSpec((1, d), lambda i: (0, 0)),
                  pl.BlockSpec((d, LANES), lambda i: (0, 0))],
        out_specs=[pl.BlockSpec((tm, d), lambda i: (i, 0)),
                   pl.BlockSpec((tm, LANES), lambda i: (i, 0))],
        out_shape=[jax.ShapeDtypeStruct((t, d), F32),
                   jax.ShapeDtypeStruct((t, LANES), F32)],
        compiler_params=_params(1),
        name="rmsnorm_router",
    )(x, g.reshape(1, d), rw)


def _issue_row_gather(idx_ref, n_rows, src_hbm, dst_buf, slot, sem):
    assert n_rows % 2 == 0

    def body(pair, carry):
        for priority in range(2):
            r = 2 * pair + priority
            row = idx_ref[0, r]
            pltpu.make_async_copy(src_hbm.at[pl.ds(row, 1)], dst_buf.at[slot, pl.ds(r, 1)],
                                  sem.at[slot]).start(priority=priority)
        return carry
    lax.fori_loop(0, n_rows // 2, body, 0, unroll=4)


def _wait_row_gather(n_rows, src_hbm, dst_buf, slot, sem):
    pltpu.make_async_copy(src_hbm.at[pl.ds(0, n_rows)], dst_buf.at[slot], sem.at[slot]).wait()


def _gathered_rows(step, n_steps, first, idx_ref, idx_next_ref, src_hbm, buf, sem, n_rows):
    slot = step % 2

    @pl.when(first)
    def _():
        @pl.when(step == 0)
        def _():
            _issue_row_gather(idx_ref, n_rows, src_hbm, buf, 0, sem)

        _wait_row_gather(n_rows, src_hbm, buf, slot, sem)

        @pl.when(step + 1 < n_steps)
        def _():
            _issue_row_gather(idx_next_ref, n_rows, src_hbm, buf, 1 - slot, sem)

    return slot


def _moe_gather_kernel(idx_ref, idx_next_ref, xn_hbm, o_ref, buf, sem, *, tmb):
    bi = pl.program_id(0)
    slot = _gathered_rows(bi, pl.num_programs(0), bi >= 0, idx_ref, idx_next_ref, xn_hbm, buf, sem, tmb)
    o_ref[...] = buf[slot].astype(o_ref.dtype)


def _moe_gate_up_kernel(blk_e_ref, blk_rows_ref, x_ref, wg_ref, wu_ref, h_ref, *, tmb):
    del blk_e_ref
    bi = pl.program_id(0)
    wg = wg_ref[...].astype(BF16)
    wu = wu_ref[...].astype(BF16)

    def rows(rs):
        a = x_ref[rs, :]
        gate = jnp.dot(a, wg, preferred_element_type=F32)
        up = jnp.dot(a, wu, preferred_element_type=F32)
        h_ref[rs, :] = (jax.nn.silu(gate) * up).astype(h_ref.dtype)

    _by_valid_rows(blk_rows_ref[bi], tmb, rows, h_ref)


def _by_valid_rows(n_valid, tmb, compute, out_ref):
    half = tmb // 2
    zeros = lambda n: jnp.zeros((n, out_ref.shape[1]), out_ref.dtype)

    @pl.when(n_valid > half)
    def _():
        compute(slice(0, tmb))

    @pl.when(jnp.logical_and(n_valid > 0, n_valid <= half))
    def _():
        compute(slice(0, half))
        out_ref[half:tmb, :] = zeros(tmb - half)

    @pl.when(n_valid <= 0)
    def _():
        out_ref[...] = zeros(tmb)


def _moe_down_kernel(blk_e_ref, blk_rows_ref, h_ref, w_ref, y_ref, *, tmb):
    del blk_e_ref
    bi = pl.program_id(0)
    w = w_ref[...].astype(BF16)

    def rows(rs):
        y_ref[rs, :] = jnp.dot(h_ref[rs, :], w, preferred_element_type=F32)

    _by_valid_rows(blk_rows_ref[bi], tmb, rows, y_ref)


def _moe_combine_kernel(idx_ref, idx_next_ref, x_ref, sel_ref, ys_hbm, *rest, tc, n_first):
    o_refs, (buf, sem) = rest[:-2], rest[-2:]
    i = pl.program_id(0)
    slot = _gathered_rows(i, pl.num_programs(0), i >= 0, idx_ref, idx_next_ref, ys_hbm, buf, sem, 2 * tc)
    g1 = sel_ref[:, TOP_K:TOP_K + 1]
    g2 = sel_ref[:, TOP_K + 1:TOP_K + 2]
    y = x_ref[...] + (g1 * buf[slot, 0:tc, :] + g2 * buf[slot, tc:2 * tc, :])
    if len(o_refs) == 1:
        o_refs[0][...] = y
    else:
        @pl.when(i < n_first)
        def _():
            o_refs[0][...] = y

        @pl.when(i >= n_first)
        def _():
            o_refs[1][...] = y


def moe_sparse(x, xn, sel, w_gate, w_up, w_down, jl, split_rows=None):
    t, d = x.shape
    n_e, _, d_fe = w_gate.shape[1:]
    n_assign = TOP_K * t
    tmb = 1024 if n_assign >= 8 * 1024 else 256
    n_blk = n_assign // tmb + n_e
    n_rows = n_blk * tmb

    e_flat = sel[:, 0:TOP_K].astype(jnp.int32).reshape(n_assign)
    onehot = (e_flat[:, None] == jnp.arange(n_e, dtype=jnp.int32)[None, :]).astype(jnp.int32)
    counts = jnp.sum(onehot, axis=0)
    rank = jnp.sum((jnp.cumsum(onehot, axis=0) - onehot) * onehot, axis=1)
    blks_e = (counts + tmb - 1) // tmb
    blk_end = jnp.cumsum(blks_e)
    blk_start = blk_end - blks_e
    pos = blk_start[e_flat] * tmb + rank
    src_tok = (jnp.arange(n_rows, dtype=jnp.int32) % t).at[pos].set(
        jnp.arange(n_assign, dtype=jnp.int32) // TOP_K)
    blk_ids = jnp.arange(n_blk, dtype=jnp.int32)
    n_active = blk_end[-1]
    blk_e = jnp.sum((blk_ids[:, None] >= blk_end[None, :]).astype(jnp.int32), axis=1)
    blk_e = jnp.minimum(blk_e, blk_e[jnp.maximum(n_active - 1, 0)])
    blk_rows = jnp.clip(counts[blk_e] - (blk_ids - blk_start[blk_e]) * tmb, 0, tmb)
    blk_rows = jnp.where(blk_ids < n_active, blk_rows, 0).astype(jnp.int32)

    idx2d = src_tok.reshape(n_blk, 1, tmb)
    xs = pl.pallas_call(
        functools.partial(_moe_gather_kernel, tmb=tmb),
        grid=(n_blk,),
        in_specs=[pl.BlockSpec((None, 1, tmb), lambda bi: (bi, 0, 0), memory_space=pltpu.SMEM),
                  pl.BlockSpec((None, 1, tmb), lambda bi: (jnp.minimum(bi + 1, n_blk - 1), 0, 0),
                               memory_space=pltpu.SMEM),
                  pl.BlockSpec(memory_space=pl.ANY)],
        out_specs=pl.BlockSpec((tmb, d), lambda bi: (bi, 0)),
        out_shape=jax.ShapeDtypeStruct((n_rows, d), BF16),
        scratch_shapes=[pltpu.VMEM((2, tmb, d), F32),
                        pltpu.SemaphoreType.DMA((2,))],
        compiler_params=_params(1),
        name="moe_gather",
    )(idx2d, idx2d, xn)

    tn = _tile(d_fe, 256)
    nj = d_fe // tn
    col = lambda bi, j, be, br: jnp.where(br[bi] > 0, j, nj - 1)
    h = pl.pallas_call(
        functools.partial(_moe_gate_up_kernel, tmb=tmb),
        grid_spec=pltpu.PrefetchScalarGridSpec(
            num_scalar_prefetch=2,
            grid=(n_blk, nj),
            in_specs=[pl.BlockSpec((tmb, d), lambda bi, j, be, br: (bi, 0)),
                      pl.BlockSpec((None, None, d, tn), lambda bi, j, be, br: (jl, be[bi], 0, col(bi, j, be, br))),
                      pl.BlockSpec((None, None, d, tn), lambda bi, j, be, br: (jl, be[bi], 0, col(bi, j, be, br)))],
            out_specs=pl.BlockSpec((tmb, tn), lambda bi, j, be, br: (bi, j))),
        out_shape=jax.ShapeDtypeStruct((n_rows, d_fe), BF16),
        compiler_params=_params(2),
        name="moe_gate_up",
    )(blk_e, blk_rows, xs, w_gate, w_up)

    tn = _tile(d, 512)
    nj = d // tn
    ys = pl.pallas_call(
        functools.partial(_moe_down_kernel, tmb=tmb),
        grid_spec=pltpu.PrefetchScalarGridSpec(
            num_scalar_prefetch=2,
            grid=(n_blk, nj),
            in_specs=[pl.BlockSpec((tmb, d_fe), lambda bi, j, be, br: (bi, 0)),
                      pl.BlockSpec((None, None, d_fe, tn), lambda bi, j, be, br: (jl, be[bi], 0, col(bi, j, be, br)))],
            out_specs=pl.BlockSpec((tmb, tn), lambda bi, j, be, br: (bi, j))),
        out_shape=jax.ShapeDtypeStruct((n_rows, d), F32),
        compiler_params=_params(2),
        name="moe_down",
    )(blk_e, blk_rows, h, w_down)

    tc = _tile(t, 512)
    n_i = t // tc
    pos_blk = pos.reshape(n_i, tc, TOP_K).transpose(0, 2, 1).reshape(n_i, 1, TOP_K * tc)
    if split_rows is None:
        n_first = n_i
        out_specs = [pl.BlockSpec((tc, d), lambda i: (i, 0))]
        out_shape = [jax.ShapeDtypeStruct((t, d), F32)]
    else:
        n_first = split_rows // tc
        out_specs = [pl.BlockSpec((tc, d), lambda i: (jnp.minimum(i, n_first - 1), 0)),
                     pl.BlockSpec((tc, d), lambda i: (jnp.maximum(i - n_first, 0), 0))]
        out_shape = [jax.ShapeDtypeStruct((split_rows, d), F32),
                     jax.ShapeDtypeStruct((t - split_rows, d), F32)]
    return pl.pallas_call(
        functools.partial(_moe_combine_kernel, tc=tc, n_first=n_first),
        grid=(n_i,),
        in_specs=[pl.BlockSpec((None, 1, TOP_K * tc), lambda i: (i, 0, 0), memory_space=pltpu.SMEM),
                  pl.BlockSpec((None, 1, TOP_K * tc), lambda i: (jnp.minimum(i + 1, n_i - 1), 0, 0),
                               memory_space=pltpu.SMEM),
                  pl.BlockSpec((tc, d), lambda i: (i, 0)),
                  pl.BlockSpec((tc, LANES), lambda i: (i, 0)),
                  pl.BlockSpec(memory_space=pl.ANY)],
        out_specs=out_specs,
        out_shape=out_shape,
        scratch_shapes=[pltpu.VMEM((2, TOP_K * tc, d), F32),
                        pltpu.SemaphoreType.DMA((2,))],
        compiler_params=_params(1),
        name="moe_combine",
    )(pos_blk, pos_blk, x, sel, ys)


def fused_matmul(name, grid, lhs, terms, extras, outs, epilogue, which=None, prep=None):
    lhs = [alts if isinstance(alts, list) else [alts] for alts in lhs]
    flat_lhs = [alt for alts in lhs for alt in alts]
    first_ref = [sum(len(a) for a in lhs[:k]) for k in range(len(lhs))]
    n_lhs, n_terms, n_ex, n_out = len(flat_lhs), len(terms), len(extras), len(outs)
    staged = [k for k, alts in enumerate(lhs) if len(alts) > 1 or alts[0][0].dtype != BF16]

    def kernel(*refs):
        lhs_refs = refs[:n_lhs]
        w_refs = refs[n_lhs:n_lhs + n_terms]
        ex_refs = refs[n_lhs + n_terms:n_lhs + n_terms + n_ex]
        out_refs = refs[n_lhs + n_terms + n_ex:n_lhs + n_terms + n_ex + n_out]
        scr_refs = refs[n_lhs + n_terms + n_ex + n_out:]
        pids = [pl.program_id(a) for a in range(len(grid))]
        if staged:
            first = pids[1] == 0
            for p in pids[2:]:
                first = jnp.logical_and(first, p == 0)
            for s, k in enumerate(staged):
                for a in range(len(lhs[k])):
                    use = first if len(lhs[k]) == 1 else jnp.logical_and(first, which(pids) == a)

                    @pl.when(use)
                    def _(s=s, k=k, r=first_ref[k] + a):
                        block = lhs_refs[r][...]
                        if prep and k in prep:
                            block = prep[k](block.astype(F32), ex_refs)
                        scr_refs[s][...] = block.astype(BF16)

        accs = []
        for (li, _, _, _), w_ref in zip(terms, w_refs):
            a = scr_refs[staged.index(li)][...] if li in staged else lhs_refs[first_ref[li]][...]
            accs.append(jnp.dot(a, w_ref[...].astype(BF16), preferred_element_type=F32))
        epilogue(accs, ex_refs, out_refs, pids)

    in_specs = ([pl.BlockSpec(bs, im) for _, bs, im in flat_lhs]
                + [pl.BlockSpec(bs, im) for _, _, bs, im in terms]
                + [pl.BlockSpec(bs, im) for _, bs, im in extras])
    args = [a for a, _, _ in flat_lhs] + [w for _, w, _, _ in terms] + [a for a, _, _ in extras]
    scratch = [pltpu.VMEM(tuple(b for b in lhs[k][0][1] if b is not None), BF16) for k in staged]
    res = pl.pallas_call(
        kernel,
        grid=grid,
        in_specs=in_specs,
        out_specs=[pl.BlockSpec(bs, im) for _, bs, im in outs],
        out_shape=[sd for sd, _, _ in outs],
        scratch_shapes=scratch,
        compiler_params=_params(len(grid)),
        name=name,
    )(*args)
    return res


def _segment_mean64(sq):
    n = sq.shape[1]
    w = MXU_DIM if n % MXU_DIM == 0 else LANES
    r = lax.broadcasted_iota(jnp.int32, (w, w), 0) >> _log2(HEAD_DIM)
    c = lax.broadcasted_iota(jnp.int32, (w, w), 1) >> _log2(HEAD_DIM)
    ones = (r == c).astype(BF16)
    hi = sq.astype(BF16)
    lo = (sq - hi.astype(F32)).astype(BF16)
    dot = functools.partial(jnp.dot, preferred_element_type=F32)
    cols = []
    for j in range(n // w):
        sl = slice(j * w, (j + 1) * w)
        cols.append(dot(hi[:, sl], ones) + dot(lo[:, sl], ones))
    return jnp.concatenate(cols, axis=1) * (1.0 / HEAD_DIM)


ATTN_HEADS_PER_STEP = 4


def _attn_prompt_kernel(lam_ref, q_ref, k_ref, v_ref, g_ref, o_ref, vt_scr, *, tq, out_scale):
    qi = pl.program_id(2)
    lam = lam_ref[0]
    cols = 2 * tq
    n_h = q_ref.shape[1] // HEAD_LANES
    head = lambda h: slice(h * HEAD_LANES, (h + 1) * HEAD_LANES)

    @pl.when(qi == 0)
    def _():
        vt_scr[...] = v_ref[...].astype(F32).T.astype(BF16)

    lane = lax.broadcasted_iota(jnp.int32, (tq, HEAD_LANES), 1)
    qs = []
    for h in range(n_h):
        q = q_ref[:, head(h)].astype(F32)
        qs.append(jnp.concatenate([jnp.where(lane < HEAD_DIM, q, 0.0),
                                   jnp.where(lane >= HEAD_DIM, q, 0.0)], axis=0).astype(BF16))

    def block(kb, carry, diagonal):
        start = pl.multiple_of(kb * tq, tq)
        out = []
        for h in range(n_h):
            m, l, acc = carry[h]
            k = k_ref[pl.ds(start, tq), head(h)]
            s = lax.dot_general(k, qs[h], (((1,), (1,)), ((), ())), preferred_element_type=F32)
            if diagonal:
                key = lax.broadcasted_iota(jnp.int32, s.shape, 0)
                qry = lax.broadcasted_iota(jnp.int32, s.shape, 1)
                qry = jnp.where(qry >= tq, qry - tq, qry)
                s = jnp.where(key <= qry, s, NEG_INF)
            m_new = jnp.maximum(m, jnp.max(s, axis=0, keepdims=True))
            alpha = jnp.exp(m - m_new)
            p = jnp.exp(s - m_new)
            l = alpha * l + jnp.sum(p, axis=0, keepdims=True)
            pv = jnp.dot(vt_scr[head(h), pl.ds(start, tq)], p.astype(BF16),
                         preferred_element_type=F32)
            out.append((m_new, l, alpha * acc + pv))
        return tuple(out)

    carry = tuple((jnp.full((1, cols), NEG_INF, F32), jnp.zeros((1, cols), F32),
                   jnp.zeros((HEAD_LANES, cols), F32)) for _ in range(n_h))
    carry = lax.fori_loop(0, qi, lambda kb, c: block(kb, c, False), carry)
    carry = block(qi, carry, True)
    for h in range(n_h):
        _, l, acc = carry[h]
        ot = acc / l
        o = (ot[:, :tq] - lam * ot[:, tq:]).T
        o = o * lax.rsqrt(jnp.mean(o * o, axis=-1, keepdims=True) + EPS) * g_ref[...] * out_scale
        o_ref[:, head(h)] = o.astype(o_ref.dtype)


def attn_prompt(lam, qn, kb, vb, subln, out_scale, n_batch, seq, n_heads):
    tq = _tile(seq, 256)
    nq = seq // tq
    hw = ATTN_HEADS_PER_STEP * HEAD_LANES
    assert n_heads % ATTN_HEADS_PER_STEP == 0
    return pl.pallas_call(
        functools.partial(_attn_prompt_kernel, tq=tq, out_scale=out_scale),
        grid=(n_batch, n_heads // ATTN_HEADS_PER_STEP, nq),
        in_specs=[pl.BlockSpec(memory_space=pltpu.SMEM),
                  pl.BlockSpec((tq, hw), lambda b, h, i: (b * nq + i, h)),
                  pl.BlockSpec((seq, hw), lambda b, h, i: (b, h)),
                  pl.BlockSpec((seq, hw), lambda b, h, i: (b, h)),
                  pl.BlockSpec((1, HEAD_LANES), lambda b, h, i: (0, 0))],
        out_specs=pl.BlockSpec((tq, hw), lambda b, h, i: (b * nq + i, h)),
        out_shape=jax.ShapeDtypeStruct((n_batch * seq, n_heads * HEAD_LANES), BF16),
        scratch_shapes=[pltpu.VMEM((hw, seq), BF16)],
        compiler_params=_params(3),
        name="attn_prompt",
    )(lam.reshape(1), qn, kb, vb, subln.reshape(1, HEAD_LANES))


def _attn_sample_kernel(pt_ref, lam_ref, q_ref, kn_ref, vn_ref, *rest, n_heads, t_new, out_scale,
                        pages_per_step):
    del pt_ref
    kc_refs = rest[:pages_per_step]
    vc_refs = rest[pages_per_step:2 * pages_per_step]
    g_ref, o_ref, q_scr, bias_scr, s_scr, m_scr, l_scr, acc_scr = rest[2 * pages_per_step:]
    p = pl.program_id(1)
    n_steps = pl.num_programs(1)
    rows = 2 * n_heads * t_new
    page_rows = kc_refs[0].shape[0]
    nt = (((1,), (1,)), ((), ()))
    log_t, head_mask = _log2(t_new), n_heads - 1

    @pl.when(p == 0)
    def _():
        q = q_ref[0]
        lane = lax.broadcasted_iota(jnp.int32, (t_new, HEAD_LANES), 1)
        pieces = []
        for half in range(2):
            for h in range(n_heads):
                qh = q[:, h * HEAD_LANES:(h + 1) * HEAD_LANES]
                pieces.append(jnp.where((lane >= HEAD_DIM) == bool(half), qh, 0.0))
        q_scr[...] = jnp.concatenate(pieces, axis=0).astype(BF16)
        r = lax.broadcasted_iota(jnp.int32, (rows, page_rows), 0)
        c = lax.broadcasted_iota(jnp.int32, (rows, page_rows), 1)
        bias_scr[...] = jnp.where(((r >> log_t) & head_mask) == (c & head_mask), 0.0, NEG_INF)
        m_scr[...] = jnp.full(m_scr.shape, NEG_INF, F32)
        l_scr[...] = jnp.zeros(l_scr.shape, F32)
        acc_scr[...] = jnp.zeros(acc_scr.shape, F32)

    def update(keys, values, bias):
        q = q_scr[...]
        m_old = m_scr[...]
        m_new = m_old
        for i, k in enumerate(keys):
            s = lax.dot_general(q, k(), nt, preferred_element_type=F32) + bias
            s_scr[i, :, 0:s.shape[1]] = s
            m_new = jnp.maximum(m_new, jnp.max(s, axis=-1, keepdims=True))
        alpha = jnp.exp(m_old - m_new)
        l = alpha * l_scr[...]
        acc = alpha * acc_scr[...]
        for i, v in enumerate(values):
            pe = jnp.exp(s_scr[i, :, 0:bias.shape[1]] - m_new)
            l = l + jnp.sum(pe, axis=-1, keepdims=True)
            acc = acc + jnp.dot(pe.astype(BF16), v(), preferred_element_type=F32)
        l_scr[...] = l
        acc_scr[...] = acc
        m_scr[...] = m_new

    update([lambda r=r: r[...].astype(BF16) for r in kc_refs],
           [lambda r=r: r[...].astype(BF16) for r in vc_refs], bias_scr[...])

    @pl.when(p == n_steps - 1)
    def _():
        new_rows = t_new * n_heads
        pad = jnp.zeros((LANES - new_rows, HEAD_LANES), F32)
        k_new = jnp.concatenate([kn_ref[0], pad], axis=0).astype(BF16)
        v_new = jnp.concatenate([vn_ref[0], pad], axis=0).astype(BF16)
        r = lax.broadcasted_iota(jnp.int32, (rows, LANES), 0)
        c = lax.broadcasted_iota(jnp.int32, (rows, LANES), 1)
        ok = jnp.logical_and(((r >> log_t) & head_mask) == (c & head_mask),
                             (c >> _log2(n_heads)) <= (r & (t_new - 1)))
        update([lambda: k_new], [lambda: v_new], jnp.where(ok, 0.0, NEG_INF))
        lam = lam_ref[0]
        inv_l = 1.0 / l_scr[...]
        outs = []
        for h in range(n_heads):
            r1 = slice(h * t_new, (h + 1) * t_new)
            r2 = slice((n_heads + h) * t_new, (n_heads + h + 1) * t_new)
            o = acc_scr[r1, :] * inv_l[r1] - lam * (acc_scr[r2, :] * inv_l[r2])
            o = o * lax.rsqrt(jnp.mean(o * o, axis=-1, keepdims=True) + EPS) * g_ref[...] * out_scale
            outs.append(o)
        o_ref[0] = jnp.concatenate(outs, axis=1).astype(o_ref.dtype)


def attn_sample(lam, page_table, q_s, k_s, v_s, cache_k, cache_v, layer, subln, out_scale, n_heads):
    db, t_new, width = q_s.shape
    page_rows = cache_k.shape[2]
    new_rows = t_new * n_heads
    n_pages = page_table.shape[1]
    rows = 2 * n_heads * t_new
    assert new_rows <= LANES
    pps = _tile(n_pages, 8)

    def page_spec(g):
        return pl.BlockSpec((None, None, page_rows, HEAD_LANES),
                            lambda b, p, pt: (layer, pt[b, p * pps + g], 0, 0))

    grid_spec = pltpu.PrefetchScalarGridSpec(
        num_scalar_prefetch=1,
        grid=(db, n_pages // pps),
        in_specs=([pl.BlockSpec(memory_space=pltpu.SMEM),
                   pl.BlockSpec((1, t_new, width), lambda b, p, pt: (b, 0, 0)),
                   pl.BlockSpec((1, new_rows, HEAD_LANES), lambda b, p, pt: (b, 0, 0)),
                   pl.BlockSpec((1, new_rows, HEAD_LANES), lambda b, p, pt: (b, 0, 0))]
                  + [page_spec(g) for g in range(pps)] + [page_spec(g) for g in range(pps)]
                  + [pl.BlockSpec((1, HEAD_LANES), lambda b, p, pt: (0, 0))]),
        out_specs=pl.BlockSpec((1, t_new, width), lambda b, p, pt: (b, 0, 0)),
        scratch_shapes=[pltpu.VMEM((rows, HEAD_LANES), BF16),
                        pltpu.VMEM((rows, page_rows), F32),
                        pltpu.VMEM((pps, rows, page_rows), F32),
                        pltpu.VMEM((rows, 1), F32),
                        pltpu.VMEM((rows, 1), F32),
                        pltpu.VMEM((rows, HEAD_LANES), F32)],
    )
    return pl.pallas_call(
        functools.partial(_attn_sample_kernel, n_heads=n_heads, t_new=t_new, out_scale=out_scale,
                          pages_per_step=pps),
        grid_spec=grid_spec,
        out_shape=jax.ShapeDtypeStruct((db, t_new, width), F32),
        compiler_params=_params(2),
        name="attn_sample",
    )(page_table, lam.reshape(1), q_s, k_s, v_s, *([cache_k] * pps), *([cache_v] * pps),
      subln.reshape(1, HEAD_LANES))


def _ssm_kernel(u_ref, h0re_ref, h0im_ref, are_ref, aim_ref, b_ref, c_ref, d_ref,
                y_ref, hre_ref, him_ref, bu_scr, *, nb, tc):
    c_idx = pl.program_id(1)
    sw = are_ref.shape[1]

    @pl.when(c_idx == 0)
    def _():
        hre_ref[...] = h0re_ref[...]
        him_ref[...] = h0im_ref[...]

    u = u_ref[...]
    bu_scr[...] = jnp.dot(u.astype(BF16), b_ref[...], preferred_element_type=F32)
    a_re = are_ref[...]
    a_im = aim_ref[...]
    for r in range(nb // SUBLANES):
        rs = slice(r * SUBLANES, (r + 1) * SUBLANES)

        def body(t, carry, r=r):
            h_re, h_im = carry
            row = pl.multiple_of(t * nb + r * SUBLANES, SUBLANES)
            n_re = a_re * h_re - a_im * h_im + bu_scr[pl.ds(row, SUBLANES), 0:sw]
            n_im = a_re * h_im + a_im * h_re + bu_scr[pl.ds(row, SUBLANES), sw:2 * sw]
            bu_scr[pl.ds(row, SUBLANES), 0:sw] = n_re
            bu_scr[pl.ds(row, SUBLANES), sw:2 * sw] = n_im
            return n_re, n_im

        h_re, h_im = lax.fori_loop(0, tc, body, (hre_ref[rs, :], him_ref[rs, :]))
        hre_ref[rs, :] = h_re
        him_ref[rs, :] = h_im
    y = jnp.dot(bu_scr[...].astype(BF16), c_ref[...], preferred_element_type=F32) + d_ref[...] * u
    y_ref[...] = jax.nn.gelu(y)


def ssm_scan(u_tm, h0_re, h0_im, a_re, a_im, b_blk, c_blk, d, nb, tc):
    rows, ch = u_tm.shape
    n_gb = ch // SSM_BLOCK_CH
    sw = h0_re.shape[1] // n_gb
    n_chunks = rows // (tc * nb)
    blk_rows = tc * nb
    return pl.pallas_call(
        functools.partial(_ssm_kernel, nb=nb, tc=tc),
        grid=(n_gb, n_chunks),
        in_specs=[pl.BlockSpec((blk_rows, SSM_BLOCK_CH), lambda g, c: (c, g)),
                  pl.BlockSpec((nb, sw), lambda g, c: (0, g)),
                  pl.BlockSpec((nb, sw), lambda g, c: (0, g)),
                  pl.BlockSpec((SUBLANES, sw), lambda g, c: (0, g)),
                  pl.BlockSpec((SUBLANES, sw), lambda g, c: (0, g)),
                  pl.BlockSpec((None, SSM_BLOCK_CH, 2 * sw), lambda g, c: (g, 0, 0)),
                  pl.BlockSpec((None, 2 * sw, SSM_BLOCK_CH), lambda g, c: (g, 0, 0)),
                  pl.BlockSpec((1, SSM_BLOCK_CH), lambda g, c: (0, g))],
        out_specs=[pl.BlockSpec((blk_rows, SSM_BLOCK_CH), lambda g, c: (c, g)),
                   pl.BlockSpec((nb, sw), lambda g, c: (0, g)),
                   pl.BlockSpec((nb, sw), lambda g, c: (0, g))],
        out_shape=[jax.ShapeDtypeStruct((rows, ch), F32),
                   jax.ShapeDtypeStruct(h0_re.shape, F32),
                   jax.ShapeDtypeStruct(h0_im.shape, F32)],
        scratch_shapes=[pltpu.VMEM((blk_rows, 2 * sw), F32)],
        compiler_params=_params(2),
        name="ssm_scan",
    )(u_tm, h0_re, h0_im, a_re, a_im, b_blk, c_blk, d)


def _ssm_weights(a_re, a_im, b_re, b_im, c_re, c_im, log_dt):
    n_groups, state = a_re.shape
    gpb = SSM_BLOCK_CH // GROUP_CH
    n_gb = n_groups // gpb
    dt = jnp.exp(log_dt)[:, None]
    mag = jnp.exp(dt * a_re)
    ab_re = mag * jnp.cos(dt * a_im)
    ab_im = mag * jnp.sin(dt * a_im)
    den = a_re * a_re + a_im * a_im
    nr = ab_re - 1.0
    coef_re = (nr * a_re + ab_im * a_im) / den
    coef_im = (ab_im * a_re - nr * a_im) / den
    bb_re = coef_re[..., None] * b_re - coef_im[..., None] * b_im
    bb_im = coef_re[..., None] * b_im + coef_im[..., None] * b_re
    eye = jnp.eye(gpb, dtype=F32)

    def b_block(bb):
        bb = bb.reshape(n_gb, gpb, state, GROUP_CH)
        return jnp.einsum("bgpc,gh->bgchp", bb, eye).reshape(n_gb, gpb * GROUP_CH, gpb * state)

    def c_block(cc):
        cc = cc.reshape(n_gb, gpb, GROUP_CH, state)
        return jnp.einsum("bgcp,gh->bgphc", cc, eye).reshape(n_gb, gpb * state, gpb * GROUP_CH)

    b_blk = jnp.concatenate([b_block(bb_re), b_block(bb_im)], axis=2).astype(BF16)
    c_blk = jnp.concatenate([c_block(c_re), c_block(-c_im)], axis=1).astype(BF16)
    bc = lambda a: jnp.broadcast_to(a.reshape(1, n_groups * state), (SUBLANES, n_groups * state))
    return bc(ab_re), bc(ab_im), b_blk, c_blk


def _lambda_init(layer):
    return 0.8 - 0.6 * math.exp(-0.3 * layer)


def kernel(x_prompt, x_sample, cache_k, cache_v, state_ssm_re, state_ssm_im, page_table, norm_mix, w_in, q_norm, k_norm, lambda_q1, lambda_k1, lambda_q2, lambda_k2, subln, ssm_a_re, ssm_a_im, ssm_b_re, ssm_b_im, ssm_c_re, ssm_c_im, ssm_d, ssm_log_dt, w_glu, w_proj_attn, w_proj_ssm, w_out, norm_ffn, ffn_w_gate, ffn_w_up, ffn_w_down, router, moe_w_gate, moe_w_up, moe_w_down):
    n_b, seq, d = x_prompt.shape
    db, t_new, _ = x_sample.shape
    depth = w_in.shape[0]
    n_heads = cache_k.shape[3]
    qk_w = n_heads * HEAD_LANES
    ssm_w = w_glu.shape[1]
    n_groups, state = ssm_a_re.shape[1:]
    n_state = n_groups * state
    page = cache_k.shape[2]
    tp = n_b * seq
    ts = db * t_new
    t_all = tp + ts
    assert n_b == SUBLANES and db % SUBLANES == 0 and ssm_w % SSM_BLOCK_CH == 0
    assert w_in.shape[2] == 3 * qk_w + ssm_w + 2 * d and ssm_w == qk_w

    tm = _tile(math.gcd(seq, ts), 1024)
    n_i = t_all // tm
    cache_k = cache_k.reshape(cache_k.shape[0], cache_k.shape[1], page * n_heads, HEAD_LANES)
    cache_v = cache_v.reshape(cache_v.shape[0], cache_v.shape[1], page * n_heads, HEAD_LANES)
    x_parts = [x_prompt.reshape(tp, d), x_sample.reshape(ts, d)]
    w_in, w_glu, w_proj_attn, w_proj_ssm, w_out, ffn_w_gate, ffn_w_up, ffn_w_down = (
        w.astype(BF16) for w in
        (w_in, w_glu, w_proj_attn, w_proj_ssm, w_out, ffn_w_gate, ffn_w_up, ffn_w_down))
    zeros_h0 = jnp.zeros((n_b, n_state), F32)
    scale = HEAD_DIM ** -0.5

    n_ip = tp // tm
    prow = lambda i: jnp.minimum(i, n_ip - 1)
    srow = lambda i: jnp.maximum(i - n_ip, 0)
    which_part = lambda pids: jnp.where(pids[0] < n_ip, 0, 1)

    def row_lhs(a):
        return (a, (tm, a.shape[1]), lambda i, j: (i, 0))

    def split_lhs(a_p, a_s):
        return [(a_p, (tm, a_p.shape[1]), lambda i, j: (prow(i), 0)),
                (a_s, (tm, a_s.shape[1]), lambda i, j: (srow(i), 0))]

    k_out, v_out, hrp_out, hip_out, hrs_out, his_out = [], [], [], [], [], []
    for layer in range(depth):
        lam_0 = _lambda_init(layer)
        lam = (jnp.exp(jnp.sum(lambda_q1[layer] * lambda_k1[layer]))
               - jnp.exp(jnp.sum(lambda_q2[layer] * lambda_k2[layer])) + lam_0).astype(F32)
        xn = rmsnorm_cast(x_parts if layer == 0 else [x], norm_mix[layer], _tile(ts, 512))

        tn = _tile(qk_w, 512)
        hpt = tn // HEAD_LANES
        nq_t, ns_t, ng_t = qk_w // tn, ssm_w // tn, 2 * d // tn
        gain_row = lambda g: (jnp.tile(g, tn // HEAD_DIM).reshape(1, tn), (1, tn), lambda i, j: (0, 0))

        def ranged_out(width, dtype, lo, n):
            return (jax.ShapeDtypeStruct((t_all, width), dtype), (tm, tn),
                    lambda i, j: (i, jnp.clip(j - lo, 0, n - 1)))

        head_rows_out = (jax.ShapeDtypeStruct((t_all * n_heads, HEAD_LANES), F32),
                         (tm * n_heads, HEAD_LANES), lambda i, j: (i, 0))

        def stores(j, lo, n, value_fn, tile_refs, head_rows_ref=None):
            @pl.when(jnp.logical_and(j >= lo, j < lo + n))
            def _():
                value = value_fn()
                for r in tile_refs:
                    r[...] = value.astype(r.dtype)
                if head_rows_ref is not None:
                    for hl in range(hpt):
                        rows = pl.ds((j - lo) * hpt + hl, tm, stride=n_heads)
                        head_rows_ref[rows, :] = value[:, hl * HEAD_LANES:(hl + 1) * HEAD_LANES]

        def ep_qkv(accs, ex, out, pids):
            j = pids[1]
            y = accs[0]

            def head_normed(gain_ref, mult):
                return y * lax.rsqrt(_segment_mean64(y * y) + EPS) * gain_ref[...] * mult

            qn_r, k32_r, knb_r, v32_r, vb_r = out
            stores(j, 0, nq_t, lambda: head_normed(ex[0], scale), [qn_r])
            stores(j, nq_t, nq_t, lambda: head_normed(ex[1], 1.0), [knb_r], k32_r)
            stores(j, 2 * nq_t, nq_t, lambda: y, [vb_r], v32_r)

        qn, k_rows, kn_b, v_rows, v_b = fused_matmul(
            "proj_qkv", (n_i, 3 * nq_t), [row_lhs(xn)],
            [(0, w_in, (None, d, tn), lambda i, j: (layer, 0, j))],
            [gain_row(q_norm[layer]), gain_row(k_norm[layer])],
            [ranged_out(qk_w, BF16, 0, nq_t),
             head_rows_out, ranged_out(qk_w, BF16, nq_t, nq_t),
             head_rows_out, ranged_out(qk_w, BF16, 2 * nq_t, nq_t)], ep_qkv)

        def ep_ug(accs, ex, out, pids):
            stores(pids[1], 0, ns_t, lambda: accs[0], [out[0]])
            stores(pids[1], ns_t, ng_t, lambda: jax.nn.sigmoid(accs[0]), [out[1]])

        u_all, gates = fused_matmul(
            "proj_ug", (n_i, ns_t + ng_t), [row_lhs(xn)],
            [(0, w_in, (None, d, tn), lambda i, j: (layer, 0, j + 3 * nq_t))], [],
            [ranged_out(ssm_w, F32, 0, ns_t), ranged_out(2 * d, F32, ns_t, ng_t)], ep_ug)

        out_scale = 1.0 - lam_0
        o_p = attn_prompt(lam, qn, kn_b, v_b, subln[layer], out_scale, n_b, seq, n_heads)
        k_s = k_rows[tp * n_heads:].reshape(db, t_new * n_heads, HEAD_LANES)
        v_s = v_rows[tp * n_heads:].reshape(db, t_new * n_heads, HEAD_LANES)
        q_s = qn[tp:].astype(F32).reshape(db, t_new, qk_w)
        o_s = attn_sample(lam, page_table, q_s, k_s, v_s, cache_k, cache_v, layer, subln[layer],
                          out_scale, n_heads).reshape(ts, qk_w)

        a_re, a_im, b_blk, c_blk = _ssm_weights(
            ssm_a_re[layer], ssm_a_im[layer], ssm_b_re[layer], ssm_b_im[layer],
            ssm_c_re[layer], ssm_c_im[layer], ssm_log_dt[layer])
        d_row = ssm_d[layer].reshape(1, ssm_w)
        u_p = u_all[:tp].reshape(n_b, seq, ssm_w).transpose(1, 0, 2).reshape(tp, ssm_w)
        u_s = u_all[tp:].reshape(db, t_new, ssm_w).transpose(1, 0, 2).reshape(ts, ssm_w)
        yg_p, hrp, hip = ssm_scan(u_p, zeros_h0, zeros_h0, a_re, a_im, b_blk, c_blk, d_row,
                                  n_b, _tile(seq, 256))
        yg_s, hrs, his = ssm_scan(u_s, state_ssm_re[layer].reshape(db, n_state),
                                  state_ssm_im[layer].reshape(db, n_state),
                                  a_re, a_im, b_blk, c_blk, d_row, db, t_new)
        yg_p = yg_p.reshape(seq, n_b, ssm_w).transpose(1, 0, 2).reshape(tp, ssm_w)
        yg_s = yg_s.reshape(t_new, db, ssm_w).transpose(1, 0, 2).reshape(ts, ssm_w)

        def ep_glu(accs, ex, out, pids):
            yg_tile = jnp.where(pids[0] < n_ip, ex[0][...], ex[1][...])
            out[0][...] = (yg_tile * jax.nn.sigmoid(accs[0])).astype(out[0].dtype)

        tn = _tile(ssm_w, 512)
        y_ssm, = fused_matmul(
            "ssm_glu", (n_i, ssm_w // tn), [split_lhs(yg_p, yg_s)],
            [(0, w_glu, (None, ssm_w, tn), lambda i, j: (layer, 0, j))],
            [(yg_p, (tm, tn), lambda i, j: (prow(i), j)),
             (yg_s, (tm, tn), lambda i, j: (srow(i), j))],
            [(jax.ShapeDtypeStruct((t_all, ssm_w), BF16), (tm, tn), lambda i, j: (i, j))], ep_glu,
            which=which_part)

        tn = _tile(d, 512)

        def ep_merge(accs, ex, out, pids):
            out[0][...] = (ex[0][...] * accs[0] + ex[1][...] * accs[1]).astype(out[0].dtype)

        merged, = fused_matmul(
            "merge", (n_i, d // tn), [split_lhs(o_p, o_s), row_lhs(y_ssm)],
            [(0, w_proj_attn, (None, qk_w, tn), lambda i, j: (layer, 0, j)),
             (1, w_proj_ssm, (None, ssm_w, tn), lambda i, j: (layer, 0, j))],
            [(gates, (tm, tn), lambda i, j: (i, j)),
             (gates, (tm, tn), lambda i, j: (i, j + d // tn))],
            [(jax.ShapeDtypeStruct((t_all, d), BF16), (tm, tn), lambda i, j: (i, j))], ep_merge,
            which=which_part)

        def ep_residual(accs, ex, out, pids):
            res = ex[0][...] if len(ex) == 1 else jnp.where(pids[0] < n_ip, ex[0][...], ex[1][...])
            out[0][...] = res + accs[0]

        if layer == 0:
            x_tiles = [(x_parts[0], (tm, tn), lambda i, j: (prow(i), j)),
                       (x_parts[1], (tm, tn), lambda i, j: (srow(i), j))]
        else:
            x_tiles = [(x, (tm, tn), lambda i, j: (i, j))]
        x, = fused_matmul(
            "out_proj", (n_i, d // tn), [row_lhs(merged)],
            [(0, w_out, (None, d, tn), lambda i, j: (layer, 0, j))], x_tiles,
            [(jax.ShapeDtypeStruct((t_all, d), F32), (tm, tn), lambda i, j: (i, j))], ep_residual)

        def ep_swiglu(accs, ex, out, pids):
            out[0][...] = (jax.nn.silu(accs[0]) * accs[1]).astype(out[0].dtype)

        jl = layer // 2
        if layer % 2 == 0:
            d_ff = ffn_w_gate.shape[2]
            tn = _tile(d_ff, 512)

            def norm_ffn_rows(block, ex):
                ms = jnp.mean(block * block, axis=-1, keepdims=True)
                return block * lax.rsqrt(ms + EPS) * ex[0][...]

            h, = fused_matmul(
                "ffn_gate_up", (n_i, d_ff // tn), [row_lhs(x)],
                [(0, ffn_w_gate, (None, d, tn), lambda i, j: (jl, 0, j)),
                 (0, ffn_w_up, (None, d, tn), lambda i, j: (jl, 0, j))],
                [(norm_ffn[layer].reshape(1, d), (1, d), lambda i, j: (0, 0))],
                [(jax.ShapeDtypeStruct((t_all, d_ff), BF16), (tm, tn), lambda i, j: (i, j))],
                ep_swiglu, prep={0: norm_ffn_rows})
            tn = _tile(d, 512)
            x, = fused_matmul(
                "ffn_down", (n_i, d // tn), [row_lhs(h)],
                [(0, ffn_w_down, (None, d_ff, tn), lambda i, j: (jl, 0, j))],
                [(x, (tm, tn), lambda i, j: (i, j))],
                [(jax.ShapeDtypeStruct((t_all, d), F32), (tm, tn), lambda i, j: (i, j))],
                ep_residual)
        else:
            xn2, sel = rmsnorm_router(x, norm_ffn[layer], router[jl], _tile(t_all, 512))
            last = layer == depth - 1
            x_out = moe_sparse(x, xn2, sel, moe_w_gate, moe_w_up, moe_w_down, jl,
                               split_rows=tp if last else None)
            x = x_out[0]

        k_out.append(k_rows)
        v_out.append(v_rows)
        hrp_out.append(hrp)
        hip_out.append(hip)
        hrs_out.append(hrs)
        his_out.append(his)

    def split(parts, lo, hi, shape):
        return jnp.stack([p[lo * n_heads:hi * n_heads].reshape(shape) for p in parts], axis=0)

    kv_p = (n_b, seq, n_heads, HEAD_LANES)
    kv_s = (db, t_new, n_heads, HEAD_LANES)
    st = lambda parts, n: jnp.stack([p.reshape(n, n_groups, state) for p in parts], axis=0)
    if depth % 2 == 0:
        x_p, x_s = x_out
    else:
        x_p, x_s = x[:tp], x[tp:]
    return (x_p.reshape(n_b, seq, d), x_s.reshape(db, t_new, d),
            split(k_out, 0, tp, kv_p), split(v_out, 0, tp, kv_p),
            st(hrp_out, n_b), st(hip_out, n_b),
            split(k_out, tp, t_all, kv_s), split(v_out, tp, t_all, kv_s),
            st(hrs_out, db), st(his_out, db))
```
